```python
import math
import numpy as np
import jax, jax.numpy as jnp
from jax import lax

D_MODEL = 1024
BATCH = 8
SEQ = 2048
DEPTH = 2

N_HEADS = 8
HEAD_DIM = 64
N_KV_GROUPS = 2
HEADS_PER_GROUP = N_HEADS // N_KV_GROUPS
ATTN_WIDTH = N_HEADS * HEAD_DIM
KV_WIDTH = N_KV_GROUPS * HEAD_DIM
CMP_BLOCK = 32
CMP_STRIDE = 16
CMP_HIDDEN = 128
SEL_BLOCK = 64
SEL_TOP = 16
WINDOW = 512
Q_CHUNK = 64
FORCE_BONUS = 1.0e4
NEG_INF = -1.0e30
CONV_CH = 512
CONV_WIDTH = 31
FFN_DIM = 2816
FFN_CONV_WIDTH = 3
NORM_EPS = 1e-6

IN_SIZES = (ATTN_WIDTH, KV_WIDTH, KV_WIDTH, KV_WIDTH, KV_WIDTH, KV_WIDTH, KV_WIDTH,
            3 * N_HEADS, 2 * CONV_CH, 2 * D_MODEL)
N_IN = sum(IN_SIZES)

kernel_name = "hybrid_nsa_conformer_convffn"


def rms_norm(x, g):
    xf = x.astype(jnp.float32)
    y = xf * lax.rsqrt(jnp.mean(xf * xf, axis=-1, keepdims=True) + NORM_EPS)
    return (y * g.astype(jnp.float32)).astype(x.dtype)


def layer_norm(x, g, b):
    xf = x.astype(jnp.float32)
    mu = jnp.mean(xf, axis=-1, keepdims=True)
    var = jnp.mean(jnp.square(xf - mu), axis=-1, keepdims=True)
    y = (xf - mu) * lax.rsqrt(var + NORM_EPS)
    return (y * g.astype(jnp.float32) + b.astype(jnp.float32)).astype(x.dtype)


def causal_dwconv(x, w):
    k, c = w.shape
    return lax.conv_general_dilated(x, w[:, None, :], window_strides=(1,), padding=[(k - 1, 0)],
                                    dimension_numbers=('NWC', 'WIO', 'NWC'), feature_group_count=c)


def alibi_slopes():
    return jnp.power(2.0, -8.0 * jnp.arange(1, N_HEADS + 1, dtype=jnp.float32) / N_HEADS)


def compress(kv, pe, w1, w2):
    b, s, g, dh = kv.shape
    ncmp = (s - CMP_BLOCK) // CMP_STRIDE + 1
    idx = np.arange(ncmp)[:, None] * CMP_STRIDE + np.arange(CMP_BLOCK)[None, :]
    blk = kv[:, idx] + pe[None, None, :, None, :]
    blk = jnp.transpose(blk, (0, 1, 3, 2, 4)).reshape(b, ncmp, g, CMP_BLOCK * dh)
    return jax.nn.gelu(blk @ w1) @ w2


def cmp_to_sel_map(s):
    ncmp = (s - CMP_BLOCK) // CMP_STRIDE + 1
    nsel = s // SEL_BLOCK
    cs = np.arange(ncmp) * CMP_STRIDE
    ce = cs + CMP_BLOCK - 1
    ss = np.arange(nsel) * SEL_BLOCK
    se = ss + SEL_BLOCK - 1
    ov = np.minimum(ce[:, None], se[None, :]) - np.maximum(cs[:, None], ss[None, :]) + 1
    return jnp.asarray(np.clip(ov, 0, None).astype(np.float32) / CMP_BLOCK)


def nsa_attention(q, k_cmp, v_cmp, k_sel, v_sel, k_win, v_win, gate_logits, pe_k, pe_v, ck1, ck2, cv1, cv2):
    b, s, _ = q.shape
    G, J, Dh = N_KV_GROUPS, HEADS_PER_GROUP, HEAD_DIM
    f32 = jnp.float32
    q = q.reshape(b, s, G, J, Dh)
    kv = lambda t: t.reshape(b, s, G, Dh)
    k_cmp, v_cmp, k_sel, v_sel, k_win, v_win = map(kv, (k_cmp, v_cmp, k_sel, v_sel, k_win, v_win))
    slopes = alibi_slopes().reshape(G, J)
    pos = jnp.arange(s)
    scale = HEAD_DIM ** -0.5

    kc = compress(k_cmp, pe_k, ck1, ck2)
    vc = compress(v_cmp, pe_v, cv1, cv2)
    ncmp = kc.shape[1]
    c_end = jnp.arange(ncmp) * CMP_STRIDE + CMP_BLOCK - 1
    dist_c = pos[:, None] - c_end[None, :]
    valid_c = dist_c >= 0
    s_c = jnp.einsum('btgjd,bngd->bgjtn', q, kc).astype(f32) * scale - slopes[:, :, None, None] * dist_c
    s_c = jnp.where(valid_c, s_c, NEG_INF)
    p_c = jax.nn.softmax(s_c, axis=-1) * valid_c
    o_c = jnp.einsum('bgjtn,bngd->btgjd', p_c.astype(vc.dtype), vc)

    nsel = s // SEL_BLOCK
    n_top = min(SEL_TOP, nsel)
    imp = jnp.einsum('bgjtn,nk->bgtk', p_c, cmp_to_sel_map(s))
    blk_id = jnp.arange(nsel)[None, :]
    cur = (pos // SEL_BLOCK)[:, None]
    forced = (blk_id == 0) | (blk_id == cur) | (blk_id == cur - 1)
    imp = jnp.where(blk_id <= cur, imp + FORCE_BONUS * forced, NEG_INF)
    _, sel_idx = lax.top_k(imp, n_top)

    ks_blocks = k_sel.reshape(b, nsel, SEL_BLOCK, G, Dh).transpose(0, 3, 1, 2, 4)
    vs_blocks = v_sel.reshape(b, nsel, SEL_BLOCK, G, Dh).transpose(0, 3, 1, 2, 4)
    gather = jax.vmap(jax.vmap(lambda blocks, ids: blocks[ids]))
    kw_pad = jnp.pad(k_win, ((0, 0), (WINDOW, 0), (0, 0), (0, 0)))
    vw_pad = jnp.pad(v_win, ((0, 0), (WINDOW, 0), (0, 0), (0, 0)))
    m_sel = n_top * SEL_BLOCK

    def chunk(c):
        t0 = c * Q_CHUNK
        qc = lax.dynamic_slice_in_dim(q, t0, Q_CHUNK, axis=1)
        tq = t0 + jnp.arange(Q_CHUNK)
        ids = lax.dynamic_slice_in_dim(sel_idx, t0, Q_CHUNK, axis=2)
        ksg = gather(ks_blocks, ids).reshape(b, G, Q_CHUNK, m_sel, Dh)
        vsg = gather(vs_blocks, ids).reshape(b, G, Q_CHUNK, m_sel, Dh)
        kpos = (ids[..., None] * SEL_BLOCK + jnp.arange(SEL_BLOCK)).reshape(b, G, Q_CHUNK, m_sel)
        dist_s = (tq[None, None, :, None] - kpos)[:, :, None]
        s_s = jnp.einsum('btgjd,bgtmd->bgjtm', qc, ksg).astype(f32) * scale \
            - slopes[None, :, :, None, None] * dist_s
        p_s = jax.nn.softmax(jnp.where(dist_s >= 0, s_s, NEG_INF), axis=-1)
        o_s = jnp.einsum('bgjtm,bgtmd->btgjd', p_s.astype(vsg.dtype), vsg)
        kw = lax.dynamic_slice_in_dim(kw_pad, t0, Q_CHUNK + WINDOW, axis=1)
        vw = lax.dynamic_slice_in_dim(vw_pad, t0, Q_CHUNK + WINDOW, axis=1)
        kpos_w = t0 - WINDOW + jnp.arange(Q_CHUNK + WINDOW)
        dist_w = tq[:, None] - kpos_w[None, :]
        valid_w = (dist_w >= 0) & (dist_w < WINDOW) & (kpos_w[None, :] >= 0)
        s_w = jnp.einsum('btgjd,bsgd->bgjts', qc, kw).astype(f32) * scale - slopes[:, :, None, None] * dist_w
        p_w = jax.nn.softmax(jnp.where(valid_w, s_w, NEG_INF), axis=-1)
        o_w = jnp.einsum('bgjts,bsgd->btgjd', p_w.astype(vw.dtype), vw)
        return o_s, o_w

    o_s, o_w = lax.map(chunk, jnp.arange(s // Q_CHUNK))
    o_s = jnp.moveaxis(o_s, 0, 1).reshape(b, s, G, J, Dh)
    o_w = jnp.moveaxis(o_w, 0, 1).reshape(b, s, G, J, Dh)

    g = jax.nn.sigmoid(gate_logits).reshape(b, s, G, J, 3)
    o = g[..., 0:1] * o_c + g[..., 1:2] * o_s + g[..., 2:3] * o_w
    return o.reshape(b, s, ATTN_WIDTH)


def conformer_conv(u, dw_w, dw_b, ln_g, ln_b, w_proj):
    a, gate = jnp.split(u, 2, axis=-1)
    z = a * jax.nn.sigmoid(gate)
    z = causal_dwconv(z, dw_w) + dw_b
    z = jax.nn.silu(layer_norm(z, ln_g, ln_b))
    return z @ w_proj


def conv_ffn(h, w_up, dw_w, dw_b, w_down):
    u = causal_dwconv(h @ w_up, dw_w) + dw_b
    a, v = jnp.split(u, 2, axis=-1)
    return (jax.nn.silu(a) * v) @ w_down


def setup_inputs(seed: int = 0) -> dict:
    key = jax.random.key(seed)
    keys = jax.random.split(key, 32)
    counter = [0]
    f32 = jnp.float32

    def nxt():
        k = keys[counter[0]]
        counter[0] += 1
        return k

    def dense(shape, fan_in):
        return jax.random.normal(nxt(), shape, f32) * fan_in ** -0.5

    def gain(shape):
        return 1.0 + 0.05 * jax.random.normal(nxt(), shape, f32)

    def small(shape, scale=0.02):
        return scale * jax.random.normal(nxt(), shape, f32)

    L = DEPTH
    return {
        "x": jax.random.normal(nxt(), (BATCH, SEQ, D_MODEL), f32),
        "norm1_g": gain((L, D_MODEL)),
        "w_in": dense((L, D_MODEL, N_IN), D_MODEL),
        "cmp_pe_k": small((L, CMP_BLOCK, HEAD_DIM), 0.1),
        "cmp_pe_v": small((L, CMP_BLOCK, HEAD_DIM), 0.1),
        "cmp_k_w1": dense((L, CMP_BLOCK * HEAD_DIM, CMP_HIDDEN), CMP_BLOCK * HEAD_DIM),
        "cmp_k_w2": dense((L, CMP_HIDDEN, HEAD_DIM), CMP_HIDDEN),
        "cmp_v_w1": dense((L, CMP_BLOCK * HEAD_DIM, CMP_HIDDEN), CMP_BLOCK * HEAD_DIM),
        "cmp_v_w2": dense((L, CMP_HIDDEN, HEAD_DIM), CMP_HIDDEN),
        "w_attn_br": dense((L, ATTN_WIDTH, D_MODEL), ATTN_WIDTH),
        "conv_dw_w": dense((L, CONV_WIDTH, CONV_CH), CONV_WIDTH),
        "conv_dw_b": small((L, CONV_CH)),
        "conv_ln_g": gain((L, CONV_CH)),
        "conv_ln_b": small((L, CONV_CH)),
        "w_conv_br": dense((L, CONV_CH, D_MODEL), CONV_CH),
        "w_out": dense((L, D_MODEL, D_MODEL), D_MODEL),
        "norm2_g": gain((L, D_MODEL)),
        "ffn_w_up": dense((L, D_MODEL, 2 * FFN_DIM), D_MODEL),
        "ffn_dw_w": dense((L, FFN_CONV_WIDTH, 2 * FFN_DIM), FFN_CONV_WIDTH),
        "ffn_dw_b": small((L, 2 * FFN_DIM)),
        "ffn_w_down": dense((L, FFN_DIM, D_MODEL), FFN_DIM),
        "final_g": gain((D_MODEL,)),
    }


def reference(x, norm1_g, w_in, cmp_pe_k, cmp_pe_v, cmp_k_w1, cmp_k_w2, cmp_v_w1, cmp_v_w2,
              w_attn_br, conv_dw_w, conv_dw_b, conv_ln_g, conv_ln_b, w_conv_br, w_out,
              norm2_g, ffn_w_up, ffn_dw_w, ffn_dw_b, ffn_w_down, final_g):
    split_points = list(np.cumsum(IN_SIZES)[:-1])
    for l in range(DEPTH):
        h = rms_norm(x, norm1_g[l])
        proj = h @ w_in[l]
        (q, k_c, v_c, k_s, v_s, k_w, v_w, g_nsa, u_conv, g_merge) = jnp.split(proj, split_points, axis=-1)
        y_attn = nsa_attention(q, k_c, v_c, k_s, v_s, k_w, v_w, g_nsa, cmp_pe_k[l], cmp_pe_v[l],
                               cmp_k_w1[l], cmp_k_w2[l], cmp_v_w1[l], cmp_v_w2[l]) @ w_attn_br[l]
        y_conv = conformer_conv(u_conv, conv_dw_w[l], conv_dw_b[l], conv_ln_g[l], conv_ln_b[l], w_conv_br[l])
        g_a, g_b = jnp.split(jax.nn.sigmoid(g_merge), 2, axis=-1)
        x = x + (g_a * y_attn + g_b * y_conv) @ w_out[l]
        x = x + conv_ffn(rms_norm(x, norm2_g[l]), ffn_w_up[l], ffn_dw_w[l], ffn_dw_b[l], ffn_w_down[l])
    return rms_norm(x, final_g)
```

```python
import functools
import math

import numpy as np
import jax
import jax.numpy as jnp
from jax import lax
from jax.experimental import pallas as pl
from jax.experimental.pallas import tpu as pltpu

F32 = jnp.float32
BF16 = jnp.bfloat16

D_MODEL = 1024
N_HEADS = 8
HEAD_DIM = 64
N_KV_GROUPS = 2
HEADS_PER_GROUP = N_HEADS // N_KV_GROUPS
ATTN_WIDTH = N_HEADS * HEAD_DIM
KV_WIDTH = N_KV_GROUPS * HEAD_DIM
CMP_BLOCK = 32
CMP_STRIDE = 16
CMP_HIDDEN = 128
SEL_BLOCK = 64
SEL_TOP = 16
SEL_SHIFT = 6
WINDOW = 512
FORCE_BONUS = 1.0e4
NEG_INF = -1.0e30
CONV_CH = 512
CONV_WIDTH = 31
FFN_DIM = 2816
FFN_CONV_WIDTH = 3
NORM_EPS = 1e-6
N_GATES = 3 * N_HEADS
ATTN_SCALE = HEAD_DIM ** -0.5

VMEM_LIMIT_BYTES = 56 * 1024 * 1024
LANES = 128

ROW_TILE = 512
ATTN_Q_TILE = 256
ATTN_K_TILE = 256
CMP_Q_TILE = 512
CONV_TILE = 256
CONV_HALO = 32
FFN_COL_TILE = 256
FFN_HALO = 8


def _params(sem):
    return pltpu.CompilerParams(dimension_semantics=sem, vmem_limit_bytes=VMEM_LIMIT_BYTES)


def _rms(x, g):
    y = x * lax.rsqrt(jnp.mean(x * x, axis=-1, keepdims=True) + NORM_EPS)
    return y * g


def _sigmoid(x):
    return jax.nn.sigmoid(x)


def _inproj_kernel(x_ref, g_ref, wq_ref, wkv_ref, wgn_ref, wuc_ref, wgm_ref,
                   q_ref, kv_ref, gn_ref, uc_ref, gm_ref):
    h = _rms(x_ref[...], g_ref[...]).astype(BF16)
    q_ref[...] = jnp.dot(h, wq_ref[...], preferred_element_type=F32).astype(BF16)
    kv_ref[...] = jnp.dot(h, wkv_ref[...], preferred_element_type=F32).astype(BF16)
    gn_ref[...] = jnp.dot(h, wgn_ref[...], preferred_element_type=F32)
    uc_ref[...] = jnp.dot(h, wuc_ref[...], preferred_element_type=F32)
    gm_ref[...] = jnp.dot(h, wgm_ref[...], preferred_element_type=F32)


def _inproj(x2, g, wq, wkv, wgn, wuc, wgm):
    t = x2.shape[0]
    tm = ROW_TILE
    row = lambda n: pl.BlockSpec((tm, n), lambda i: (i, 0))
    full = lambda a: pl.BlockSpec(a.shape, lambda i: (0, 0))
    widths = (wq.shape[1], wkv.shape[1], wgn.shape[1], wuc.shape[1], wgm.shape[1])
    dtypes = (BF16, BF16, F32, F32, F32)
    return pl.pallas_call(
        _inproj_kernel,
        grid=(t // tm,),
        in_specs=[row(D_MODEL), full(g), full(wq), full(wkv), full(wgn), full(wuc), full(wgm)],
        out_specs=[row(n) for n in widths],
        out_shape=[jax.ShapeDtypeStruct((t, n), d) for n, d in zip(widths, dtypes)],
        compiler_params=_params(("parallel",)),
        name="inproj",
    )(x2, g, wq, wkv, wgn, wuc, wgm)


def _gelu_tanh(x):
    return 0.5 * x * (1.0 + jnp.tanh(math.sqrt(2.0 / math.pi) * (x + 0.044715 * (x * x * x))))


def _compress(r_ref, pe_ref, w1_ref, w2_ref):
    r = r_ref[0, 0].astype(F32)
    top = (r + pe_ref[0:1, :]).astype(BF16)
    bot = (r + pe_ref[1:2, :]).astype(BF16)
    a = jnp.dot(top, w1_ref[0], preferred_element_type=F32)
    b = jnp.dot(bot, w1_ref[1], preferred_element_type=F32)
    nr = a.shape[0]
    pre = a + pltpu.roll(b, nr - 1, 0)
    hid = _gelu_tanh(pre).astype(BF16)
    return jnp.dot(hid, w2_ref[...], preferred_element_type=F32)


def _cmp_kernel(kr_ref, vr_ref, pek_ref, pev_ref, w1k_ref, w2k_ref, w1v_ref, w2v_ref,
                q_ref, mapt_ref, oc_ref, selt_ref, *, seq):
    grp = pl.program_id(1)
    kc = _compress(kr_ref, pek_ref, w1k_ref, w2k_ref).astype(BF16)
    vc = _compress(vr_ref, pev_ref, w1v_ref, w2v_ref).astype(BF16)
    nr = kc.shape[0]
    nsel = seq // SEL_BLOCK
    tq = CMP_Q_TILE
    c_end = lax.broadcasted_iota(jnp.int32, (1, nr), 1) * CMP_STRIDE + (CMP_BLOCK - 1)
    blk = lax.broadcasted_iota(jnp.int32, (nsel, 1), 0)

    def tile(ti, carry):
        t0 = pl.multiple_of(ti * tq, tq)
        pos_col = t0 + lax.broadcasted_iota(jnp.int32, (tq, 1), 0)
        dist = pos_col - c_end
        valid = dist >= 0
        distf = dist.astype(F32)
        psum = jnp.zeros((tq, nr), F32)
        for j in range(HEADS_PER_GROUP):
            slope = jnp.where(grp == 0, 2.0 ** -(j + 1), 2.0 ** -(HEADS_PER_GROUP + j + 1)).astype(F32)
            q = q_ref[0, 0, j, pl.ds(t0, tq), :]
            s = lax.dot_general(q, kc, (((1,), (1,)), ((), ())), preferred_element_type=F32)
            s = s * ATTN_SCALE - slope * distf
            s = jnp.where(valid, s, NEG_INF)
            m = jnp.max(s, axis=-1, keepdims=True)
            e = jnp.where(valid, jnp.exp(s - m), 0.0)
            l = jnp.sum(e, axis=-1, keepdims=True)
            p = e / jnp.where(l > 0.0, l, 1.0)
            psum = psum + p
            oc_ref[0, 0, j, pl.ds(t0, tq), :] = jnp.dot(p.astype(BF16), vc, preferred_element_type=F32)
        imp = lax.dot_general(mapt_ref[...], psum, (((1,), (1,)), ((), ())),
                              precision=lax.Precision.HIGHEST, preferred_element_type=F32)
        pos_row = t0 + lax.broadcasted_iota(jnp.int32, (1, tq), 1)
        cur = pos_row >> SEL_SHIFT
        forced = (blk == 0) | (blk == cur) | (blk == cur - 1)
        causal = blk <= cur
        val = jnp.where(causal, imp + jnp.where(forced, FORCE_BONUS, 0.0), NEG_INF)
        cnt = jnp.zeros((nsel, tq), F32)
        for i in range(nsel):
            vi = val[i:i + 1, :]
            tie = (blk > i).astype(F32)
            cnt = cnt + jnp.where(vi > val, 1.0, jnp.where(vi == val, tie, 0.0))
        n_top = min(SEL_TOP, nsel)
        selt_ref[0, 0, :, pl.ds(t0, tq)] = jnp.where(causal & (cnt < n_top), 1.0, 0.0)
        return carry

    lax.fori_loop(0, seq // tq, tile, 0)


def _cmp_branch(kr, vr, pek, pev, w1k, w2k, w1v, w2v, qh, mapt):
    b, g, nr, rw = kr.shape
    seq = qh.shape[3]
    nsel = seq // SEL_BLOCK
    bg4 = lambda shape: pl.BlockSpec((1, 1) + shape, lambda i, j: (i, j, 0, 0))
    full = lambda a: pl.BlockSpec(a.shape, lambda i, j: (0,) * a.ndim)
    return pl.pallas_call(
        functools.partial(_cmp_kernel, seq=seq),
        grid=(b, g),
        in_specs=[bg4((nr, rw)), bg4((nr, rw)), full(pek), full(pev), full(w1k), full(w2k), full(w1v), full(w2v),
                  pl.BlockSpec((1, 1, HEADS_PER_GROUP, seq, HEAD_DIM), lambda i, j: (i, j, 0, 0, 0)),
                  full(mapt)],
        out_specs=[pl.BlockSpec((1, 1, HEADS_PER_GROUP, seq, HEAD_DIM), lambda i, j: (i, j, 0, 0, 0)),
                   bg4((nsel, seq))],
        out_shape=[jax.ShapeDtypeStruct((b, g, HEADS_PER_GROUP, seq, HEAD_DIM), F32),
                   jax.ShapeDtypeStruct((b, g, nsel, seq), F32)],
        compiler_params=_params(("parallel", "parallel")),
        name="cmp_topk",
    )(kr, vr, pek, pev, w1k, w2k, w1v, w2v, qh, mapt)


def _attn_kernel(q_ref, ks_ref, vs_ref, kw_ref, vw_ref, selt_ref, oc_ref, gate_ref, o_ref,
                 m_ref, l_ref, acc_ref):
    grp = pl.program_id(1)
    qi = pl.program_id(2)
    tq, tk = ATTN_Q_TILE, ATTN_K_TILE
    nh = HEADS_PER_GROUP
    rows = nh * tq
    t0 = qi * tq
    q = q_ref[0, 0].reshape(rows, HEAD_DIM)
    sel = jnp.transpose(selt_ref[0, 0]).astype(BF16)
    nsel = sel.shape[1]
    row = lax.broadcasted_iota(jnp.int32, (rows, 1), 0)
    tpos = t0 + (row & (tq - 1))
    head = row >> int(math.log2(tq))
    slope = jnp.zeros((rows, 1), F32)
    for j in range(nh):
        sj = jnp.where(grp == 0, 2.0 ** -(j + 1), 2.0 ** -(nh + j + 1)).astype(F32)
        slope = jnp.where(head == j, sj, slope)
    blk_row = lax.broadcasted_iota(jnp.int32, (nsel, 1), 0)

    def branch(k_ref, v_ref, n_steps, selected):
        m_ref[...] = jnp.full((rows, 1), NEG_INF, F32)
        l_ref[...] = jnp.zeros((rows, 1), F32)
        acc_ref[...] = jnp.zeros((rows, HEAD_DIM), F32)

        def step(i, carry):
            kt = qi - i
            k0 = pl.multiple_of(kt * tk, tk)
            k = k_ref[0, 0, pl.ds(k0, tk), :]
            v = v_ref[0, 0, pl.ds(k0, tk), :]
            s = lax.dot_general(q, k, (((1,), (1,)), ((), ())), preferred_element_type=F32)
            kpos = k0 + lax.broadcasted_iota(jnp.int32, (1, tk), 1)
            dist = tpos - kpos
            s = s * ATTN_SCALE - slope * dist.astype(F32)
            if selected:
                expand = (blk_row == (kpos >> SEL_SHIFT)).astype(BF16)
                chosen = jnp.dot(sel, expand, preferred_element_type=F32)
                chosen = jnp.concatenate([chosen] * nh, axis=0)
                mask = (chosen > 0.5) & (dist >= 0)
            else:
                mask = (dist >= 0) & (dist < WINDOW)
            s = jnp.where(mask, s, NEG_INF)
            m_old = m_ref[...]
            m_new = jnp.maximum(m_old, jnp.max(s, axis=-1, keepdims=True))
            alpha = jnp.exp(m_old - m_new)
            p = jnp.exp(s - m_new)
            l_ref[...] = alpha * l_ref[...] + jnp.sum(p, axis=-1, keepdims=True)
            acc_ref[...] = alpha * acc_ref[...] + jnp.dot(p.astype(BF16), v, preferred_element_type=F32)
            m_ref[...] = m_new
            return carry

        lax.fori_loop(0, n_steps, step, 0)
        return acc_ref[...] / l_ref[...]

    o_sel = branch(ks_ref, vs_ref, qi + 1, True)
    o_win = branch(kw_ref, vw_ref, jnp.minimum(qi, WINDOW // tk) + 1, False)
    gate = _sigmoid(gate_ref[0, 0])
    for j in range(nh):
        sl = slice(j * tq, (j + 1) * tq)
        o = (gate[:, 3 * j:3 * j + 1] * oc_ref[0, 0, j]
             + gate[:, 3 * j + 1:3 * j + 2] * o_sel[sl]
             + gate[:, 3 * j + 2:3 * j + 3] * o_win[sl])
        o_ref[0, 0, j] = o.astype(BF16)


def _attention(qh, ks, vs, kw, vw, selt, oc, gate):
    b, g, nh, seq, dh = qh.shape
    tq = ATTN_Q_TILE
    nsel = selt.shape[2]
    kv_spec = pl.BlockSpec((1, 1, seq, dh), lambda i, j, t: (i, j, 0, 0))
    head_spec = pl.BlockSpec((1, 1, nh, tq, dh), lambda i, j, t: (i, j, 0, t, 0))
    return pl.pallas_call(
        _attn_kernel,
        grid=(b, g, seq // tq),
        in_specs=[head_spec, kv_spec, kv_spec, kv_spec, kv_spec,
                  pl.BlockSpec((1, 1, nsel, tq), lambda i, j, t: (i, j, 0, t)),
                  head_spec,
                  pl.BlockSpec((1, 1, tq, 3 * nh), lambda i, j, t: (i, j, t, 0))],
        out_specs=head_spec,
        out_shape=jax.ShapeDtypeStruct((b, g, nh, seq, dh), BF16),
        scratch_shapes=[pltpu.VMEM((nh * tq, 1), F32), pltpu.VMEM((nh * tq, 1), F32),
                        pltpu.VMEM((nh * tq, dh), F32)],
        compiler_params=_params(("parallel", "parallel", "arbitrary")),
        name="nsa_flash",
    )(qh, ks, vs, kw, vw, selt, oc, gate)


def _glu(u):
    return u[:, :CONV_CH] * _sigmoid(u[:, CONV_CH:])


def _conformer_kernel(cur_ref, prev_ref, dww_ref, dwb_ref, lng_ref, lnb_ref, wp_ref, y_ref, z_ref, *, tiles_per_seq):
    ts, halo = CONV_TILE, CONV_HALO
    first = pl.program_id(1) == 0
    zp = _glu(prev_ref[0, ts - halo:, :])
    z_ref[0:halo, :] = jnp.where(first, 0.0, zp)
    z_ref[halo:, :] = _glu(cur_ref[0])
    acc = jnp.zeros((ts, CONV_CH), F32) + dwb_ref[...]
    base = halo - (CONV_WIDTH - 1)
    for k in range(CONV_WIDTH):
        acc = acc + dww_ref[k:k + 1, :] * z_ref[base + k:base + k + ts, :]
    mu = jnp.mean(acc, axis=-1, keepdims=True)
    cen = acc - mu
    var = jnp.mean(cen * cen, axis=-1, keepdims=True)
    y = cen * lax.rsqrt(var + NORM_EPS) * lng_ref[...] + lnb_ref[...]
    y = y * _sigmoid(y)
    y_ref[0] = jnp.dot(y.astype(BF16), wp_ref[...], preferred_element_type=F32)


def _conformer(u3, dww, dwb, lng, lnb, wp):
    b, seq, w = u3.shape
    ts = CONV_TILE
    full = lambda a: pl.BlockSpec(a.shape, lambda i, t: (0,) * a.ndim)
    return pl.pallas_call(
        functools.partial(_conformer_kernel, tiles_per_seq=seq // ts),
        grid=(b, seq // ts),
        in_specs=[pl.BlockSpec((1, ts, w), lambda i, t: (i, t, 0)),
                  pl.BlockSpec((1, ts, w), lambda i, t: (i, jnp.maximum(t - 1, 0), 0)),
                  full(dww), full(dwb), full(lng), full(lnb), full(wp)],
        out_specs=pl.BlockSpec((1, ts, D_MODEL), lambda i, t: (i, t, 0)),
        out_shape=jax.ShapeDtypeStruct((b, seq, D_MODEL), F32),
        scratch_shapes=[pltpu.VMEM((ts + CONV_HALO, CONV_CH), F32)],
        compiler_params=_params(("parallel", "arbitrary")),
        name="conformer_conv",
    )(u3, u3, dww, dwb, lng, lnb, wp)


def _merge_kernel(o_ref, yc_ref, gm_ref, x_ref, wa_ref, wo_ref, out_ref):
    ya = jnp.dot(o_ref[...], wa_ref[...], preferred_element_type=F32)
    gm = gm_ref[...]
    mix = _sigmoid(gm[:, :D_MODEL]) * ya + _sigmoid(gm[:, D_MODEL:]) * yc_ref[...]
    out_ref[...] = x_ref[...] + jnp.dot(mix.astype(BF16), wo_ref[...], preferred_element_type=F32)


def _merge(o2, yc2, gm2, x2, wa, wo):
    t = x2.shape[0]
    tm = ROW_TILE
    row = lambda n: pl.BlockSpec((tm, n), lambda i: (i, 0))
    full = lambda a: pl.BlockSpec(a.shape, lambda i: (0, 0))
    return pl.pallas_call(
        _merge_kernel,
        grid=(t // tm,),
        in_specs=[row(ATTN_WIDTH), row(D_MODEL), row(2 * D_MODEL), row(D_MODEL), full(wa), full(wo)],
        out_specs=row(D_MODEL),
        out_shape=jax.ShapeDtypeStruct((t, D_MODEL), F32),
        compiler_params=_params(("parallel",)),
        name="merge_outproj",
    )(o2, yc2, gm2, x2, wa, wo)


def _ffn_kernel(x_ref, xh_ref, g_ref, wa_ref, wv_ref, cwa_ref, cwv_ref, cba_ref, cbv_ref, wd_ref, out_ref,
                h_ref, a_ref, v_ref, acc_ref, *, tiles_per_seq):
    tm, halo = ROW_TILE, FFN_HALO
    j = pl.program_id(1)

    @pl.when(j == 0)
    def _():
        seq_start = pl.program_id(0) % tiles_per_seq == 0
        hh = _rms(xh_ref[...], g_ref[...])
        h_ref[0:halo, :] = jnp.where(seq_start, 0.0, hh).astype(BF16)
        h_ref[halo:, :] = _rms(x_ref[...], g_ref[...]).astype(BF16)
        acc_ref[...] = jnp.zeros_like(acc_ref)

    h = h_ref[...]
    a_ref[...] = jnp.dot(h, wa_ref[0], preferred_element_type=F32)
    v_ref[...] = jnp.dot(h, wv_ref[0], preferred_element_type=F32)

    def conv(ref, w_ref, b_ref):
        out = b_ref[0]
        for k in range(FFN_CONV_WIDTH):
            off = halo - (FFN_CONV_WIDTH - 1) + k
            out = out + w_ref[0, k:k + 1, :] * ref[off:off + tm, :]
        return out

    ca = conv(a_ref, cwa_ref, cba_ref)
    cv = conv(v_ref, cwv_ref, cbv_ref)
    gated = (ca * _sigmoid(ca) * cv).astype(BF16)
    acc_ref[...] += jnp.dot(gated, wd_ref[0], preferred_element_type=F32)

    @pl.when(j == pl.num_programs(1) - 1)
    def _():
        out_ref[...] = x_ref[...] + acc_ref[...]


def _ffn(x2, g, wa, wv, cwa, cwv, cba, cbv, wd, seq):
    t = x2.shape[0]
    tm, halo = ROW_TILE, FFN_HALO
    nj, _, tf = wa.shape
    col3 = lambda a: pl.BlockSpec((1,) + a.shape[1:], lambda i, j: (j, 0, 0))
    return pl.pallas_call(
        functools.partial(_ffn_kernel, tiles_per_seq=seq // tm),
        grid=(t // tm, nj),
        in_specs=[pl.BlockSpec((tm, D_MODEL), lambda i, j: (i, 0)),
                  pl.BlockSpec((halo, D_MODEL), lambda i, j: (jnp.maximum(i * (tm // halo) - 1, 0), 0)),
                  pl.BlockSpec(g.shape, lambda i, j: (0, 0)),
                  col3(wa), col3(wv), col3(cwa), col3(cwv), col3(cba), col3(cbv), col3(wd)],
        out_specs=pl.BlockSpec((tm, D_MODEL), lambda i, j: (i, 0)),
        out_shape=jax.ShapeDtypeStruct((t, D_MODEL), F32),
        scratch_shapes=[pltpu.VMEM((tm + halo, D_MODEL), BF16), pltpu.VMEM((tm + halo, tf), F32),
                        pltpu.VMEM((tm + halo, tf), F32), pltpu.VMEM((tm, D_MODEL), F32)],
        compiler_params=_params(("parallel", "arbitrary")),
        name="conv_ffn",
    )(x2, x2, g, wa, wv, cwa, cwv, cba, cbv, wd)


def _norm_kernel(x_ref, g_ref, o_ref):
    o_ref[...] = _rms(x_ref[...], g_ref[...])


def _final_norm(x2, g):
    t = x2.shape[0]
    tm = ROW_TILE
    return pl.pallas_call(
        _norm_kernel,
        grid=(t // tm,),
        in_specs=[pl.BlockSpec((tm, D_MODEL), lambda i: (i, 0)), pl.BlockSpec(g.shape, lambda i: (0, 0))],
        out_specs=pl.BlockSpec((tm, D_MODEL), lambda i: (i, 0)),
        out_shape=jax.ShapeDtypeStruct((t, D_MODEL), F32),
        compiler_params=_params(("parallel",)),
        name="final_norm",
    )(x2, g)


def _sel_map_t(seq):
    ncmp = (seq - CMP_BLOCK) // CMP_STRIDE + 1
    nr = seq // CMP_STRIDE
    nsel = seq // SEL_BLOCK
    cs = np.arange(ncmp) * CMP_STRIDE
    ce = cs + CMP_BLOCK - 1
    ss = np.arange(nsel) * SEL_BLOCK
    se = ss + SEL_BLOCK - 1
    ov = np.minimum(ce[:, None], se[None, :]) - np.maximum(cs[:, None], ss[None, :]) + 1
    m = np.zeros((nsel, nr), np.float32)
    m[:, :ncmp] = (np.clip(ov, 0, None).astype(np.float32) / CMP_BLOCK).T
    return jnp.asarray(m)


def _layer(x2, batch, seq, p):
    g_, j_, dh = N_KV_GROUPS, HEADS_PER_GROUP, HEAD_DIM
    q2, kv2, gn2, uc2, gm2 = _inproj(x2, p["norm1_g"], p["wq"], p["wkv"], p["wgn"], p["wuc"], p["wgm"])
    qh = q2.reshape(batch, seq, g_, j_, dh).transpose(0, 2, 3, 1, 4)
    kv = kv2.reshape(batch, seq, 6, g_, dh).transpose(2, 0, 3, 1, 4)
    rows = seq // CMP_STRIDE
    kr = kv[0].reshape(batch, g_, rows, CMP_STRIDE * dh)
    vr = kv[1].reshape(batch, g_, rows, CMP_STRIDE * dh)
    gate = gn2[:, :N_GATES].reshape(batch, seq, g_, 3 * j_).transpose(0, 2, 1, 3)
    oc, selt = _cmp_branch(kr, vr, p["pek"], p["pev"], p["w1k"], p["w2k"], p["w1v"], p["w2v"], qh, _sel_map_t(seq))
    oh = _attention(qh, kv[2], kv[3], kv[4], kv[5], selt, oc, gate)
    o2 = oh.transpose(0, 3, 1, 2, 4).reshape(batch * seq, ATTN_WIDTH)
    yc = _conformer(uc2.reshape(batch, seq, 2 * CONV_CH), p["dww"], p["dwb"], p["lng"], p["lnb"], p["wconv"])
    x2 = _merge(o2, yc.reshape(batch * seq, D_MODEL), gm2, x2, p["wattn"], p["wout"])
    return _ffn(x2, p["norm2_g"], p["wa"], p["wv"], p["cwa"], p["cwv"], p["cba"], p["cbv"], p["wd"], seq)


def _prep_layer(l, norm1_g, w_in, cmp_pe_k, cmp_pe_v, cmp_k_w1, cmp_k_w2, cmp_v_w1, cmp_v_w2, w_attn_br,
                conv_dw_w, conv_dw_b, conv_ln_g, conv_ln_b, w_conv_br, w_out, norm2_g, ffn_w_up, ffn_dw_w,
                ffn_dw_b, ffn_w_down):
    w = w_in[l].astype(BF16)
    c0 = ATTN_WIDTH
    c1 = c0 + 6 * KV_WIDTH
    c2 = c1 + N_GATES
    c3 = c2 + 2 * CONV_CH
    half = CMP_BLOCK // 2 * HEAD_DIM
    tf = FFN_COL_TILE
    nj = FFN_DIM // tf
    cols = lambda a: a.reshape(a.shape[0], nj, tf).transpose(1, 0, 2)
    up = ffn_w_up[l].astype(BF16)
    dw = ffn_dw_w[l]
    db = ffn_dw_b[l][None, :]
    return dict(
        norm1_g=norm1_g[l][None, :],
        wq=w[:, :c0], wkv=w[:, c0:c1],
        wgn=jnp.pad(w[:, c1:c2], ((0, 0), (0, LANES - N_GATES))),
        wuc=w[:, c2:c3], wgm=w[:, c3:],
        pek=cmp_pe_k[l].reshape(2, half), pev=cmp_pe_v[l].reshape(2, half),
        w1k=cmp_k_w1[l].astype(BF16).reshape(2, half, CMP_HIDDEN), w2k=cmp_k_w2[l].astype(BF16),
        w1v=cmp_v_w1[l].astype(BF16).reshape(2, half, CMP_HIDDEN), w2v=cmp_v_w2[l].astype(BF16),
        wattn=w_attn_br[l].astype(BF16),
        dww=conv_dw_w[l], dwb=conv_dw_b[l][None, :], lng=conv_ln_g[l][None, :], lnb=conv_ln_b[l][None, :],
        wconv=w_conv_br[l].astype(BF16), wout=w_out[l].astype(BF16),
        norm2_g=norm2_g[l][None, :],
        wa=cols(up[:, :FFN_DIM]), wv=cols(up[:, FFN_DIM:]),
        cwa=cols(dw[:, :FFN_DIM]), cwv=cols(dw[:, FFN_DIM:]),
        cba=cols(db[:, :FFN_DIM]), cbv=cols(db[:, FFN_DIM:]),
        wd=ffn_w_down[l].astype(BF16).reshape(nj, tf, D_MODEL),
    )


def kernel(x, norm1_g, w_in, cmp_pe_k, cmp_pe_v, cmp_k_w1, cmp_k_w2, cmp_v_w1, cmp_v_w2, w_attn_br, conv_dw_w, conv_dw_b, conv_ln_g, conv_ln_b, w_conv_br, w_out, norm2_g, ffn_w_up, ffn_dw_w, ffn_dw_b, ffn_w_down, final_g):
    batch, seq, d = x.shape
    assert d == D_MODEL and seq % ROW_TILE == 0 and seq % ATTN_Q_TILE == 0 and seq % CMP_Q_TILE == 0
    x2 = x.reshape(batch * seq, d)
    for l in range(w_in.shape[0]):
        p = _prep_layer(l, norm1_g, w_in, cmp_pe_k, cmp_pe_v, cmp_k_w1, cmp_k_w2, cmp_v_w1, cmp_v_w2, w_attn_br,
                        conv_dw_w, conv_dw_b, conv_ln_g, conv_ln_b, w_conv_br, w_out, norm2_g, ffn_w_up,
                        ffn_dw_w, ffn_dw_b, ffn_w_down)
        x2 = _layer(x2, batch, seq, p)
    return _final_norm(x2, final_g[None, :]).reshape(batch, seq, d)
```

```python
import functools
import math

import numpy as np
import jax
import jax.numpy as jnp
from jax import lax
from jax.experimental import pallas as pl
from jax.experimental.pallas import tpu as pltpu

F32 = jnp.float32
BF16 = jnp.bfloat16

D_MODEL = 1024
N_HEADS = 8
HEAD_DIM = 64
N_KV_GROUPS = 2
HEADS_PER_GROUP = N_HEADS // N_KV_GROUPS
ATTN_WIDTH = N_HEADS * HEAD_DIM
KV_WIDTH = N_KV_GROUPS * HEAD_DIM
CMP_BLOCK = 32
CMP_STRIDE = 16
CMP_HIDDEN = 128
SEL_BLOCK = 64
SEL_TOP = 16
SEL_SHIFT = 6
WINDOW = 512
FORCE_BONUS = 1.0e4
NEG_INF = -1.0e30
CONV_CH = 512
CONV_WIDTH = 31
FFN_DIM = 2816
FFN_CONV_WIDTH = 3
NORM_EPS = 1e-6
N_GATES = 3 * N_HEADS
ATTN_SCALE = HEAD_DIM ** -0.5
UNSELECTED = -(2.0 ** 99)

VMEM_LIMIT_BYTES = 56 * 1024 * 1024
LANES = 128

ROW_TILE = 512
ATTN_TILE = 256
CMP_Q_TILE = 512
CONV_TILE = 256
CONV_HALO = 32
FFN_COL_TILE = 256
FFN_HALO = 8

AUG = LANES - HEAD_DIM
AUG_POS = N_HEADS * 4
GATE_ROWS = 16
Q_ROWS = HEADS_PER_GROUP * HEAD_DIM

assert AUG_POS == 2048 // SEL_BLOCK and AUG_POS + 4 <= AUG


def _params(sem):
    return pltpu.CompilerParams(dimension_semantics=sem, vmem_limit_bytes=VMEM_LIMIT_BYTES)


def _rms(x, g):
    y = x * lax.rsqrt(jnp.mean(x * x, axis=-1, keepdims=True) + NORM_EPS)
    return y * g


def _sigmoid(x):
    return jax.nn.sigmoid(x)


def _dot_nt(a, b, **kw):
    return lax.dot_general(a, b, (((1,), (1,)), ((), ())), preferred_element_type=F32, **kw)


def _dot_tn(a, b):
    return lax.dot_general(a, b, (((0,), (0,)), ((), ())), preferred_element_type=F32)


def _inproj_kernel(x_ref, g_ref, wn_ref, wt_ref, aug_ref,
                   ksa_ref, kwa_ref, kcv_ref, uc_ref, gm_ref, qt_ref, vt_ref, gnt_ref):
    h = _rms(x_ref[...], g_ref[...]).astype(BF16)
    c = 0
    for ref, extra in ((ksa_ref, aug_ref.at[:, 0:2 * LANES]), (kwa_ref, aug_ref.at[:, 2 * LANES:4 * LANES]),
                       (kcv_ref, None), (uc_ref, None), (gm_ref, None)):
        n = ref.shape[1]
        y = jnp.dot(h, wn_ref[:, c:c + n], preferred_element_type=F32)
        if extra is not None:
            y = y + extra[...]
        ref[...] = y.astype(ref.dtype)
        c += n
    t = _dot_nt(wt_ref[...], h)
    nq, nv = qt_ref.shape[0], vt_ref.shape[0]
    qt_ref[...] = (t[0:nq] * ATTN_SCALE).astype(BF16)
    vt_ref[...] = t[nq:nq + nv].astype(BF16)
    gnt_ref[...] = t[nq + nv:]


def _inproj(x2, g, wn, wt, aug, seq):
    t = x2.shape[0]
    tm = ROW_TILE
    row = lambda n: pl.BlockSpec((tm, n), lambda i: (i, 0))
    col = lambda n: pl.BlockSpec((n, tm), lambda i: (0, i))
    full = lambda a: pl.BlockSpec(a.shape, lambda i: (0, 0))
    nat = ((2 * LANES, BF16), (2 * LANES, BF16), (2 * KV_WIDTH, BF16), (2 * CONV_CH, F32), (2 * D_MODEL, F32))
    trn = ((ATTN_WIDTH, BF16), (2 * KV_WIDTH, BF16), (N_KV_GROUPS * GATE_ROWS, F32))
    return pl.pallas_call(
        _inproj_kernel,
        grid=(t // tm,),
        in_specs=[row(D_MODEL), full(g), full(wn), full(wt),
                  pl.BlockSpec((tm, 4 * LANES), lambda i: (i % (seq // tm), 0))],
        out_specs=[row(n) for n, _ in nat] + [col(n) for n, _ in trn],
        out_shape=[jax.ShapeDtypeStruct((t, n), d) for n, d in nat]
                  + [jax.ShapeDtypeStruct((n, t), d) for n, d in trn],
        compiler_params=_params(("parallel",)),
        name="inproj",
    )(x2, g, wn, wt, aug)


def _gelu_tanh(x):
    return 0.5 * x * (1.0 + jnp.tanh(math.sqrt(2.0 / math.pi) * (x + 0.044715 * (x * x * x))))


def _compress(r_ref, pe_ref, w1_ref, w2_ref):
    r = r_ref[0, 0].astype(F32)
    top = (r + pe_ref[0:1, :]).astype(BF16)
    bot = (r + pe_ref[1:2, :]).astype(BF16)
    a = jnp.dot(top, w1_ref[0], preferred_element_type=F32)
    b = jnp.dot(bot, w1_ref[1], preferred_element_type=F32)
    nr = a.shape[0]
    pre = a + pltpu.roll(b, nr - 1, 0)
    hid = _gelu_tanh(pre).astype(BF16)
    return jnp.dot(hid, w2_ref[...], preferred_element_type=F32)


def _cmp_kernel(kr_ref, vr_ref, pek_ref, pev_ref, w1k_ref, w2k_ref, w1v_ref, w2v_ref,
                qt_ref, mapt_ref, oct_ref, selb_ref, *, seq):
    grp = pl.program_id(1)
    kc = _compress(kr_ref, pek_ref, w1k_ref, w2k_ref).astype(BF16)
    vct = jnp.transpose(_compress(vr_ref, pev_ref, w1v_ref, w2v_ref)).astype(BF16)
    nr = kc.shape[0]
    nsel = seq // SEL_BLOCK
    tq = CMP_Q_TILE
    c_end = lax.broadcasted_iota(jnp.int32, (nr, 1), 0) * CMP_STRIDE + (CMP_BLOCK - 1)
    blk = lax.broadcasted_iota(jnp.int32, (nsel, 1), 0)

    def tile(ti, carry):
        t0 = pl.multiple_of(ti * tq, tq)
        pos = t0 + lax.broadcasted_iota(jnp.int32, (1, tq), 1)
        dist = pos - c_end
        valid = dist >= 0
        distf = dist.astype(F32)
        psum = jnp.zeros((nr, tq), F32)
        for j in range(HEADS_PER_GROUP):
            slope = jnp.where(grp == 0, 2.0 ** -(j + 1), 2.0 ** -(HEADS_PER_GROUP + j + 1)).astype(F32)
            rows = pl.ds(j * HEAD_DIM, HEAD_DIM)
            s = jnp.dot(kc, qt_ref[rows, pl.ds(t0, tq)], preferred_element_type=F32)
            s = jnp.where(valid, s - slope * distf, NEG_INF)
            m = jnp.max(s, axis=0, keepdims=True)
            e = jnp.where(valid, jnp.exp(s - m), 0.0)
            l = jnp.sum(e, axis=0, keepdims=True)
            p = e / jnp.where(l > 0.0, l, 1.0)
            psum = psum + p
            oct_ref[rows, pl.ds(t0, tq)] = jnp.dot(vct, p.astype(BF16), preferred_element_type=F32)
        imp = jnp.dot(mapt_ref[...], psum, precision=lax.Precision.HIGHEST, preferred_element_type=F32)
        cur = pos >> SEL_SHIFT
        forced = (blk == 0) | (blk == cur) | (blk == cur - 1)
        causal = blk <= cur
        val = jnp.where(causal, imp + jnp.where(forced, FORCE_BONUS, 0.0), NEG_INF)
        cnt = jnp.zeros((nsel, tq), F32)
        for i in range(nsel):
            vi = val[i:i + 1, :]
            tie = (blk > i).astype(F32)
            cnt = cnt + jnp.where(vi > val, 1.0, jnp.where(vi == val, tie, 0.0))
        n_top = min(SEL_TOP, nsel)
        selb_ref[0, 0, :, pl.ds(t0, tq)] = jnp.where(causal & (cnt < n_top), 0.0, UNSELECTED).astype(BF16)
        return carry

    lax.fori_loop(0, seq // tq, tile, 0)


def _cmp_branch(kr, vr, pek, pev, w1k, w2k, w1v, w2v, qt, mapt, seq):
    b, g, nr, rw = kr.shape
    nsel = seq // SEL_BLOCK
    bg4 = lambda shape: pl.BlockSpec((1, 1) + shape, lambda i, j: (i, j, 0, 0))
    full = lambda a: pl.BlockSpec(a.shape, lambda i, j: (0,) * a.ndim)
    heads = pl.BlockSpec((Q_ROWS, seq), lambda i, j: (j, i))
    return pl.pallas_call(
        functools.partial(_cmp_kernel, seq=seq),
        grid=(b, g),
        in_specs=[bg4((nr, rw)), bg4((nr, rw)), full(pek), full(pev), full(w1k), full(w2k), full(w1v), full(w2v),
                  heads, full(mapt)],
        out_specs=[heads, bg4((nsel, seq))],
        out_shape=[jax.ShapeDtypeStruct((ATTN_WIDTH, b * seq), F32),
                   jax.ShapeDtypeStruct((b, g, nsel, seq), BF16)],
        compiler_params=_params(("parallel", "parallel")),
        name="cmp_topk",
    )(kr, vr, pek, pev, w1k, w2k, w1v, w2v, qt, mapt)


def _attn_kernel(qt_ref, ks_ref, kw_ref, vs_ref, vw_ref, selb_ref, oct_ref, gnt_ref, o_ref,
                 qa_ref, m_ref, l_ref, acc_ref, out_ref):
    grp = pl.program_id(1)
    qi = pl.program_id(2)
    ta = ATTN_TILE
    nh, dh = HEADS_PER_GROUP, HEAD_DIM
    nsel = selb_ref.shape[2]
    pos = qi * ta + lax.broadcasted_iota(jnp.int32, (1, ta), 1)
    tb = (pos >> SEL_SHIFT).astype(F32)
    tr = (pos & (SEL_BLOCK - 1)).astype(F32)
    arow = lax.broadcasted_iota(jnp.int32, (16, 1), 0)
    gate = _sigmoid(gnt_ref[...])

    def gate_row(c):
        return jnp.concatenate([gate[3 * j + c:3 * j + c + 1, :] for j in range(nh)], axis=1)

    for j in range(nh):
        slope = jnp.where(grp == 0, 2.0 ** -(j + 1), 2.0 ** -(nh + j + 1)).astype(F32)
        cols = slice(j * ta, (j + 1) * ta)
        qa_ref[0:dh, cols] = qt_ref[j * dh:(j + 1) * dh, :]
        qa_ref[dh:dh + nsel, cols] = selb_ref[0, 0]
        alibi = jnp.where(arow == 0, slope * SEL_BLOCK,
                          jnp.where(arow == 1, slope,
                                    jnp.where(arow == 2, -slope * SEL_BLOCK * tb,
                                              jnp.where(arow == 3, -slope * tr, 0.0))))
        qa_ref[dh + nsel:dh + nsel + 16, cols] = alibi.astype(BF16)
        qa_ref[dh + nsel + 16:, cols] = jnp.zeros((LANES - dh - nsel - 16, ta), BF16)
        out_ref[:, cols] = gate[3 * j:3 * j + 1, :] * oct_ref[j * dh:(j + 1) * dh, :]

    krow = lax.broadcasted_iota(jnp.int32, (ta, 1), 0)
    qcol = lax.broadcasted_iota(jnp.int32, (1, nh * ta), 1) & (ta - 1)
    causal_bias = jnp.where(qcol >= krow, 0.0, NEG_INF)
    far_bias = jnp.where(qcol < krow, 0.0, NEG_INF)

    def init():
        m_ref[...] = jnp.full(m_ref.shape, NEG_INF, F32)
        l_ref[...] = jnp.zeros(l_ref.shape, F32)
        acc_ref[...] = jnp.zeros(acc_ref.shape, F32)

    def step(k_ref, v_ref, kt, bias):
        k0 = pl.multiple_of(kt * ta, ta)
        k = k_ref[pl.ds(k0, ta), :]
        v = v_ref[:, pl.ds(k0, ta)]
        s = jnp.dot(k, qa_ref[...], preferred_element_type=F32)
        if bias is not None:
            s = s + bias
        m_old = m_ref[...]
        m_new = jnp.maximum(m_old, jnp.max(s, axis=0, keepdims=True))
        alpha = jnp.exp(m_old - m_new)
        p = jnp.exp(s - m_new)
        l_ref[...] = alpha * l_ref[...] + jnp.sum(p, axis=0, keepdims=True)
        acc_ref[...] = alpha * acc_ref[...] + jnp.dot(v, p.astype(BF16), preferred_element_type=F32)
        m_ref[...] = m_new

    def flush(c):
        out_ref[...] = out_ref[...] + gate_row(c) * (acc_ref[...] / l_ref[...])

    init()
    step(ks_ref, vs_ref, qi, causal_bias)

    def sel_body(i, carry):
        step(ks_ref, vs_ref, qi - i, None)
        return carry

    lax.fori_loop(1, qi + 1, sel_body, 0)
    flush(1)

    init()
    step(kw_ref, vw_ref, qi, causal_bias)

    @pl.when(qi >= 1)
    def _():
        step(kw_ref, vw_ref, qi - 1, None)

    @pl.when(qi >= 2)
    def _():
        step(kw_ref, vw_ref, qi - 2, far_bias)

    flush(2)
    for j in range(nh):
        o_ref[j * dh:(j + 1) * dh, :] = out_ref[:, j * ta:(j + 1) * ta].astype(BF16)


def _attention(qt, ksa, kwa, vt, selb, oct, gnt, batch, seq):
    ta = ATTN_TILE
    assert WINDOW == 2 * ta
    nqt = seq // ta
    g = N_KV_GROUPS
    nsel = selb.shape[2]
    nh, dh = HEADS_PER_GROUP, HEAD_DIM
    qtile = lambda rows: pl.BlockSpec((rows, ta), lambda b, j, t: (j, b * nqt + t))
    kspec = pl.BlockSpec((seq, LANES), lambda b, j, t: (b, j))
    return pl.pallas_call(
        _attn_kernel,
        grid=(batch, g, nqt),
        in_specs=[qtile(Q_ROWS), kspec, kspec,
                  pl.BlockSpec((dh, seq), lambda b, j, t: (j, b)),
                  pl.BlockSpec((dh, seq), lambda b, j, t: (g + j, b)),
                  pl.BlockSpec((1, 1, nsel, ta), lambda b, j, t: (b, j, 0, t)),
                  qtile(Q_ROWS), qtile(GATE_ROWS)],
        out_specs=qtile(Q_ROWS),
        out_shape=jax.ShapeDtypeStruct((ATTN_WIDTH, batch * seq), BF16),
        scratch_shapes=[pltpu.VMEM((LANES, nh * ta), BF16), pltpu.VMEM((1, nh * ta), F32),
                        pltpu.VMEM((1, nh * ta), F32), pltpu.VMEM((dh, nh * ta), F32),
                        pltpu.VMEM((dh, nh * ta), F32)],
        compiler_params=_params(("parallel", "parallel", "arbitrary")),
        name="nsa_flash",
    )(qt, ksa, kwa, vt, vt, selb, oct, gnt)


def _glu(u):
    return u[:, :CONV_CH] * _sigmoid(u[:, CONV_CH:])


def _conformer_kernel(cur_ref, prev_ref, dww_ref, dwb_ref, lng_ref, lnb_ref, wp_ref, y_ref, z_ref):
    ts, halo = CONV_TILE, CONV_HALO
    first = pl.program_id(1) == 0
    zp = _glu(prev_ref[0, ts - halo:, :])
    z_ref[0:halo, :] = jnp.where(first, 0.0, zp)
    z_ref[halo:, :] = _glu(cur_ref[0])
    acc = jnp.zeros((ts, CONV_CH), F32) + dwb_ref[...]
    base = halo - (CONV_WIDTH - 1)
    for k in range(CONV_WIDTH):
        acc = acc + dww_ref[k:k + 1, :] * z_ref[base + k:base + k + ts, :]
    mu = jnp.mean(acc, axis=-1, keepdims=True)
    cen = acc - mu
    var = jnp.mean(cen * cen, axis=-1, keepdims=True)
    y = cen * lax.rsqrt(var + NORM_EPS) * lng_ref[...] + lnb_ref[...]
    y = y * _sigmoid(y)
    y_ref[0] = jnp.dot(y.astype(BF16), wp_ref[...], preferred_element_type=F32)


def _conformer(u3, dww, dwb, lng, lnb, wp):
    b, seq, w = u3.shape
    ts = CONV_TILE
    full = lambda a: pl.BlockSpec(a.shape, lambda i, t: (0,) * a.ndim)
    return pl.pallas_call(
        _conformer_kernel,
        grid=(b, seq // ts),
        in_specs=[pl.BlockSpec((1, ts, w), lambda i, t: (i, t, 0)),
                  pl.BlockSpec((1, ts, w), lambda i, t: (i, jnp.maximum(t - 1, 0), 0)),
                  full(dww), full(dwb), full(lng), full(lnb), full(wp)],
        out_specs=pl.BlockSpec((1, ts, D_MODEL), lambda i, t: (i, t, 0)),
        out_shape=jax.ShapeDtypeStruct((b, seq, D_MODEL), F32),
        scratch_shapes=[pltpu.VMEM((ts + CONV_HALO, CONV_CH), F32)],
        compiler_params=_params(("parallel", "arbitrary")),
        name="conformer_conv",
    )(u3, u3, dww, dwb, lng, lnb, wp)


def _merge_kernel(ot_ref, yc_ref, gm_ref, x_ref, wa_ref, wo_ref, out_ref):
    ya = _dot_tn(ot_ref[...], wa_ref[...])
    gm = gm_ref[...]
    mix = _sigmoid(gm[:, :D_MODEL]) * ya + _sigmoid(gm[:, D_MODEL:]) * yc_ref[...]
    out_ref[...] = x_ref[...] + jnp.dot(mix.astype(BF16), wo_ref[...], preferred_element_type=F32)


def _merge(ot, yc2, gm2, x2, wa, wo):
    t = x2.shape[0]
    tm = ROW_TILE
    row = lambda n: pl.BlockSpec((tm, n), lambda i: (i, 0))
    full = lambda a: pl.BlockSpec(a.shape, lambda i: (0, 0))
    return pl.pallas_call(
        _merge_kernel,
        grid=(t // tm,),
        in_specs=[pl.BlockSpec((ATTN_WIDTH, tm), lambda i: (0, i)), row(D_MODEL), row(2 * D_MODEL), row(D_MODEL),
                  full(wa), full(wo)],
        out_specs=row(D_MODEL),
        out_shape=jax.ShapeDtypeStruct((t, D_MODEL), F32),
        compiler_params=_params(("parallel",)),
        name="merge_outproj",
    )(ot, yc2, gm2, x2, wa, wo)


def _ffn_kernel(x_ref, xh_ref, g_ref, wa_ref, wv_ref, cwa_ref, cwv_ref, cba_ref, cbv_ref, wd_ref, out_ref,
                h_ref, a_ref, v_ref, acc_ref, *, tiles_per_seq):
    tm, halo = ROW_TILE, FFN_HALO
    j = pl.program_id(1)

    @pl.when(j == 0)
    def _():
        seq_start = pl.program_id(0) % tiles_per_seq == 0
        hh = _rms(xh_ref[...], g_ref[...])
        h_ref[0:halo, :] = jnp.where(seq_start, 0.0, hh).astype(BF16)
        h_ref[halo:, :] = _rms(x_ref[...], g_ref[...]).astype(BF16)
        acc_ref[...] = jnp.zeros_like(acc_ref)

    h = h_ref[...]
    a_ref[...] = jnp.dot(h, wa_ref[0], preferred_element_type=F32)
    v_ref[...] = jnp.dot(h, wv_ref[0], preferred_element_type=F32)

    def conv(ref, w_ref, b_ref):
        out = b_ref[0]
        for k in range(FFN_CONV_WIDTH):
            off = halo - (FFN_CONV_WIDTH - 1) + k
            out = out + w_ref[0, k:k + 1, :] * ref[off:off + tm, :]
        return out

    ca = conv(a_ref, cwa_ref, cba_ref)
    cv = conv(v_ref, cwv_ref, cbv_ref)
    gated = (ca * _sigmoid(ca) * cv).astype(BF16)
    acc_ref[...] += jnp.dot(gated, wd_ref[0], preferred_element_type=F32)

    @pl.when(j == pl.num_programs(1) - 1)
    def _():
        out_ref[...] = x_ref[...] + acc_ref[...]


def _ffn(x2, g, wa, wv, cwa, cwv, cba, cbv, wd, seq):
    t = x2.shape[0]
    tm, halo = ROW_TILE, FFN_HALO
    nj, _, tf = wa.shape
    col3 = lambda a: pl.BlockSpec((1,) + a.shape[1:], lambda i, j: (j, 0, 0))
    return pl.pallas_call(
        functools.partial(_ffn_kernel, tiles_per_seq=seq // tm),
        grid=(t // tm, nj),
        in_specs=[pl.BlockSpec((tm, D_MODEL), lambda i, j: (i, 0)),
                  pl.BlockSpec((halo, D_MODEL), lambda i, j: (jnp.maximum(i * (tm // halo) - 1, 0), 0)),
                  pl.BlockSpec(g.shape, lambda i, j: (0, 0)),
                  col3(wa), col3(wv), col3(cwa), col3(cwv), col3(cba), col3(cbv), col3(wd)],
        out_specs=pl.BlockSpec((tm, D_MODEL), lambda i, j: (i, 0)),
        out_shape=jax.ShapeDtypeStruct((t, D_MODEL), F32),
        scratch_shapes=[pltpu.VMEM((tm + halo, D_MODEL), BF16), pltpu.VMEM((tm + halo, tf), F32),
                        pltpu.VMEM((tm + halo, tf), F32), pltpu.VMEM((tm, D_MODEL), F32)],
        compiler_params=_params(("parallel", "arbitrary")),
        name="conv_ffn",
    )(x2, x2, g, wa, wv, cwa, cwv, cba, cbv, wd)


def _norm_kernel(x_ref, g_ref, o_ref):
    o_ref[...] = _rms(x_ref[...], g_ref[...])


def _final_norm(x2, g):
    t = x2.shape[0]
    tm = ROW_TILE
    return pl.pallas_call(
        _norm_kernel,
        grid=(t // tm,),
        in_specs=[pl.BlockSpec((tm, D_MODEL), lambda i: (i, 0)), pl.BlockSpec(g.shape, lambda i: (0, 0))],
        out_specs=pl.BlockSpec((tm, D_MODEL), lambda i: (i, 0)),
        out_shape=jax.ShapeDtypeStruct((t, D_MODEL), F32),
        compiler_params=_params(("parallel",)),
        name="final_norm",
    )(x2, g)


def _sel_map_t(seq):
    ncmp = (seq - CMP_BLOCK) // CMP_STRIDE + 1
    nr = seq // CMP_STRIDE
    nsel = seq // SEL_BLOCK
    cs = np.arange(ncmp) * CMP_STRIDE
    ce = cs + CMP_BLOCK - 1
    ss = np.arange(nsel) * SEL_BLOCK
    se = ss + SEL_BLOCK - 1
    ov = np.minimum(ce[:, None], se[None, :]) - np.maximum(cs[:, None], ss[None, :]) + 1
    m = np.zeros((nsel, nr), np.float32)
    m[:, :ncmp] = (np.clip(ov, 0, None).astype(np.float32) / CMP_BLOCK).T
    return jnp.asarray(m)


def _key_aug(seq):
    pos = np.arange(seq)
    nsel = seq // SEL_BLOCK
    a = np.zeros((seq, AUG), np.float32)
    a[:, AUG_POS + 0] = pos // SEL_BLOCK
    a[:, AUG_POS + 1] = pos % SEL_BLOCK
    a[:, AUG_POS + 2] = 1.0
    a[:, AUG_POS + 3] = 1.0
    s = a.copy()
    s[pos, pos // SEL_BLOCK] = 1.0
    assert nsel <= AUG_POS
    z = np.zeros((seq, HEAD_DIM), np.float32)
    return jnp.asarray(np.concatenate([z, s, z, s, z, a, z, a], axis=1))


def _layer(x2, batch, seq, p):
    g_, dh = N_KV_GROUPS, HEAD_DIM
    ksa, kwa, kcv, uc2, gm2, qt, vt, gnt = _inproj(x2, p["norm1_g"], p["wn"], p["wt"], _key_aug(seq), seq)
    rows = seq // CMP_STRIDE
    kcv = kcv.reshape(batch, seq, 2, g_, dh).transpose(2, 0, 3, 1, 4).reshape(2, batch, g_, rows, CMP_STRIDE * dh)
    oct, selb = _cmp_branch(kcv[0], kcv[1], p["pek"], p["pev"], p["w1k"], p["w2k"], p["w1v"], p["w2v"], qt,
                            _sel_map_t(seq), seq)
    ot = _attention(qt, ksa, kwa, vt, selb, oct, gnt, batch, seq)
    yc = _conformer(uc2.reshape(batch, seq, 2 * CONV_CH), p["dww"], p["dwb"], p["lng"], p["lnb"], p["wconv"])
    x2 = _merge(ot, yc.reshape(batch * seq, D_MODEL), gm2, x2, p["wattn"], p["wout"])
    return _ffn(x2, p["norm2_g"], p["wa"], p["wv"], p["cwa"], p["cwv"], p["cba"], p["cbv"], p["wd"], seq)


def _prep_layer(l, norm1_g, w_in, cmp_pe_k, cmp_pe_v, cmp_k_w1, cmp_k_w2, cmp_v_w1, cmp_v_w2, w_attn_br,
                conv_dw_w, conv_dw_b, conv_ln_g, conv_ln_b, w_conv_br, w_out, norm2_g, ffn_w_up, ffn_dw_w,
                ffn_dw_b, ffn_w_down):
    w = w_in[l].astype(BF16)
    kvw = KV_WIDTH
    c_q = ATTN_WIDTH
    c_kc, c_vc, c_ks, c_vs, c_kw, c_vw = (c_q + i * kvw for i in range(6))
    c_gn = c_q + 6 * kvw
    c_uc = c_gn + N_GATES
    c_gm = c_uc + 2 * CONV_CH

    def widen(cols):
        k = cols.reshape(D_MODEL, N_KV_GROUPS, HEAD_DIM)
        return jnp.pad(k, ((0, 0), (0, 0), (0, AUG))).reshape(D_MODEL, N_KV_GROUPS * LANES)

    wn = jnp.concatenate([widen(w[:, c_ks:c_ks + kvw]), widen(w[:, c_kw:c_kw + kvw]),
                          w[:, c_kc:c_kc + 2 * kvw], w[:, c_uc:c_gm], w[:, c_gm:]], axis=1)
    gates = w[:, c_gn:c_uc].reshape(D_MODEL, N_KV_GROUPS, 3 * HEADS_PER_GROUP)
    gates = jnp.pad(gates, ((0, 0), (0, 0), (0, GATE_ROWS - 3 * HEADS_PER_GROUP)))
    wt = jnp.concatenate([w[:, :c_q], w[:, c_vs:c_vs + kvw], w[:, c_vw:c_vw + kvw],
                          gates.reshape(D_MODEL, N_KV_GROUPS * GATE_ROWS)], axis=1).T
    half = CMP_BLOCK // 2 * HEAD_DIM
    tf = FFN_COL_TILE
    nj = FFN_DIM // tf
    cols = lambda a: a.reshape(a.shape[0], nj, tf).transpose(1, 0, 2)
    up = ffn_w_up[l].astype(BF16)
    dw = ffn_dw_w[l]
    db = ffn_dw_b[l][None, :]
    return dict(
        norm1_g=norm1_g[l][None, :], wn=wn, wt=wt,
        pek=cmp_pe_k[l].reshape(2, half), pev=cmp_pe_v[l].reshape(2, half),
        w1k=cmp_k_w1[l].astype(BF16).reshape(2, half, CMP_HIDDEN), w2k=cmp_k_w2[l].astype(BF16),
        w1v=cmp_v_w1[l].astype(BF16).reshape(2, half, CMP_HIDDEN), w2v=cmp_v_w2[l].astype(BF16),
        wattn=w_attn_br[l].astype(BF16),
        dww=conv_dw_w[l], dwb=conv_dw_b[l][None, :], lng=conv_ln_g[l][None, :], lnb=conv_ln_b[l][None, :],
        wconv=w_conv_br[l].astype(BF16), wout=w_out[l].astype(BF16),
        norm2_g=norm2_g[l][None, :],
        wa=cols(up[:, :FFN_DIM]), wv=cols(up[:, FFN_DIM:]),
        cwa=cols(dw[:, :FFN_DIM]), cwv=cols(dw[:, FFN_DIM:]),
        cba=cols(db[:, :FFN_DIM]), cbv=cols(db[:, FFN_DIM:]),
        wd=ffn_w_down[l].astype(BF16).reshape(nj, tf, D_MODEL),
    )


def kernel(x, norm1_g, w_in, cmp_pe_k, cmp_pe_v, cmp_k_w1, cmp_k_w2, cmp_v_w1, cmp_v_w2, w_attn_br, conv_dw_w, conv_dw_b, conv_ln_g, conv_ln_b, w_conv_br, w_out, norm2_g, ffn_w_up, ffn_dw_w, ffn_dw_b, ffn_w_down, final_g):
    batch, seq, d = x.shape
    assert d == D_MODEL and seq % ROW_TILE == 0 and seq % CMP_Q_TILE == 0 and seq // SEL_BLOCK == AUG_POS
    x2 = x.reshape(batch * seq, d)
    for l in range(w_in.shape[0]):
        p = _prep_layer(l, norm1_g, w_in, cmp_pe_k, cmp_pe_v, cmp_k_w1, cmp_k_w2, cmp_v_w1, cmp_v_w2, w_attn_br,
                        conv_dw_w, conv_dw_b, conv_ln_g, conv_ln_b, w_conv_br, w_out, norm2_g, ffn_w_up,
                        ffn_dw_w, ffn_dw_b, ffn_w_down)
        x2 = _layer(x2, batch, seq, p)
    return _final_norm(x2, final_g[None, :]).reshape(batch, seq, d)
```

```python
import functools
import math

import numpy as np
import jax
import jax.numpy as jnp
from jax import lax
from jax.experimental import pallas as pl
from jax.experimental.pallas import tpu as pltpu

F32 = jnp.float32
BF16 = jnp.bfloat16

D_MODEL = 1024
N_HEADS = 8
HEAD_DIM = 64
N_KV_GROUPS = 2
HEADS_PER_GROUP = N_HEADS // N_KV_GROUPS
ATTN_WIDTH = N_HEADS * HEAD_DIM
KV_WIDTH = N_KV_GROUPS * HEAD_DIM
CMP_BLOCK = 32
CMP_STRIDE = 16
CMP_HIDDEN = 128
SEL_BLOCK = 64
SEL_TOP = 16
SEL_SHIFT = 6
WINDOW = 512
FORCE_BONUS = 1.0e4
NEG_INF = -1.0e30
CONV_CH = 512
CONV_WIDTH = 31
FFN_DIM = 2816
FFN_CONV_WIDTH = 3
NORM_EPS = 1e-6
N_GATES = 3 * N_HEADS
ATTN_SCALE = HEAD_DIM ** -0.5
UNSELECTED = -(2.0 ** 99)

VMEM_LIMIT_BYTES = 56 * 1024 * 1024
LANES = 128

ROW_TILE = 512
ATTN_TILE = 256
CMP_Q_TILE = 512
CONV_TILE = 256
CONV_HALO = 32
FFN_COL_TILE = 256
FFN_HALO = 8

AUG = LANES - HEAD_DIM
AUG_POS = N_HEADS * 4
GATE_ROWS = 16
Q_ROWS = HEADS_PER_GROUP * HEAD_DIM

assert AUG_POS == 2048 // SEL_BLOCK and AUG_POS + 4 <= AUG


def _params(sem):
    return pltpu.CompilerParams(dimension_semantics=sem, vmem_limit_bytes=VMEM_LIMIT_BYTES)


def _rms(x, g):
    y = x * lax.rsqrt(jnp.mean(x * x, axis=-1, keepdims=True) + NORM_EPS)
    return y * g


def _sigmoid(x):
    return 1.0 / (1.0 + jnp.exp(-x))


def _dot_nt(a, b, **kw):
    return lax.dot_general(a, b, (((1,), (1,)), ((), ())), preferred_element_type=F32, **kw)


def _dot_tn(a, b):
    return lax.dot_general(a, b, (((0,), (0,)), ((), ())), preferred_element_type=F32)


def _inproj_kernel(x_ref, g_ref, wn_ref, wt_ref, aug_ref,
                   kall_ref, kcv_ref, uc_ref, gm_ref, qt_ref, vt_ref, gnt_ref):
    h = _rms(x_ref[...], g_ref[...]).astype(BF16)
    c = 0
    for ref, extra in ((kall_ref.at[0], aug_ref.at[:, 0:2 * LANES]), (kall_ref.at[1], aug_ref.at[:, 2 * LANES:4 * LANES]),
                       (kcv_ref, None), (uc_ref, None), (gm_ref, None)):
        n = ref.shape[1]
        y = jnp.dot(h, wn_ref[:, c:c + n], preferred_element_type=F32)
        if extra is not None:
            y = y + extra[...]
        ref[...] = y.astype(ref.dtype)
        c += n
    t = _dot_nt(wt_ref[...], h)
    nq, nv = qt_ref.shape[0], vt_ref.shape[0]
    qt_ref[...] = (t[0:nq] * ATTN_SCALE).astype(BF16)
    vt_ref[...] = t[nq:nq + nv].astype(BF16)
    gnt_ref[...] = t[nq + nv:]


def _inproj(x2, g, wn, wt, aug, seq):
    t = x2.shape[0]
    tm = ROW_TILE
    row = lambda n: pl.BlockSpec((tm, n), lambda i: (i, 0))
    col = lambda n: pl.BlockSpec((n, tm), lambda i: (0, i))
    full = lambda a: pl.BlockSpec(a.shape, lambda i: (0, 0))
    kw = N_KV_GROUPS * LANES
    nat = ((2 * KV_WIDTH, BF16), (2 * CONV_CH, F32), (2 * D_MODEL, F32))
    trn = ((ATTN_WIDTH, BF16), (2 * KV_WIDTH, BF16), (N_KV_GROUPS * GATE_ROWS, F32))
    return pl.pallas_call(
        _inproj_kernel,
        grid=(t // tm,),
        in_specs=[row(D_MODEL), full(g), full(wn), full(wt),
                  pl.BlockSpec((tm, 2 * kw), lambda i: (i % (seq // tm), 0))],
        out_specs=[pl.BlockSpec((2, tm, kw), lambda i: (0, i, 0))] + [row(n) for n, _ in nat]
                  + [col(n) for n, _ in trn],
        out_shape=[jax.ShapeDtypeStruct((2, t, kw), BF16)] + [jax.ShapeDtypeStruct((t, n), d) for n, d in nat]
                  + [jax.ShapeDtypeStruct((n, t), d) for n, d in trn],
        compiler_params=_params(("parallel",)),
        name="inproj",
    )(x2, g, wn, wt, aug)


def _gelu_tanh(x):
    return 0.5 * x * (1.0 + jnp.tanh(math.sqrt(2.0 / math.pi) * (x + 0.044715 * (x * x * x))))


def _compress(r_ref, pe_ref, w1_ref, w2_ref):
    r = r_ref[0, 0].astype(F32)
    top = (r + pe_ref[0:1, :]).astype(BF16)
    bot = (r + pe_ref[1:2, :]).astype(BF16)
    a = jnp.dot(top, w1_ref[0], preferred_element_type=F32)
    b = jnp.dot(bot, w1_ref[1], preferred_element_type=F32)
    nr = a.shape[0]
    pre = a + pltpu.roll(b, nr - 1, 0)
    hid = _gelu_tanh(pre).astype(BF16)
    return jnp.dot(hid, w2_ref[...], preferred_element_type=F32)


def _cmp_kernel(kr_ref, vr_ref, pek_ref, pev_ref, w1k_ref, w2k_ref, w1v_ref, w2v_ref,
                qt_ref, mapt_ref, oct_ref, selb_ref, *, seq):
    grp = pl.program_id(1)
    kc = _compress(kr_ref, pek_ref, w1k_ref, w2k_ref).astype(BF16)
    vct = jnp.transpose(_compress(vr_ref, pev_ref, w1v_ref, w2v_ref)).astype(BF16)
    nr = kc.shape[0]
    nsel = seq // SEL_BLOCK
    tq = CMP_Q_TILE
    c_end = lax.broadcasted_iota(jnp.int32, (nr, 1), 0) * CMP_STRIDE + (CMP_BLOCK - 1)
    blk = lax.broadcasted_iota(jnp.int32, (nsel, 1), 0)

    def tile(ti, carry):
        t0 = pl.multiple_of(ti * tq, tq)
        pos = t0 + lax.broadcasted_iota(jnp.int32, (1, tq), 1)
        dist = pos - c_end
        valid = dist >= 0
        distf = dist.astype(F32)
        psum = jnp.zeros((nr, tq), F32)
        for j in range(HEADS_PER_GROUP):
            slope = jnp.where(grp == 0, 2.0 ** -(j + 1), 2.0 ** -(HEADS_PER_GROUP + j + 1)).astype(F32)
            rows = pl.ds(j * HEAD_DIM, HEAD_DIM)
            s = jnp.dot(kc, qt_ref[rows, pl.ds(t0, tq)], preferred_element_type=F32)
            s = jnp.where(valid, s - slope * distf, NEG_INF)
            m = jnp.max(s, axis=0, keepdims=True)
            e = jnp.where(valid, jnp.exp(s - m), 0.0)
            l = jnp.sum(e, axis=0, keepdims=True)
            p = e / jnp.where(l > 0.0, l, 1.0)
            psum = psum + p
            oct_ref[rows, pl.ds(t0, tq)] = jnp.dot(vct, p.astype(BF16), preferred_element_type=F32)
        imp = jnp.dot(mapt_ref[...], psum, precision=lax.Precision.HIGHEST, preferred_element_type=F32)
        cur = pos >> SEL_SHIFT
        forced = (blk == 0) | (blk == cur) | (blk == cur - 1)
        causal = blk <= cur
        val = jnp.where(causal, imp + jnp.where(forced, FORCE_BONUS, 0.0), NEG_INF)
        cnt = jnp.zeros((nsel, tq), F32)
        for i in range(nsel):
            vi = val[i:i + 1, :]
            tie = (blk > i).astype(F32)
            cnt = cnt + jnp.where(vi > val, 1.0, jnp.where(vi == val, tie, 0.0))
        n_top = min(SEL_TOP, nsel)
        selb_ref[0, 0, :, pl.ds(t0, tq)] = jnp.where(causal & (cnt < n_top), 0.0, UNSELECTED).astype(BF16)
        return carry

    lax.fori_loop(0, seq // tq, tile, 0)


def _cmp_branch(kr, vr, pek, pev, w1k, w2k, w1v, w2v, qt, mapt, seq):
    b, g, nr, rw = kr.shape
    nsel = seq // SEL_BLOCK
    bg4 = lambda shape: pl.BlockSpec((1, 1) + shape, lambda i, j: (i, j, 0, 0))
    full = lambda a: pl.BlockSpec(a.shape, lambda i, j: (0,) * a.ndim)
    heads = pl.BlockSpec((Q_ROWS, seq), lambda i, j: (j, i))
    return pl.pallas_call(
        functools.partial(_cmp_kernel, seq=seq),
        grid=(b, g),
        in_specs=[bg4((nr, rw)), bg4((nr, rw)), full(pek), full(pev), full(w1k), full(w2k), full(w1v), full(w2v),
                  heads, full(mapt)],
        out_specs=[heads, bg4((nsel, seq))],
        out_shape=[jax.ShapeDtypeStruct((ATTN_WIDTH, b * seq), F32),
                   jax.ShapeDtypeStruct((b, g, nsel, seq), BF16)],
        compiler_params=_params(("parallel", "parallel")),
        name="cmp_topk",
    )(kr, vr, pek, pev, w1k, w2k, w1v, w2v, qt, mapt)


def _attn_kernel(qt_ref, k_ref, v_ref, selb_ref, oct_ref, gnt_ref, o_ref,
                 qa_ref, s_ref, p_ref, a_ref, c_ref, m_ref, l_ref, acc_ref, out_ref):
    grp = pl.program_id(1)
    qi = pl.program_id(2)
    ta = ATTN_TILE
    nh, dh = HEADS_PER_GROUP, HEAD_DIM
    nsel = selb_ref.shape[2]
    pos = qi * ta + lax.broadcasted_iota(jnp.int32, (1, ta), 1)
    tb = (pos >> SEL_SHIFT).astype(F32)
    tr = (pos & (SEL_BLOCK - 1)).astype(F32)
    arow = lax.broadcasted_iota(jnp.int32, (16, 1), 0)
    gate = _sigmoid(gnt_ref[...])

    def gate_row(c):
        return jnp.concatenate([gate[3 * j + c:3 * j + c + 1, :] for j in range(nh)], axis=1)

    for j in range(nh):
        slope = jnp.where(grp == 0, 2.0 ** -(j + 1), 2.0 ** -(nh + j + 1)).astype(F32)
        cols = slice(j * ta, (j + 1) * ta)
        qa_ref[0:dh, cols] = qt_ref[j * dh:(j + 1) * dh, :]
        qa_ref[dh:dh + nsel, cols] = selb_ref[0, 0]
        alibi = jnp.where(arow == 0, slope * SEL_BLOCK,
                          jnp.where(arow == 1, slope,
                                    jnp.where(arow == 2, -slope * SEL_BLOCK * tb,
                                              jnp.where(arow == 3, -slope * tr, 0.0))))
        qa_ref[dh + nsel:dh + nsel + 16, cols] = alibi.astype(BF16)
        qa_ref[dh + nsel + 16:, cols] = jnp.zeros((LANES - dh - nsel - 16, ta), BF16)
        out_ref[:, cols] = gate[3 * j:3 * j + 1, :] * oct_ref[j * dh:(j + 1) * dh, :]

    qa = qa_ref[...]
    krow = lax.broadcasted_iota(jnp.int32, (ta, 1), 0)
    qcol = lax.broadcasted_iota(jnp.int32, (1, ta), 1)
    lanes4 = lambda b: jnp.concatenate([b] * nh, axis=1)

    def qk(br, kt):
        return jnp.dot(k_ref[br, pl.ds(pl.multiple_of(kt * ta, ta), ta), :], qa, preferred_element_type=F32)

    def pv(br, kt, p):
        v = v_ref[br, 0, :, pl.ds(pl.multiple_of(kt * ta, ta), ta)]
        return jnp.dot(v, p.astype(BF16), preferred_element_type=F32)

    far = WINDOW // ta
    causal = lanes4(jnp.where(qcol >= krow, 0.0, NEG_INF))
    kt_far = jnp.maximum(qi - far, 0)
    beyond = lanes4(jnp.where((qcol < krow) & (qi >= far), 0.0, NEG_INF))
    s_sel = qk(0, qi) + causal
    m_sel = jnp.max(s_sel, axis=0, keepdims=True)
    p_sel = jnp.exp(s_sel - m_sel)
    m_ref[0] = m_sel
    l_ref[0] = jnp.sum(p_sel, axis=0, keepdims=True)
    acc_ref[0] = pv(0, qi, p_sel)
    s_win = qk(1, qi) + causal
    s_far = qk(1, kt_far) + beyond
    m_win = jnp.maximum(jnp.max(s_win, axis=0, keepdims=True), jnp.max(s_far, axis=0, keepdims=True))
    p_win = jnp.exp(s_win - m_win)
    p_far = jnp.exp(s_far - m_win)
    m_ref[1] = m_win
    l_ref[1] = jnp.sum(p_win, axis=0, keepdims=True) + jnp.sum(p_far, axis=0, keepdims=True)
    acc_ref[1] = pv(1, qi, p_win) + pv(1, kt_far, p_far)

    n_tasks = qi + 1

    def task(n):
        n = jnp.minimum(n, n_tasks - 1)
        return jnp.where(n == 0, 1, 0), qi - jnp.maximum(n, 1)

    def scores(n, slot):
        s = qk(*task(n))
        s_ref[slot] = s
        c_ref[slot] = jnp.max(s, axis=0, keepdims=True)

    def softmax(n, slot):
        br, _ = task(n)
        m_old = m_ref[br]
        m_new = jnp.maximum(m_old, c_ref[slot])
        alpha = jnp.exp(m_old - m_new)
        p = jnp.exp(s_ref[slot] - m_new)
        l_ref[br] = alpha * l_ref[br] + jnp.sum(p, axis=0, keepdims=True)
        m_ref[br] = m_new
        a_ref[slot] = alpha
        p_ref[slot] = p.astype(BF16)

    def values(n, slot):
        br, kt = task(n)
        acc_ref[br] = a_ref[slot] * acc_ref[br] + pv(br, kt, p_ref[slot])

    def stage(n, slot):
        scores(n + 2, slot)
        softmax(n + 1, 1 - slot)
        values(n, slot)

    @pl.when(qi >= 1)
    def _():
        scores(0, 0)
        scores(1, 1)
        softmax(0, 0)
        n_full = n_tasks - 1

        def pair(i, carry):
            stage(2 * i, 0)
            stage(2 * i + 1, 1)
            return carry

        lax.fori_loop(0, n_full // 2, pair, 0)

        @pl.when(n_full % 2 == 1)
        def _():
            stage(n_full - 1, 0)
            values(n_full, 1)

        @pl.when(n_full % 2 == 0)
        def _():
            values(n_full, 0)

    out = (out_ref[...] + gate_row(1) * (acc_ref[0] / l_ref[0]) + gate_row(2) * (acc_ref[1] / l_ref[1]))
    for j in range(nh):
        o_ref[j * dh:(j + 1) * dh, :] = out[:, j * ta:(j + 1) * ta].astype(BF16)


def _attention(qt, kall, vt, selb, oct, gnt, batch, seq):
    ta = ATTN_TILE
    assert WINDOW % ta == 0
    nqt = seq // ta
    g = N_KV_GROUPS
    nsel = selb.shape[2]
    nh, dh = HEADS_PER_GROUP, HEAD_DIM
    qtile = lambda rows: pl.BlockSpec((rows, ta), lambda b, j, t: (j, b * nqt + t))
    vall = vt.reshape(2, g, dh, batch * seq)
    return pl.pallas_call(
        _attn_kernel,
        grid=(batch, g, nqt),
        in_specs=[qtile(Q_ROWS),
                  pl.BlockSpec((2, seq, LANES), lambda b, j, t: (0, b, j)),
                  pl.BlockSpec((2, 1, dh, seq), lambda b, j, t: (0, j, 0, b)),
                  pl.BlockSpec((1, 1, nsel, ta), lambda b, j, t: (b, j, 0, t)),
                  qtile(Q_ROWS), qtile(GATE_ROWS)],
        out_specs=qtile(Q_ROWS),
        out_shape=jax.ShapeDtypeStruct((ATTN_WIDTH, batch * seq), BF16),
        scratch_shapes=[pltpu.VMEM((LANES, nh * ta), BF16), pltpu.VMEM((2, ta, nh * ta), F32),
                        pltpu.VMEM((2, ta, nh * ta), BF16), pltpu.VMEM((2, 1, nh * ta), F32),
                        pltpu.VMEM((2, 1, nh * ta), F32), pltpu.VMEM((2, 1, nh * ta), F32),
                        pltpu.VMEM((2, 1, nh * ta), F32), pltpu.VMEM((2, dh, nh * ta), F32),
                        pltpu.VMEM((dh, nh * ta), F32)],
        compiler_params=_params(("parallel", "parallel", "arbitrary")),
        name="nsa_flash",
    )(qt, kall, vall, selb, oct, gnt)


def _glu(u):
    return u[:, :CONV_CH] * _sigmoid(u[:, CONV_CH:])


def _conformer_kernel(cur_ref, prev_ref, dww_ref, dwb_ref, lng_ref, lnb_ref, wp_ref, y_ref, z_ref):
    ts, halo = CONV_TILE, CONV_HALO
    first = pl.program_id(1) == 0
    zp = _glu(prev_ref[0, ts - halo:, :])
    z_ref[0:halo, :] = jnp.where(first, 0.0, zp)
    z_ref[halo:, :] = _glu(cur_ref[0])
    base = halo - (CONV_WIDTH - 1)
    sub = 8
    acc = jnp.zeros((ts, CONV_CH), F32) + dwb_ref[...]
    for r in range(sub):
        part = None
        ext = ts + (sub if r else 0)
        for s in range(r, base + CONV_WIDTH, sub):
            if s < base:
                continue
            term = dww_ref[s - base:s - base + 1, :] * z_ref[s - r:s - r + ext, :]
            part = term if part is None else part + term
        acc = acc + part[r:r + ts, :]
    mu = jnp.mean(acc, axis=-1, keepdims=True)
    cen = acc - mu
    var = jnp.mean(cen * cen, axis=-1, keepdims=True)
    y = cen * lax.rsqrt(var + NORM_EPS) * lng_ref[...] + lnb_ref[...]
    y = y * _sigmoid(y)
    y_ref[0] = jnp.dot(y.astype(BF16), wp_ref[...], preferred_element_type=F32)


def _conformer(u3, dww, dwb, lng, lnb, wp):
    b, seq, w = u3.shape
    ts = CONV_TILE
    full = lambda a: pl.BlockSpec(a.shape, lambda i, t: (0,) * a.ndim)
    return pl.pallas_call(
        _conformer_kernel,
        grid=(b, seq // ts),
        in_specs=[pl.BlockSpec((1, ts, w), lambda i, t: (i, t, 0)),
                  pl.BlockSpec((1, ts, w), lambda i, t: (i, jnp.maximum(t - 1, 0), 0)),
                  full(dww), full(dwb), full(lng), full(lnb), full(wp)],
        out_specs=pl.BlockSpec((1, ts, D_MODEL), lambda i, t: (i, t, 0)),
        out_shape=jax.ShapeDtypeStruct((b, seq, D_MODEL), F32),
        scratch_shapes=[pltpu.VMEM((ts + CONV_HALO, CONV_CH), F32)],
        compiler_params=_params(("parallel", "arbitrary")),
        name="conformer_conv",
    )(u3, u3, dww, dwb, lng, lnb, wp)


def _merge_kernel(ot_ref, yc_ref, gm_ref, x_ref, wa_ref, wo_ref, out_ref):
    ya = _dot_tn(ot_ref[...], wa_ref[...])
    gm = gm_ref[...]
    mix = _sigmoid(gm[:, :D_MODEL]) * ya + _sigmoid(gm[:, D_MODEL:]) * yc_ref[...]
    out_ref[...] = x_ref[...] + jnp.dot(mix.astype(BF16), wo_ref[...], preferred_element_type=F32)


def _merge(ot, yc2, gm2, x2, wa, wo):
    t = x2.shape[0]
    tm = ROW_TILE
    row = lambda n: pl.BlockSpec((tm, n), lambda i: (i, 0))
    full = lambda a: pl.BlockSpec(a.shape, lambda i: (0, 0))
    return pl.pallas_call(
        _merge_kernel,
        grid=(t // tm,),
        in_specs=[pl.BlockSpec((ATTN_WIDTH, tm), lambda i: (0, i)), row(D_MODEL), row(2 * D_MODEL), row(D_MODEL),
                  full(wa), full(wo)],
        out_specs=row(D_MODEL),
        out_shape=jax.ShapeDtypeStruct((t, D_MODEL), F32),
        compiler_params=_params(("parallel",)),
        name="merge_outproj",
    )(ot, yc2, gm2, x2, wa, wo)


def _ffn_kernel(x_ref, xh_ref, g_ref, wa_ref, wv_ref, cwa_ref, cwv_ref, cba_ref, cbv_ref, wd_ref, *rest,
                tiles_per_seq, final_norm):
    if final_norm:
        fg_ref, out_ref, h_ref, a_ref, v_ref, gt_ref = rest
    else:
        out_ref, h_ref, a_ref, v_ref, gt_ref = rest
    tm, halo = ROW_TILE, FFN_HALO
    nj, _, tf = wa_ref.shape
    seq_start = pl.program_id(0) % tiles_per_seq == 0
    hh = _rms(xh_ref[...], g_ref[...])
    h_ref[0:halo, :] = jnp.where(seq_start, 0.0, hh).astype(BF16)
    h_ref[halo:, :] = _rms(x_ref[...], g_ref[...]).astype(BF16)
    h = h_ref[...]

    def conv(ref, w, b):
        out = b
        for k in range(FFN_CONV_WIDTH):
            off = halo - (FFN_CONV_WIDTH - 1) + k
            out = out + w[k:k + 1, :] * ref[off:off + tm, :]
        return out

    for j in range(nj):
        slot = j % 2
        a_ref[slot] = jnp.dot(h, wa_ref[j], preferred_element_type=F32)
        v_ref[slot] = jnp.dot(h, wv_ref[j], preferred_element_type=F32)
        ca = conv(a_ref.at[slot], cwa_ref[j], cba_ref[j])
        cv = conv(v_ref.at[slot], cwv_ref[j], cbv_ref[j])
        gt_ref[:, j * tf:(j + 1) * tf] = (ca * _sigmoid(ca) * cv).astype(BF16)
    y = x_ref[...] + jnp.dot(gt_ref[...], wd_ref[...], preferred_element_type=F32)
    out_ref[...] = _rms(y, fg_ref[...]) if final_norm else y


def _ffn(x2, g, wa, wv, cwa, cwv, cba, cbv, wd, seq, final_g=None):
    t = x2.shape[0]
    tm, halo = ROW_TILE, FFN_HALO
    nj, _, tf = wa.shape
    once = lambda a: pl.BlockSpec(a.shape, lambda i: (0,) * a.ndim, pipeline_mode=pl.Buffered(1))
    args = [x2, x2, g, wa, wv, cwa, cwv, cba, cbv, wd] + ([final_g] if final_g is not None else [])
    return pl.pallas_call(
        functools.partial(_ffn_kernel, tiles_per_seq=seq // tm, final_norm=final_g is not None),
        grid=(t // tm,),
        in_specs=[pl.BlockSpec((tm, D_MODEL), lambda i: (i, 0)),
                  pl.BlockSpec((halo, D_MODEL), lambda i: (jnp.maximum(i * (tm // halo) - 1, 0), 0))]
                 + [once(a) for a in args[2:]],
        out_specs=pl.BlockSpec((tm, D_MODEL), lambda i: (i, 0)),
        out_shape=jax.ShapeDtypeStruct((t, D_MODEL), F32),
        scratch_shapes=[pltpu.VMEM((tm + halo, D_MODEL), BF16), pltpu.VMEM((2, tm + halo, tf), F32),
                        pltpu.VMEM((2, tm + halo, tf), F32), pltpu.VMEM((tm, nj * tf), BF16)],
        compiler_params=_params(("parallel",)),
        name="conv_ffn",
    )(*args)


def _sel_map_t(seq):
    ncmp = (seq - CMP_BLOCK) // CMP_STRIDE + 1
    nr = seq // CMP_STRIDE
    nsel = seq // SEL_BLOCK
    cs = np.arange(ncmp) * CMP_STRIDE
    ce = cs + CMP_BLOCK - 1
    ss = np.arange(nsel) * SEL_BLOCK
    se = ss + SEL_BLOCK - 1
    ov = np.minimum(ce[:, None], se[None, :]) - np.maximum(cs[:, None], ss[None, :]) + 1
    m = np.zeros((nsel, nr), np.float32)
    m[:, :ncmp] = (np.clip(ov, 0, None).astype(np.float32) / CMP_BLOCK).T
    return jnp.asarray(m)


def _key_aug(seq):
    pos = np.arange(seq)
    nsel = seq // SEL_BLOCK
    a = np.zeros((seq, AUG), np.float32)
    a[:, AUG_POS + 0] = pos // SEL_BLOCK
    a[:, AUG_POS + 1] = pos % SEL_BLOCK
    a[:, AUG_POS + 2] = 1.0
    a[:, AUG_POS + 3] = 1.0
    s = a.copy()
    s[pos, pos // SEL_BLOCK] = 1.0
    assert nsel <= AUG_POS
    z = np.zeros((seq, HEAD_DIM), np.float32)
    return jnp.asarray(np.concatenate([z, s, z, s, z, a, z, a], axis=1))


def _layer(x2, batch, seq, p, final_g):
    g_, dh = N_KV_GROUPS, HEAD_DIM
    kall, kcv, uc2, gm2, qt, vt, gnt = _inproj(x2, p["norm1_g"], p["wn"], p["wt"], _key_aug(seq), seq)
    rows = seq // CMP_STRIDE
    kcv = kcv.reshape(batch, seq, 2, g_, dh).transpose(2, 0, 3, 1, 4).reshape(2, batch, g_, rows, CMP_STRIDE * dh)
    oct, selb = _cmp_branch(kcv[0], kcv[1], p["pek"], p["pev"], p["w1k"], p["w2k"], p["w1v"], p["w2v"], qt,
                            _sel_map_t(seq), seq)
    ot = _attention(qt, kall, vt, selb, oct, gnt, batch, seq)
    yc = _conformer(uc2.reshape(batch, seq, 2 * CONV_CH), p["dww"], p["dwb"], p["lng"], p["lnb"], p["wconv"])
    x2 = _merge(ot, yc.reshape(batch * seq, D_MODEL), gm2, x2, p["wattn"], p["wout"])
    return _ffn(x2, p["norm2_g"], p["wa"], p["wv"], p["cwa"], p["cwv"], p["cba"], p["cbv"], p["wd"], seq, final_g)


def _prep_layer(l, norm1_g, w_in, cmp_pe_k, cmp_pe_v, cmp_k_w1, cmp_k_w2, cmp_v_w1, cmp_v_w2, w_attn_br,
                conv_dw_w, conv_dw_b, conv_ln_g, conv_ln_b, w_conv_br, w_out, norm2_g, ffn_w_up, ffn_dw_w,
                ffn_dw_b, ffn_w_down):
    w = w_in[l].astype(BF16)
    kvw = KV_WIDTH
    c_q = ATTN_WIDTH
    c_kc, c_vc, c_ks, c_vs, c_kw, c_vw = (c_q + i * kvw for i in range(6))
    c_gn = c_q + 6 * kvw
    c_uc = c_gn + N_GATES
    c_gm = c_uc + 2 * CONV_CH

    def widen(cols):
        k = cols.reshape(D_MODEL, N_KV_GROUPS, HEAD_DIM)
        return jnp.pad(k, ((0, 0), (0, 0), (0, AUG))).reshape(D_MODEL, N_KV_GROUPS * LANES)

    wn = jnp.concatenate([widen(w[:, c_ks:c_ks + kvw]), widen(w[:, c_kw:c_kw + kvw]),
                          w[:, c_kc:c_kc + 2 * kvw], w[:, c_uc:c_gm], w[:, c_gm:]], axis=1)
    gates = w[:, c_gn:c_uc].reshape(D_MODEL, N_KV_GROUPS, 3 * HEADS_PER_GROUP)
    gates = jnp.pad(gates, ((0, 0), (0, 0), (0, GATE_ROWS - 3 * HEADS_PER_GROUP)))
    wt = jnp.concatenate([w[:, :c_q], w[:, c_vs:c_vs + kvw], w[:, c_vw:c_vw + kvw],
                          gates.reshape(D_MODEL, N_KV_GROUPS * GATE_ROWS)], axis=1).T
    half = CMP_BLOCK // 2 * HEAD_DIM
    tf = FFN_COL_TILE
    nj = FFN_DIM // tf
    cols = lambda a: a.reshape(a.shape[0], nj, tf).transpose(1, 0, 2)
    up = ffn_w_up[l].astype(BF16)
    dw = ffn_dw_w[l]
    db = ffn_dw_b[l][None, :]
    return dict(
        norm1_g=norm1_g[l][None, :], wn=wn, wt=wt,
        pek=cmp_pe_k[l].reshape(2, half), pev=cmp_pe_v[l].reshape(2, half),
        w1k=cmp_k_w1[l].astype(BF16).reshape(2, half, CMP_HIDDEN), w2k=cmp_k_w2[l].astype(BF16),
        w1v=cmp_v_w1[l].astype(BF16).reshape(2, half, CMP_HIDDEN), w2v=cmp_v_w2[l].astype(BF16),
        wattn=w_attn_br[l].astype(BF16),
        dww=conv_dw_w[l], dwb=conv_dw_b[l][None, :], lng=conv_ln_g[l][None, :], lnb=conv_ln_b[l][None, :],
        wconv=w_conv_br[l].astype(BF16), wout=w_out[l].astype(BF16),
        norm2_g=norm2_g[l][None, :],
        wa=cols(up[:, :FFN_DIM]), wv=cols(up[:, FFN_DIM:]),
        cwa=cols(dw[:, :FFN_DIM]), cwv=cols(dw[:, FFN_DIM:]),
        cba=cols(db[:, :FFN_DIM]), cbv=cols(db[:, FFN_DIM:]),
        wd=ffn_w_down[l].astype(BF16),
    )


def kernel(x, norm1_g, w_in, cmp_pe_k, cmp_pe_v, cmp_k_w1, cmp_k_w2, cmp_v_w1, cmp_v_w2, w_attn_br, conv_dw_w, conv_dw_b, conv_ln_g, conv_ln_b, w_conv_br, w_out, norm2_g, ffn_w_up, ffn_dw_w, ffn_dw_b, ffn_w_down, final_g):
    batch, seq, d = x.shape
    assert d == D_MODEL and seq % ROW_TILE == 0 and seq % CMP_Q_TILE == 0 and seq // SEL_BLOCK == AUG_POS
    x2 = x.reshape(batch * seq, d)
    for l in range(w_in.shape[0]):
        p = _prep_layer(l, norm1_g, w_in, cmp_pe_k, cmp_pe_v, cmp_k_w1, cmp_k_w2, cmp_v_w1, cmp_v_w2, w_attn_br,
                        conv_dw_w, conv_dw_b, conv_ln_g, conv_ln_b, w_conv_br, w_out, norm2_g, ffn_w_up,
                        ffn_dw_w, ffn_dw_b, ffn_w_down)
        last = l == w_in.shape[0] - 1
        x2 = _layer(x2, batch, seq, p, final_g[None, :] if last else None)
    return x2.reshape(batch, seq, d)
```

```python
import functools
import math

import numpy as np
import jax
import jax.numpy as jnp
from jax import lax
from jax.experimental import pallas as pl
from jax.experimental.pallas import tpu as pltpu

F32 = jnp.float32
BF16 = jnp.bfloat16

D_MODEL = 1024
N_HEADS = 8
HEAD_DIM = 64
N_KV_GROUPS = 2
HEADS_PER_GROUP = N_HEADS // N_KV_GROUPS
ATTN_WIDTH = N_HEADS * HEAD_DIM
KV_WIDTH = N_KV_GROUPS * HEAD_DIM
CMP_BLOCK = 32
CMP_STRIDE = 16
CMP_HIDDEN = 128
SEL_BLOCK = 64
SEL_TOP = 16
SEL_SHIFT = 6
WINDOW = 512
FORCE_BONUS = 1.0e4
NEG_INF = -1.0e30
CONV_CH = 512
CONV_WIDTH = 31
FFN_DIM = 2816
FFN_CONV_WIDTH = 3
NORM_EPS = 1e-6
N_GATES = 3 * N_HEADS
ATTN_SCALE = HEAD_DIM ** -0.5
UNSELECTED = -(2.0 ** 99)

VMEM_LIMIT_BYTES = 56 * 1024 * 1024
LANES = 128

ROW_TILE = 512
FFN_ROW_TILE = 512
ATTN_TILE = 256
SOFTMAX_COLS = 128
CMP_Q_TILE = 512
CONV_TILE = 256
CONV_HALO = 32
FFN_COL_TILE = 256
FFN_HALO = 8

AUG = LANES - HEAD_DIM
AUG_POS = N_HEADS * 4
GATE_ROWS = 16
Q_ROWS = HEADS_PER_GROUP * HEAD_DIM

assert AUG_POS == 2048 // SEL_BLOCK and AUG_POS + 4 <= AUG


def _params(sem):
    return pltpu.CompilerParams(dimension_semantics=sem, vmem_limit_bytes=VMEM_LIMIT_BYTES)


def _rms(x, g):
    y = x * lax.rsqrt(jnp.mean(x * x, axis=-1, keepdims=True) + NORM_EPS)
    return y * g


def _sigmoid(x):
    return 1.0 / (1.0 + jnp.exp(-x))


def _dot_nt(a, b, **kw):
    return lax.dot_general(a, b, (((1,), (1,)), ((), ())), preferred_element_type=F32, **kw)


def _dot_tn(a, b):
    return lax.dot_general(a, b, (((0,), (0,)), ((), ())), preferred_element_type=F32)


def _inproj_kernel(x_ref, g_ref, wn_ref, wt_ref, aug_ref,
                   kall_ref, kcv_ref, uc_ref, gm_ref, qt_ref, vt_ref, gnt_ref):
    h = _rms(x_ref[...], g_ref[...]).astype(BF16)
    c = 0
    for ref, extra in ((kall_ref.at[0], aug_ref.at[:, 0:2 * LANES]), (kall_ref.at[1], aug_ref.at[:, 2 * LANES:4 * LANES]),
                       (kcv_ref, None), (uc_ref, None), (gm_ref, None)):
        n = ref.shape[1]
        y = jnp.dot(h, wn_ref[:, c:c + n], preferred_element_type=F32)
        if extra is not None:
            y = y + extra[...]
        ref[...] = y.astype(ref.dtype)
        c += n
    t = _dot_nt(wt_ref[...], h)
    nq, nv = qt_ref.shape[0], vt_ref.shape[0]
    qt_ref[...] = (t[0:nq] * ATTN_SCALE).astype(BF16)
    vt_ref[...] = t[nq:nq + nv].astype(BF16)
    gnt_ref[...] = t[nq + nv:]


def _inproj(x2, g, wn, wt, aug, seq):
    t = x2.shape[0]
    tm = ROW_TILE
    row = lambda n: pl.BlockSpec((tm, n), lambda i: (i, 0))
    col = lambda n: pl.BlockSpec((n, tm), lambda i: (0, i))
    full = lambda a: pl.BlockSpec(a.shape, lambda i: (0, 0))
    kw = N_KV_GROUPS * LANES
    nat = ((2 * KV_WIDTH, BF16), (2 * CONV_CH, BF16), (2 * D_MODEL, BF16))
    trn = ((ATTN_WIDTH, BF16), (2 * KV_WIDTH, BF16), (N_KV_GROUPS * GATE_ROWS, F32))
    return pl.pallas_call(
        _inproj_kernel,
        grid=(t // tm,),
        in_specs=[row(D_MODEL), full(g), full(wn), full(wt),
                  pl.BlockSpec((tm, 2 * kw), lambda i: (i % (seq // tm), 0))],
        out_specs=[pl.BlockSpec((2, tm, kw), lambda i: (0, i, 0))] + [row(n) for n, _ in nat]
                  + [col(n) for n, _ in trn],
        out_shape=[jax.ShapeDtypeStruct((2, t, kw), BF16)] + [jax.ShapeDtypeStruct((t, n), d) for n, d in nat]
                  + [jax.ShapeDtypeStruct((n, t), d) for n, d in trn],
        compiler_params=_params(("parallel",)),
        name="inproj",
    )(x2, g, wn, wt, aug)


def _gelu_tanh(x):
    return 0.5 * x * (1.0 + jnp.tanh(math.sqrt(2.0 / math.pi) * (x + 0.044715 * (x * x * x))))


def _compress(r_ref, pe_ref, w1_ref, w2_ref):
    r = r_ref[0, 0].astype(F32)
    top = (r + pe_ref[0:1, :]).astype(BF16)
    bot = (r + pe_ref[1:2, :]).astype(BF16)
    a = jnp.dot(top, w1_ref[0], preferred_element_type=F32)
    b = jnp.dot(bot, w1_ref[1], preferred_element_type=F32)
    nr = a.shape[0]
    pre = a + pltpu.roll(b, nr - 1, 0)
    hid = _gelu_tanh(pre).astype(BF16)
    return jnp.dot(hid, w2_ref[...], preferred_element_type=F32)


def _cmp_kernel(kr_ref, vr_ref, pek_ref, pev_ref, w1k_ref, w2k_ref, w1v_ref, w2v_ref,
                qt_ref, mapt_ref, oct_ref, selb_ref, *, seq):
    grp = pl.program_id(1)
    kc = _compress(kr_ref, pek_ref, w1k_ref, w2k_ref).astype(BF16)
    vct = jnp.transpose(_compress(vr_ref, pev_ref, w1v_ref, w2v_ref)).astype(BF16)
    nr = kc.shape[0]
    nsel = seq // SEL_BLOCK
    tq = CMP_Q_TILE
    c_end = lax.broadcasted_iota(jnp.int32, (nr, 1), 0) * CMP_STRIDE + (CMP_BLOCK - 1)
    blk = lax.broadcasted_iota(jnp.int32, (nsel, 1), 0)

    def tile(ti, carry):
        t0 = pl.multiple_of(ti * tq, tq)
        pos = t0 + lax.broadcasted_iota(jnp.int32, (1, tq), 1)
        dist = pos - c_end
        valid = dist >= 0
        distf = dist.astype(F32)
        psum = jnp.zeros((nr, tq), F32)
        for j in range(HEADS_PER_GROUP):
            slope = jnp.where(grp == 0, 2.0 ** -(j + 1), 2.0 ** -(HEADS_PER_GROUP + j + 1)).astype(F32)
            rows = pl.ds(j * HEAD_DIM, HEAD_DIM)
            s = jnp.dot(kc, qt_ref[rows, pl.ds(t0, tq)], preferred_element_type=F32)
            s = jnp.where(valid, s - slope * distf, NEG_INF)
            m = jnp.max(s, axis=0, keepdims=True)
            e = jnp.where(valid, jnp.exp(s - m), 0.0)
            l = jnp.sum(e, axis=0, keepdims=True)
            p = e / jnp.where(l > 0.0, l, 1.0)
            psum = psum + p
            oct_ref[rows, pl.ds(t0, tq)] = jnp.dot(vct, p.astype(BF16), preferred_element_type=F32)
        imp = jnp.dot(mapt_ref[...], psum, precision=lax.Precision.HIGHEST, preferred_element_type=F32)
        cur = pos >> SEL_SHIFT
        forced = (blk == 0) | (blk == cur) | (blk == cur - 1)
        causal = blk <= cur
        val = jnp.where(causal, imp + jnp.where(forced, FORCE_BONUS, 0.0), NEG_INF)
        cnt = jnp.zeros((nsel, tq), F32)
        for i in range(nsel):
            vi = val[i:i + 1, :]
            tie = (blk > i).astype(F32)
            cnt = cnt + jnp.where(vi > val, 1.0, jnp.where(vi == val, tie, 0.0))
        n_top = min(SEL_TOP, nsel)
        selb_ref[0, 0, :, pl.ds(t0, tq)] = jnp.where(causal & (cnt < n_top), 0.0, UNSELECTED).astype(BF16)
        return carry

    lax.fori_loop(0, seq // tq, tile, 0)


def _cmp_branch(kr, vr, pek, pev, w1k, w2k, w1v, w2v, qt, mapt, seq):
    b, g, nr, rw = kr.shape
    nsel = seq // SEL_BLOCK
    bg4 = lambda shape: pl.BlockSpec((1, 1) + shape, lambda i, j: (i, j, 0, 0))
    full = lambda a: pl.BlockSpec(a.shape, lambda i, j: (0,) * a.ndim)
    heads = pl.BlockSpec((Q_ROWS, seq), lambda i, j: (j, i))
    return pl.pallas_call(
        functools.partial(_cmp_kernel, seq=seq),
        grid=(b, g),
        in_specs=[bg4((nr, rw)), bg4((nr, rw)), full(pek), full(pev), full(w1k), full(w2k), full(w1v), full(w2v),
                  heads, full(mapt)],
        out_specs=[heads, bg4((nsel, seq))],
        out_shape=[jax.ShapeDtypeStruct((ATTN_WIDTH, b * seq), F32),
                   jax.ShapeDtypeStruct((b, g, nsel, seq), BF16)],
        compiler_params=_params(("parallel", "parallel")),
        name="cmp_topk",
    )(kr, vr, pek, pev, w1k, w2k, w1v, w2v, qt, mapt)


def _attn_kernel(qt_ref, k_ref, v_ref, selb_ref, oct_ref, gnt_ref, o_ref,
                 qa_ref, s_ref, p_ref, a_ref, c_ref, m_ref, l_ref, acc_ref, out_ref):
    qi = pl.program_id(1)
    ta = ATTN_TILE
    ng, nh, dh = N_KV_GROUPS, HEADS_PER_GROUP, HEAD_DIM
    nsel = selb_ref.shape[2]
    groups = range(ng)
    pos = qi * ta + lax.broadcasted_iota(jnp.int32, (1, ta), 1)
    tb = (pos >> SEL_SHIFT).astype(F32)
    tr = (pos & (SEL_BLOCK - 1)).astype(F32)
    arow = lax.broadcasted_iota(jnp.int32, (16, 1), 0)
    gate = _sigmoid(gnt_ref[...])

    def gate_row(g, c):
        r = g * GATE_ROWS + c
        return jnp.concatenate([gate[r + 3 * j:r + 3 * j + 1, :] for j in range(nh)], axis=1)

    for g in groups:
        for j in range(nh):
            slope = 2.0 ** -(g * nh + j + 1)
            cols = slice(j * ta, (j + 1) * ta)
            rows = slice((g * nh + j) * dh, (g * nh + j + 1) * dh)
            qa_ref[g, 0:dh, cols] = qt_ref[rows, :]
            qa_ref[g, dh:dh + nsel, cols] = selb_ref[0, g]
            alibi = jnp.where(arow == 0, slope * SEL_BLOCK,
                              jnp.where(arow == 1, slope,
                                        jnp.where(arow == 2, -slope * SEL_BLOCK * tb,
                                                  jnp.where(arow == 3, -slope * tr, 0.0))))
            qa_ref[g, dh + nsel:dh + nsel + 16, cols] = alibi.astype(BF16)
            qa_ref[g, dh + nsel + 16:, cols] = jnp.zeros((LANES - dh - nsel - 16, ta), BF16)
            out_ref[g, :, cols] = gate[g * GATE_ROWS + 3 * j:g * GATE_ROWS + 3 * j + 1, :] * oct_ref[rows, :]

    krow = lax.broadcasted_iota(jnp.int32, (ta, 1), 0)
    qcol = lax.broadcasted_iota(jnp.int32, (1, ta), 1)
    lanes4 = lambda b: jnp.concatenate([b] * nh, axis=1)

    def qk(g, br, kt):
        k = k_ref[br, pl.ds(pl.multiple_of(kt * ta, ta), ta), g * LANES:(g + 1) * LANES]
        return jnp.dot(k, qa_ref[g], preferred_element_type=F32)

    def pv(g, br, kt, p):
        v = v_ref[br, g, :, pl.ds(pl.multiple_of(kt * ta, ta), ta)]
        return jnp.dot(v, p.astype(BF16), preferred_element_type=F32)

    far = WINDOW // ta
    causal = lanes4(jnp.where(qcol >= krow, 0.0, NEG_INF))
    kt_far = jnp.maximum(qi - far, 0)
    beyond = lanes4(jnp.where((qcol < krow) & (qi >= far), 0.0, NEG_INF))
    for g in groups:
        s_sel = qk(g, 0, qi) + causal
        m_sel = jnp.max(s_sel, axis=0, keepdims=True)
        p_sel = jnp.exp(s_sel - m_sel)
        m_ref[g, 0] = m_sel
        l_ref[g, 0] = jnp.sum(p_sel, axis=0, keepdims=True)
        acc_ref[g, 0] = pv(g, 0, qi, p_sel)
        s_win = qk(g, 1, qi) + causal
        s_far = qk(g, 1, kt_far) + beyond
        m_win = jnp.maximum(jnp.max(s_win, axis=0, keepdims=True), jnp.max(s_far, axis=0, keepdims=True))
        p_win = jnp.exp(s_win - m_win)
        p_far = jnp.exp(s_far - m_win)
        m_ref[g, 1] = m_win
        l_ref[g, 1] = jnp.sum(p_win, axis=0, keepdims=True) + jnp.sum(p_far, axis=0, keepdims=True)
        acc_ref[g, 1] = pv(g, 1, qi, p_win) + pv(g, 1, kt_far, p_far)

    n_tasks = qi + 1

    def task(n):
        n = jnp.minimum(n, n_tasks - 1)
        return jnp.where(n == 0, 1, 0), qi - jnp.maximum(n, 1)

    def scores(n, slot):
        br, kt = task(n)
        for g in groups:
            s = qk(g, br, kt)
            s_ref[g, slot] = s
            c_ref[g, slot] = jnp.max(s, axis=0, keepdims=True)

    def softmax(n, slot):
        br, _ = task(n)
        for c in range(nh * ta // SOFTMAX_COLS):
            cols = slice(c * SOFTMAX_COLS, (c + 1) * SOFTMAX_COLS)
            for g in groups:
                m_old = m_ref[g, br, :, cols]
                m_new = jnp.maximum(m_old, c_ref[g, slot, :, cols])
                alpha = jnp.exp(m_old - m_new)
                p = jnp.exp(s_ref[g, slot, :, cols] - m_new)
                l_ref[g, br, :, cols] = alpha * l_ref[g, br, :, cols] + jnp.sum(p, axis=0, keepdims=True)
                m_ref[g, br, :, cols] = m_new
                a_ref[g, slot, :, cols] = alpha
                p_ref[g, slot, :, cols] = p.astype(BF16)

    def values(n, slot):
        br, kt = task(n)
        for g in groups:
            acc_ref[g, br] = a_ref[g, slot] * acc_ref[g, br] + pv(g, br, kt, p_ref[g, slot])

    def stage(n, slot):
        scores(n + 2, slot)
        softmax(n + 1, 1 - slot)
        values(n, slot)

    @pl.when(qi >= 1)
    def _():
        scores(0, 0)
        scores(1, 1)
        softmax(0, 0)
        n_full = n_tasks - 1

        def pair(i, carry):
            stage(2 * i, 0)
            stage(2 * i + 1, 1)
            return carry

        lax.fori_loop(0, n_full // 2, pair, 0)

        @pl.when(n_full % 2 == 1)
        def _():
            stage(n_full - 1, 0)
            values(n_full, 1)

        @pl.when(n_full % 2 == 0)
        def _():
            values(n_full, 0)

    for g in groups:
        out = (out_ref[g] + gate_row(g, 1) * (acc_ref[g, 0] / l_ref[g, 0])
               + gate_row(g, 2) * (acc_ref[g, 1] / l_ref[g, 1]))
        for j in range(nh):
            rows = slice((g * nh + j) * dh, (g * nh + j + 1) * dh)
            o_ref[rows, :] = out[:, j * ta:(j + 1) * ta].astype(BF16)


def _attention(qt, kall, vt, selb, oct, gnt, batch, seq):
    ta = ATTN_TILE
    assert WINDOW % ta == 0
    nqt = seq // ta
    ng, nh, dh = N_KV_GROUPS, HEADS_PER_GROUP, HEAD_DIM
    nsel = selb.shape[2]
    qtile = lambda rows: pl.BlockSpec((rows, ta), lambda b, t: (0, b * nqt + t))
    vall = vt.reshape(2, ng, dh, batch * seq)
    wide = nh * ta
    return pl.pallas_call(
        _attn_kernel,
        grid=(batch, nqt),
        in_specs=[qtile(ATTN_WIDTH),
                  pl.BlockSpec((2, seq, ng * LANES), lambda b, t: (0, b, 0)),
                  pl.BlockSpec((2, ng, dh, seq), lambda b, t: (0, 0, 0, b)),
                  pl.BlockSpec((1, ng, nsel, ta), lambda b, t: (b, 0, 0, t)),
                  qtile(ATTN_WIDTH), qtile(ng * GATE_ROWS)],
        out_specs=qtile(ATTN_WIDTH),
        out_shape=jax.ShapeDtypeStruct((ATTN_WIDTH, batch * seq), BF16),
        scratch_shapes=[pltpu.VMEM((ng, LANES, wide), BF16),
                        pltpu.VMEM((ng, 2, ta, wide), F32),
                        pltpu.VMEM((ng, 2, ta, wide), BF16),
                        pltpu.VMEM((ng, 2, 1, wide), F32),
                        pltpu.VMEM((ng, 2, 1, wide), F32),
                        pltpu.VMEM((ng, 2, 1, wide), F32),
                        pltpu.VMEM((ng, 2, 1, wide), F32),
                        pltpu.VMEM((ng, 2, dh, wide), F32),
                        pltpu.VMEM((ng, dh, wide), F32)],
        compiler_params=_params(("parallel", "arbitrary")),
        name="nsa_flash",
    )(qt, kall, vall, selb, oct, gnt)


def _glu(u):
    u = u.astype(F32)
    return u[:, :CONV_CH] * _sigmoid(u[:, CONV_CH:])


def _conformer_kernel(cur_ref, prev_ref, dww_ref, dwb_ref, lng_ref, lnb_ref, wp_ref, y_ref, z_ref):
    ts, halo = CONV_TILE, CONV_HALO
    first = pl.program_id(1) == 0
    zp = _glu(prev_ref[0, ts - halo:, :])
    z_ref[0:halo, :] = jnp.where(first, 0.0, zp)
    z_ref[halo:, :] = _glu(cur_ref[0])
    base = halo - (CONV_WIDTH - 1)
    sub = 8
    acc = jnp.zeros((ts, CONV_CH), F32) + dwb_ref[...]
    for r in range(sub):
        part = None
        ext = ts + (sub if r else 0)
        for s in range(r, base + CONV_WIDTH, sub):
            if s < base:
                continue
            term = dww_ref[s - base:s - base + 1, :] * z_ref[s - r:s - r + ext, :]
            part = term if part is None else part + term
        acc = acc + part[r:r + ts, :]
    mu = jnp.mean(acc, axis=-1, keepdims=True)
    cen = acc - mu
    var = jnp.mean(cen * cen, axis=-1, keepdims=True)
    y = cen * lax.rsqrt(var + NORM_EPS) * lng_ref[...] + lnb_ref[...]
    y = y * _sigmoid(y)
    y_ref[0] = jnp.dot(y.astype(BF16), wp_ref[...], preferred_element_type=F32).astype(BF16)


def _conformer(u3, dww, dwb, lng, lnb, wp):
    b, seq, w = u3.shape
    ts = CONV_TILE
    full = lambda a: pl.BlockSpec(a.shape, lambda i, t: (0,) * a.ndim)
    return pl.pallas_call(
        _conformer_kernel,
        grid=(b, seq // ts),
        in_specs=[pl.BlockSpec((1, ts, w), lambda i, t: (i, t, 0)),
                  pl.BlockSpec((1, ts, w), lambda i, t: (i, jnp.maximum(t - 1, 0), 0)),
                  full(dww), full(dwb), full(lng), full(lnb), full(wp)],
        out_specs=pl.BlockSpec((1, ts, D_MODEL), lambda i, t: (i, t, 0)),
        out_shape=jax.ShapeDtypeStruct((b, seq, D_MODEL), BF16),
        scratch_shapes=[pltpu.VMEM((ts + CONV_HALO, CONV_CH), F32)],
        compiler_params=_params(("parallel", "arbitrary")),
        name="conformer_conv",
    )(u3, u3, dww, dwb, lng, lnb, wp)


def _merge_kernel(ot_ref, yc_ref, gm_ref, x_ref, wa_ref, wo_ref, out_ref):
    ya = _dot_tn(ot_ref[...], wa_ref[...])
    gm = gm_ref[...].astype(F32)
    mix = _sigmoid(gm[:, :D_MODEL]) * ya + _sigmoid(gm[:, D_MODEL:]) * yc_ref[...].astype(F32)
    out_ref[...] = x_ref[...] + jnp.dot(mix.astype(BF16), wo_ref[...], preferred_element_type=F32)


def _merge(ot, yc2, gm2, x2, wa, wo):
    t = x2.shape[0]
    tm = ROW_TILE
    row = lambda n: pl.BlockSpec((tm, n), lambda i: (i, 0))
    full = lambda a: pl.BlockSpec(a.shape, lambda i: (0, 0))
    return pl.pallas_call(
        _merge_kernel,
        grid=(t // tm,),
        in_specs=[pl.BlockSpec((ATTN_WIDTH, tm), lambda i: (0, i)), row(D_MODEL), row(2 * D_MODEL), row(D_MODEL),
                  full(wa), full(wo)],
        out_specs=row(D_MODEL),
        out_shape=jax.ShapeDtypeStruct((t, D_MODEL), F32),
        compiler_params=_params(("parallel",)),
        name="merge_outproj",
    )(ot, yc2, gm2, x2, wa, wo)


def _ffn_kernel(x_ref, xh_ref, g_ref, wa_ref, wv_ref, cwa_ref, cwv_ref, cba_ref, cbv_ref, wd_ref, *rest,
                tiles_per_seq, final_norm):
    if final_norm:
        fg_ref, out_ref, h_ref, a_ref, v_ref, gt_ref = rest
    else:
        out_ref, h_ref, a_ref, v_ref, gt_ref = rest
    tm, halo = FFN_ROW_TILE, FFN_HALO
    nj, _, tf = wa_ref.shape
    seq_start = pl.program_id(0) % tiles_per_seq == 0
    hh = _rms(xh_ref[...], g_ref[...])
    h_ref[0:halo, :] = jnp.where(seq_start, 0.0, hh).astype(BF16)
    h_ref[halo:, :] = _rms(x_ref[...], g_ref[...]).astype(BF16)
    h = h_ref[...]

    def conv(ref, w, b):
        out = b
        for k in range(FFN_CONV_WIDTH):
            off = halo - (FFN_CONV_WIDTH - 1) + k
            out = out + w[k:k + 1, :] * ref[off:off + tm, :]
        return out

    for j in range(nj):
        slot = j % 2
        a_ref[slot] = jnp.dot(h, wa_ref[j], preferred_element_type=F32)
        v_ref[slot] = jnp.dot(h, wv_ref[j], preferred_element_type=F32)
        ca = conv(a_ref.at[slot], cwa_ref[j], cba_ref[j])
        cv = conv(v_ref.at[slot], cwv_ref[j], cbv_ref[j])
        gt_ref[:, j * tf:(j + 1) * tf] = (ca * _sigmoid(ca) * cv).astype(BF16)
    y = x_ref[...] + jnp.dot(gt_ref[...], wd_ref[...], preferred_element_type=F32)
    out_ref[...] = _rms(y, fg_ref[...]) if final_norm else y


def _ffn(x2, g, wa, wv, cwa, cwv, cba, cbv, wd, seq, final_g=None):
    t = x2.shape[0]
    tm, halo = FFN_ROW_TILE, FFN_HALO
    nj, _, tf = wa.shape
    once = lambda a: pl.BlockSpec(a.shape, lambda i: (0,) * a.ndim, pipeline_mode=pl.Buffered(1))
    args = [x2, x2, g, wa, wv, cwa, cwv, cba, cbv, wd] + ([final_g] if final_g is not None else [])
    return pl.pallas_call(
        functools.partial(_ffn_kernel, tiles_per_seq=seq // tm, final_norm=final_g is not None),
        grid=(t // tm,),
        in_specs=[pl.BlockSpec((tm, D_MODEL), lambda i: (i, 0)),
                  pl.BlockSpec((halo, D_MODEL), lambda i: (jnp.maximum(i * (tm // halo) - 1, 0), 0))]
                 + [once(a) for a in args[2:]],
        out_specs=pl.BlockSpec((tm, D_MODEL), lambda i: (i, 0)),
        out_shape=jax.ShapeDtypeStruct((t, D_MODEL), F32),
        scratch_shapes=[pltpu.VMEM((tm + halo, D_MODEL), BF16), pltpu.VMEM((2, tm + halo, tf), F32),
                        pltpu.VMEM((2, tm + halo, tf), F32), pltpu.VMEM((tm, nj * tf), BF16)],
        compiler_params=_params(("parallel",)),
        name="conv_ffn",
    )(*args)


def _sel_map_t(seq):
    ncmp = (seq - CMP_BLOCK) // CMP_STRIDE + 1
    nr = seq // CMP_STRIDE
    nsel = seq // SEL_BLOCK
    cs = np.arange(ncmp) * CMP_STRIDE
    ce = cs + CMP_BLOCK - 1
    ss = np.arange(nsel) * SEL_BLOCK
    se = ss + SEL_BLOCK - 1
    ov = np.minimum(ce[:, None], se[None, :]) - np.maximum(cs[:, None], ss[None, :]) + 1
    m = np.zeros((nsel, nr), np.float32)
    m[:, :ncmp] = (np.clip(ov, 0, None).astype(np.float32) / CMP_BLOCK).T
    return jnp.asarray(m)


def _key_aug(seq):
    pos = np.arange(seq)
    nsel = seq // SEL_BLOCK
    a = np.zeros((seq, AUG), np.float32)
    a[:, AUG_POS + 0] = pos // SEL_BLOCK
    a[:, AUG_POS + 1] = pos % SEL_BLOCK
    a[:, AUG_POS + 2] = 1.0
    a[:, AUG_POS + 3] = 1.0
    s = a.copy()
    s[pos, pos // SEL_BLOCK] = 1.0
    assert nsel <= AUG_POS
    z = np.zeros((seq, HEAD_DIM), np.float32)
    return jnp.asarray(np.concatenate([z, s, z, s, z, a, z, a], axis=1))


def _layer(x2, batch, seq, p, final_g):
    g_, dh = N_KV_GROUPS, HEAD_DIM
    kall, kcv, uc2, gm2, qt, vt, gnt = _inproj(x2, p["norm1_g"], p["wn"], p["wt"], _key_aug(seq), seq)
    rows = seq // CMP_STRIDE
    kcv = kcv.reshape(batch, seq, 2, g_, dh).transpose(2, 0, 3, 1, 4).reshape(2, batch, g_, rows, CMP_STRIDE * dh)
    oct, selb = _cmp_branch(kcv[0], kcv[1], p["pek"], p["pev"], p["w1k"], p["w2k"], p["w1v"], p["w2v"], qt,
                            _sel_map_t(seq), seq)
    ot = _attention(qt, kall, vt, selb, oct, gnt, batch, seq)
    yc = _conformer(uc2.reshape(batch, seq, 2 * CONV_CH), p["dww"], p["dwb"], p["lng"], p["lnb"], p["wconv"])
    x2 = _merge(ot, yc.reshape(batch * seq, D_MODEL), gm2, x2, p["wattn"], p["wout"])
    return _ffn(x2, p["norm2_g"], p["wa"], p["wv"], p["cwa"], p["cwv"], p["cba"], p["cbv"], p["wd"], seq, final_g)


def _prep_layer(l, norm1_g, w_in, cmp_pe_k, cmp_pe_v, cmp_k_w1, cmp_k_w2, cmp_v_w1, cmp_v_w2, w_attn_br,
                conv_dw_w, conv_dw_b, conv_ln_g, conv_ln_b, w_conv_br, w_out, norm2_g, ffn_w_up, ffn_dw_w,
                ffn_dw_b, ffn_w_down):
    w = w_in[l].astype(BF16)
    kvw = KV_WIDTH
    c_q = ATTN_WIDTH
    c_kc, c_vc, c_ks, c_vs, c_kw, c_vw = (c_q + i * kvw for i in range(6))
    c_gn = c_q + 6 * kvw
    c_uc = c_gn + N_GATES
    c_gm = c_uc + 2 * CONV_CH

    def widen(cols):
        k = cols.reshape(D_MODEL, N_KV_GROUPS, HEAD_DIM)
        return jnp.pad(k, ((0, 0), (0, 0), (0, AUG))).reshape(D_MODEL, N_KV_GROUPS * LANES)

    wn = jnp.concatenate([widen(w[:, c_ks:c_ks + kvw]), widen(w[:, c_kw:c_kw + kvw]),
                          w[:, c_kc:c_kc + 2 * kvw], w[:, c_uc:c_gm], w[:, c_gm:]], axis=1)
    gates = w[:, c_gn:c_uc].reshape(D_MODEL, N_KV_GROUPS, 3 * HEADS_PER_GROUP)
    gates = jnp.pad(gates, ((0, 0), (0, 0), (0, GATE_ROWS - 3 * HEADS_PER_GROUP)))
    wt = jnp.concatenate([w[:, :c_q], w[:, c_vs:c_vs + kvw], w[:, c_vw:c_vw + kvw],
                          gates.reshape(D_MODEL, N_KV_GROUPS * GATE_ROWS)], axis=1).T
    half = CMP_BLOCK // 2 * HEAD_DIM
    tf = FFN_COL_TILE
    nj = FFN_DIM // tf
    cols = lambda a: a.reshape(a.shape[0], nj, tf).transpose(1, 0, 2)
    up = ffn_w_up[l].astype(BF16)
    dw = ffn_dw_w[l]
    db = ffn_dw_b[l][None, :]
    return dict(
        norm1_g=norm1_g[l][None, :], wn=wn, wt=wt,
        pek=cmp_pe_k[l].reshape(2, half), pev=cmp_pe_v[l].reshape(2, half),
        w1k=cmp_k_w1[l].astype(BF16).reshape(2, half, CMP_HIDDEN), w2k=cmp_k_w2[l].astype(BF16),
        w1v=cmp_v_w1[l].astype(BF16).reshape(2, half, CMP_HIDDEN), w2v=cmp_v_w2[l].astype(BF16),
        wattn=w_attn_br[l].astype(BF16),
        dww=conv_dw_w[l], dwb=conv_dw_b[l][None, :], lng=conv_ln_g[l][None, :], lnb=conv_ln_b[l][None, :],
        wconv=w_conv_br[l].astype(BF16), wout=w_out[l].astype(BF16),
        norm2_g=norm2_g[l][None, :],
        wa=cols(up[:, :FFN_DIM]), wv=cols(up[:, FFN_DIM:]),
        cwa=cols(dw[:, :FFN_DIM]), cwv=cols(dw[:, FFN_DIM:]),
        cba=cols(db[:, :FFN_DIM]), cbv=cols(db[:, FFN_DIM:]),
        wd=ffn_w_down[l].astype(BF16),
    )


def kernel(x, norm1_g, w_in, cmp_pe_k, cmp_pe_v, cmp_k_w1, cmp_k_w2, cmp_v_w1, cmp_v_w2, w_attn_br, conv_dw_w, conv_dw_b, conv_ln_g, conv_ln_b, w_conv_br, w_out, norm2_g, ffn_w_up, ffn_dw_w, ffn_dw_b, ffn_w_down, final_g):
    batch, seq, d = x.shape
    assert d == D_MODEL and seq % ROW_TILE == 0 and seq % FFN_ROW_TILE == 0 and seq % CMP_Q_TILE == 0
    assert seq // SEL_BLOCK == AUG_POS
    x2 = x.reshape(batch * seq, d)
    for l in range(w_in.shape[0]):
        p = _prep_layer(l, norm1_g, w_in, cmp_pe_k, cmp_pe_v, cmp_k_w1, cmp_k_w2, cmp_v_w1, cmp_v_w2, w_attn_br,
                        conv_dw_w, conv_dw_b, conv_ln_g, conv_ln_b, w_conv_br, w_out, norm2_g, ffn_w_up,
                        ffn_dw_w, ffn_dw_b, ffn_w_down)
        last = l == w_in.shape[0] - 1
        x2 = _layer(x2, batch, seq, p, final_g[None, :] if last else None)
    return x2.reshape(batch, seq, d)
```

```python
import functools
import math

import numpy as np
import jax
import jax.numpy as jnp
from jax import lax
from jax.experimental import pallas as pl
from jax.experimental.pallas import tpu as pltpu

F32 = jnp.float32
BF16 = jnp.bfloat16

D_MODEL = 1024
N_HEADS = 8
HEAD_DIM = 64
N_KV_GROUPS = 2
HEADS_PER_GROUP = N_HEADS // N_KV_GROUPS
ATTN_WIDTH = N_HEADS * HEAD_DIM
KV_WIDTH = N_KV_GROUPS * HEAD_DIM
CMP_BLOCK = 32
CMP_STRIDE = 16
CMP_HIDDEN = 128
SEL_BLOCK = 64
SEL_TOP = 16
SEL_SHIFT = 6
WINDOW = 512
FORCE_BONUS = 1.0e4
NEG_INF = -1.0e30
CONV_CH = 512
CONV_WIDTH = 31
FFN_DIM = 2816
FFN_CONV_WIDTH = 3
NORM_EPS = 1e-6
N_GATES = 3 * N_HEADS
ATTN_SCALE = HEAD_DIM ** -0.5
UNSELECTED = -(2.0 ** 99)

VMEM_LIMIT_BYTES = 56 * 1024 * 1024
LANES = 128

ROW_TILE = 512
FFN_ROW_TILE = 512
ATTN_TILE = 256
SOFTMAX_COLS = 128
CMP_Q_TILE = 512
CONV_TILE = 256
CONV_HALO = 32
FFN_COL_TILE = 256
FFN_HALO = 8

AUG = LANES - HEAD_DIM
AUG_POS = N_HEADS * 4
GATE_ROWS = 16
Q_ROWS = HEADS_PER_GROUP * HEAD_DIM

assert AUG_POS == 2048 // SEL_BLOCK and AUG_POS + 4 <= AUG


def _params(sem):
    return pltpu.CompilerParams(dimension_semantics=sem, vmem_limit_bytes=VMEM_LIMIT_BYTES)


def _rms(x, g):
    y = x * lax.rsqrt(jnp.mean(x * x, axis=-1, keepdims=True) + NORM_EPS)
    return y * g


def _sigmoid(x):
    return 1.0 / (1.0 + jnp.exp(-x))


def _dot_nt(a, b, **kw):
    return lax.dot_general(a, b, (((1,), (1,)), ((), ())), preferred_element_type=F32, **kw)


def _dot_tn(a, b):
    return lax.dot_general(a, b, (((0,), (0,)), ((), ())), preferred_element_type=F32)


def _inproj_kernel(x_ref, g_ref, wn_ref, wt_ref, aug_ref,
                   kall_ref, kc_ref, vc_ref, uc_ref, gm_ref, qt_ref, vt_ref, gnt_ref):
    h = _rms(x_ref[...], g_ref[...]).astype(BF16)
    c = 0
    for ref, extra in ((kall_ref.at[0], aug_ref.at[:, 0:2 * LANES]), (kall_ref.at[1], aug_ref.at[:, 2 * LANES:4 * LANES]),
                       (kc_ref, None), (vc_ref, None), (uc_ref, None), (gm_ref, None)):
        n = ref.shape[1]
        y = jnp.dot(h, wn_ref[:, c:c + n], preferred_element_type=F32)
        if extra is not None:
            y = y + extra[...]
        ref[...] = y.astype(ref.dtype)
        c += n
    t = _dot_nt(wt_ref[...], h)
    nq, nv = qt_ref.shape[0], vt_ref.shape[0]
    qt_ref[...] = (t[0:nq] * ATTN_SCALE).astype(BF16)
    vt_ref[...] = t[nq:nq + nv].astype(BF16)
    gnt_ref[...] = t[nq + nv:]


def _inproj(x2, g, wn, wt, aug, seq):
    t = x2.shape[0]
    tm = ROW_TILE
    row = lambda n: pl.BlockSpec((tm, n), lambda i: (i, 0))
    col = lambda n: pl.BlockSpec((n, tm), lambda i: (0, i))
    full = lambda a: pl.BlockSpec(a.shape, lambda i: (0, 0))
    kw = N_KV_GROUPS * LANES
    nat = ((KV_WIDTH, F32), (KV_WIDTH, F32), (2 * CONV_CH, BF16), (2 * D_MODEL, BF16))
    trn = ((ATTN_WIDTH, BF16), (2 * KV_WIDTH, BF16), (N_KV_GROUPS * GATE_ROWS, F32))
    return pl.pallas_call(
        _inproj_kernel,
        grid=(t // tm,),
        in_specs=[row(D_MODEL), full(g), full(wn), full(wt),
                  pl.BlockSpec((tm, 2 * kw), lambda i: (i % (seq // tm), 0))],
        out_specs=[pl.BlockSpec((2, tm, kw), lambda i: (0, i, 0))] + [row(n) for n, _ in nat]
                  + [col(n) for n, _ in trn],
        out_shape=[jax.ShapeDtypeStruct((2, t, kw), BF16)] + [jax.ShapeDtypeStruct((t, n), d) for n, d in nat]
                  + [jax.ShapeDtypeStruct((n, t), d) for n, d in trn],
        compiler_params=_params(("parallel",)),
        name="inproj",
    )(x2, g, wn, wt, aug)


def _gelu_tanh(x):
    return 0.5 * x * (1.0 + jnp.tanh(math.sqrt(2.0 / math.pi) * (x + 0.044715 * (x * x * x))))


def _compress(x_ref, pe_ref, w1_ref, w2_ref, nr):
    half = CMP_BLOCK // 2
    a = b = None
    for l in range(half):
        x = x_ref[pl.ds(l, nr, stride=CMP_STRIDE), :]
        ta = jnp.dot((x + pe_ref[l:l + 1, :]).astype(BF16), w1_ref[l], preferred_element_type=F32)
        tb = jnp.dot((x + pe_ref[half + l:half + l + 1, :]).astype(BF16), w1_ref[half + l],
                     preferred_element_type=F32)
        a = ta if a is None else a + ta
        b = tb if b is None else b + tb
    pre = a + pltpu.roll(b, nr - 1, 0)
    hid = _gelu_tanh(pre).astype(BF16)
    return jnp.dot(hid, w2_ref[...], preferred_element_type=F32)


def _cmp_kernel(kc_ref, vc_ref, pek_ref, pev_ref, w1k_ref, w2k_ref, w1v_ref, w2v_ref,
                qt_ref, mapt_ref, oct_ref, selb_ref, *, seq):
    ng, nh, dh = N_KV_GROUPS, HEADS_PER_GROUP, HEAD_DIM
    nr = seq // CMP_STRIDE
    kc = _compress(kc_ref, pek_ref, w1k_ref, w2k_ref, nr).astype(BF16)
    vct = jnp.transpose(_compress(vc_ref, pev_ref, w1v_ref, w2v_ref, nr)).astype(BF16)
    nsel = seq // SEL_BLOCK
    tq = CMP_Q_TILE
    c_end = lax.broadcasted_iota(jnp.int32, (nr, 1), 0) * CMP_STRIDE + (CMP_BLOCK - 1)
    blk = lax.broadcasted_iota(jnp.int32, (nsel, 1), 0)
    zeros_q = jnp.zeros((dh, tq), BF16)

    def tile(ti, carry):
        t0 = pl.multiple_of(ti * tq, tq)
        pos = t0 + lax.broadcasted_iota(jnp.int32, (1, tq), 1)
        dist = pos - c_end
        valid = dist >= 0
        distf = dist.astype(F32)
        cur = pos >> SEL_SHIFT
        forced = (blk == 0) | (blk == cur) | (blk == cur - 1)
        causal = blk <= cur
        for g in range(ng):
            psum = jnp.zeros((nr, tq), F32)
            for j in range(nh):
                slope = 2.0 ** -(g * nh + j + 1)
                rows = pl.ds((g * nh + j) * dh, dh)
                q = qt_ref[rows, pl.ds(t0, tq)]
                q2 = jnp.concatenate([q, zeros_q] if g == 0 else [zeros_q, q], axis=0)
                s = jnp.dot(kc, q2, preferred_element_type=F32)
                s = jnp.where(valid, s - slope * distf, NEG_INF)
                m = jnp.max(s, axis=0, keepdims=True)
                e = jnp.where(valid, jnp.exp(s - m), 0.0)
                l = jnp.sum(e, axis=0, keepdims=True)
                p = e / jnp.where(l > 0.0, l, 1.0)
                psum = psum + p
                oct_ref[rows, pl.ds(t0, tq)] = jnp.dot(vct[g * dh:(g + 1) * dh, :], p.astype(BF16),
                                                       preferred_element_type=F32)
            imp = jnp.dot(mapt_ref[...], psum, precision=lax.Precision.HIGHEST, preferred_element_type=F32)
            val = jnp.where(causal, imp + jnp.where(forced, FORCE_BONUS, 0.0), NEG_INF)
            cnt = jnp.zeros((nsel, tq), F32)
            for i in range(nsel):
                vi = val[i:i + 1, :]
                tie = (blk > i).astype(F32)
                cnt = cnt + jnp.where(vi > val, 1.0, jnp.where(vi == val, tie, 0.0))
            n_top = min(SEL_TOP, nsel)
            selb_ref[0, g, :, pl.ds(t0, tq)] = jnp.where(causal & (cnt < n_top), 0.0, UNSELECTED).astype(BF16)
        return carry

    lax.fori_loop(0, seq // tq, tile, 0)


def _cmp_branch(kc, vc, pek, pev, w1k, w2k, w1v, w2v, qt, mapt, batch, seq):
    nsel = seq // SEL_BLOCK
    full = lambda a: pl.BlockSpec(a.shape, lambda i: (0,) * a.ndim)
    rows = pl.BlockSpec((seq, KV_WIDTH), lambda i: (i, 0))
    heads = pl.BlockSpec((ATTN_WIDTH, seq), lambda i: (0, i))
    return pl.pallas_call(
        functools.partial(_cmp_kernel, seq=seq),
        grid=(batch,),
        in_specs=[rows, rows, full(pek), full(pev), full(w1k), full(w2k), full(w1v), full(w2v), heads, full(mapt)],
        out_specs=[heads, pl.BlockSpec((1, N_KV_GROUPS, nsel, seq), lambda i: (i, 0, 0, 0))],
        out_shape=[jax.ShapeDtypeStruct((ATTN_WIDTH, batch * seq), F32),
                   jax.ShapeDtypeStruct((batch, N_KV_GROUPS, nsel, seq), BF16)],
        compiler_params=_params(("parallel",)),
        name="cmp_topk",
    )(kc, vc, pek, pev, w1k, w2k, w1v, w2v, qt, mapt)


def _attn_kernel(qt_ref, k_ref, v_ref, selb_ref, oct_ref, gnt_ref, o_ref,
                 qa_ref, s_ref, p_ref, a_ref, c_ref, m_ref, l_ref, acc_ref, out_ref):
    qi = pl.program_id(1)
    ta = ATTN_TILE
    ng, nh, dh = N_KV_GROUPS, HEADS_PER_GROUP, HEAD_DIM
    nsel = selb_ref.shape[2]
    groups = range(ng)
    pos = qi * ta + lax.broadcasted_iota(jnp.int32, (1, ta), 1)
    tb = (pos >> SEL_SHIFT).astype(F32)
    tr = (pos & (SEL_BLOCK - 1)).astype(F32)
    arow = lax.broadcasted_iota(jnp.int32, (16, 1), 0)
    gate = _sigmoid(gnt_ref[...])

    def gate_row(g, c):
        r = g * GATE_ROWS + c
        return jnp.concatenate([gate[r + 3 * j:r + 3 * j + 1, :] for j in range(nh)], axis=1)

    for g in groups:
        for j in range(nh):
            slope = 2.0 ** -(g * nh + j + 1)
            cols = slice(j * ta, (j + 1) * ta)
            rows = slice((g * nh + j) * dh, (g * nh + j + 1) * dh)
            qa_ref[g, 0:dh, cols] = qt_ref[rows, :]
            qa_ref[g, dh:dh + nsel, cols] = selb_ref[0, g]
            alibi = jnp.where(arow == 0, slope * SEL_BLOCK,
                              jnp.where(arow == 1, slope,
                                        jnp.where(arow == 2, -slope * SEL_BLOCK * tb,
                                                  jnp.where(arow == 3, -slope * tr, 0.0))))
            qa_ref[g, dh + nsel:dh + nsel + 16, cols] = alibi.astype(BF16)
            qa_ref[g, dh + nsel + 16:, cols] = jnp.zeros((LANES - dh - nsel - 16, ta), BF16)
            out_ref[g, :, cols] = gate[g * GATE_ROWS + 3 * j:g * GATE_ROWS + 3 * j + 1, :] * oct_ref[rows, :]

    krow = lax.broadcasted_iota(jnp.int32, (ta, 1), 0)
    qcol = lax.broadcasted_iota(jnp.int32, (1, ta), 1)
    lanes4 = lambda b: jnp.concatenate([b] * nh, axis=1)

    def qk(g, br, kt):
        k = k_ref[br, pl.ds(pl.multiple_of(kt * ta, ta), ta), g * LANES:(g + 1) * LANES]
        return jnp.dot(k, qa_ref[g], preferred_element_type=F32)

    def pv(g, br, kt, p):
        v = v_ref[br, g, :, pl.ds(pl.multiple_of(kt * ta, ta), ta)]
        return jnp.dot(v, p.astype(BF16), preferred_element_type=F32)

    far = WINDOW // ta
    causal = lanes4(jnp.where(qcol >= krow, 0.0, NEG_INF))
    kt_far = jnp.maximum(qi - far, 0)
    beyond = lanes4(jnp.where((qcol < krow) & (qi >= far), 0.0, NEG_INF))
    for g in groups:
        s_sel = qk(g, 0, qi) + causal
        m_sel = jnp.max(s_sel, axis=0, keepdims=True)
        p_sel = jnp.exp(s_sel - m_sel)
        m_ref[g, 0] = m_sel
        l_ref[g, 0] = jnp.sum(p_sel, axis=0, keepdims=True)
        acc_ref[g, 0] = pv(g, 0, qi, p_sel)
        s_win = qk(g, 1, qi) + causal
        s_far = qk(g, 1, kt_far) + beyond
        m_win = jnp.maximum(jnp.max(s_win, axis=0, keepdims=True), jnp.max(s_far, axis=0, keepdims=True))
        p_win = jnp.exp(s_win - m_win)
        p_far = jnp.exp(s_far - m_win)
        m_ref[g, 1] = m_win
        l_ref[g, 1] = jnp.sum(p_win, axis=0, keepdims=True) + jnp.sum(p_far, axis=0, keepdims=True)
        acc_ref[g, 1] = pv(g, 1, qi, p_win) + pv(g, 1, kt_far, p_far)

    n_tasks = qi + 1

    def task(n):
        n = jnp.minimum(n, n_tasks - 1)
        return jnp.where(n == 0, 1, 0), qi - jnp.maximum(n, 1)

    def scores(n, slot):
        br, kt = task(n)
        for g in groups:
            s = qk(g, br, kt)
            s_ref[g, slot] = s
            c_ref[g, slot] = jnp.max(s, axis=0, keepdims=True)

    def softmax(n, slot):
        br, _ = task(n)
        for c in range(nh * ta // SOFTMAX_COLS):
            cols = slice(c * SOFTMAX_COLS, (c + 1) * SOFTMAX_COLS)
            for g in groups:
                m_old = m_ref[g, br, :, cols]
                m_new = jnp.maximum(m_old, c_ref[g, slot, :, cols])
                alpha = jnp.exp(m_old - m_new)
                p = jnp.exp(s_ref[g, slot, :, cols] - m_new)
                l_ref[g, br, :, cols] = alpha * l_ref[g, br, :, cols] + jnp.sum(p, axis=0, keepdims=True)
                m_ref[g, br, :, cols] = m_new
                a_ref[g, slot, :, cols] = alpha
                p_ref[g, slot, :, cols] = p.astype(BF16)

    def values(n, slot):
        br, kt = task(n)
        for g in groups:
            acc_ref[g, br] = a_ref[g, slot] * acc_ref[g, br] + pv(g, br, kt, p_ref[g, slot])

    def stage(n, slot):
        scores(n + 2, slot)
        softmax(n + 1, 1 - slot)
        values(n, slot)

    @pl.when(qi >= 1)
    def _():
        scores(0, 0)
        scores(1, 1)
        softmax(0, 0)
        n_full = n_tasks - 1

        def pair(i, carry):
            stage(2 * i, 0)
            stage(2 * i + 1, 1)
            return carry

        lax.fori_loop(0, n_full // 2, pair, 0)

        @pl.when(n_full % 2 == 1)
        def _():
            stage(n_full - 1, 0)
            values(n_full, 1)

        @pl.when(n_full % 2 == 0)
        def _():
            values(n_full, 0)

    for g in groups:
        out = (out_ref[g] + gate_row(g, 1) * (acc_ref[g, 0] / l_ref[g, 0])
               + gate_row(g, 2) * (acc_ref[g, 1] / l_ref[g, 1]))
        for j in range(nh):
            rows = slice((g * nh + j) * dh, (g * nh + j + 1) * dh)
            o_ref[rows, :] = out[:, j * ta:(j + 1) * ta].astype(BF16)


def _attention(qt, kall, vt, selb, oct, gnt, batch, seq):
    ta = ATTN_TILE
    assert WINDOW % ta == 0
    nqt = seq // ta
    ng, nh, dh = N_KV_GROUPS, HEADS_PER_GROUP, HEAD_DIM
    nsel = selb.shape[2]
    qtile = lambda rows: pl.BlockSpec((rows, ta), lambda b, t: (0, b * nqt + t))
    vall = vt.reshape(2, ng, dh, batch * seq)
    wide = nh * ta
    return pl.pallas_call(
        _attn_kernel,
        grid=(batch, nqt),
        in_specs=[qtile(ATTN_WIDTH),
                  pl.BlockSpec((2, seq, ng * LANES), lambda b, t: (0, b, 0)),
                  pl.BlockSpec((2, ng, dh, seq), lambda b, t: (0, 0, 0, b)),
                  pl.BlockSpec((1, ng, nsel, ta), lambda b, t: (b, 0, 0, t)),
                  qtile(ATTN_WIDTH), qtile(ng * GATE_ROWS)],
        out_specs=qtile(ATTN_WIDTH),
        out_shape=jax.ShapeDtypeStruct((ATTN_WIDTH, batch * seq), BF16),
        scratch_shapes=[pltpu.VMEM((ng, LANES, wide), BF16),
                        pltpu.VMEM((ng, 2, ta, wide), F32),
                        pltpu.VMEM((ng, 2, ta, wide), BF16),
                        pltpu.VMEM((ng, 2, 1, wide), F32),
                        pltpu.VMEM((ng, 2, 1, wide), F32),
                        pltpu.VMEM((ng, 2, 1, wide), F32),
                        pltpu.VMEM((ng, 2, 1, wide), F32),
                        pltpu.VMEM((ng, 2, dh, wide), F32),
                        pltpu.VMEM((ng, dh, wide), F32)],
        compiler_params=_params(("parallel", "arbitrary")),
        name="nsa_flash",
    )(qt, kall, vall, selb, oct, gnt)


def _glu(u):
    u = u.astype(F32)
    return u[:, :CONV_CH] * _sigmoid(u[:, CONV_CH:])


def _conformer_kernel(cur_ref, prev_ref, dww_ref, dwb_ref, lng_ref, lnb_ref, wp_ref, y_ref, z_ref):
    ts, halo = CONV_TILE, CONV_HALO
    first = pl.program_id(1) == 0
    zp = _glu(prev_ref[0, ts - halo:, :])
    z_ref[0:halo, :] = jnp.where(first, 0.0, zp)
    z_ref[halo:, :] = _glu(cur_ref[0])
    base = halo - (CONV_WIDTH - 1)
    sub = 8
    acc = jnp.zeros((ts, CONV_CH), F32) + dwb_ref[...]
    for r in range(sub):
        part = None
        ext = ts + (sub if r else 0)
        for s in range(r, base + CONV_WIDTH, sub):
            if s < base:
                continue
            term = dww_ref[s - base:s - base + 1, :] * z_ref[s - r:s - r + ext, :]
            part = term if part is None else part + term
        acc = acc + part[r:r + ts, :]
    mu = jnp.mean(acc, axis=-1, keepdims=True)
    cen = acc - mu
    var = jnp.mean(cen * cen, axis=-1, keepdims=True)
    y = cen * lax.rsqrt(var + NORM_EPS) * lng_ref[...] + lnb_ref[...]
    y = y * _sigmoid(y)
    y_ref[0] = jnp.dot(y.astype(BF16), wp_ref[...], preferred_element_type=F32).astype(BF16)


def _conformer(u3, dww, dwb, lng, lnb, wp):
    b, seq, w = u3.shape
    ts = CONV_TILE
    full = lambda a: pl.BlockSpec(a.shape, lambda i, t: (0,) * a.ndim)
    return pl.pallas_call(
        _conformer_kernel,
        grid=(b, seq // ts),
        in_specs=[pl.BlockSpec((1, ts, w), lambda i, t: (i, t, 0)),
                  pl.BlockSpec((1, ts, w), lambda i, t: (i, jnp.maximum(t - 1, 0), 0)),
                  full(dww), full(dwb), full(lng), full(lnb), full(wp)],
        out_specs=pl.BlockSpec((1, ts, D_MODEL), lambda i, t: (i, t, 0)),
        out_shape=jax.ShapeDtypeStruct((b, seq, D_MODEL), BF16),
        scratch_shapes=[pltpu.VMEM((ts + CONV_HALO, CONV_CH), F32)],
        compiler_params=_params(("parallel", "arbitrary")),
        name="conformer_conv",
    )(u3, u3, dww, dwb, lng, lnb, wp)


def _merge_kernel(ot_ref, yc_ref, gm_ref, x_ref, wa_ref, wo_ref, out_ref):
    ya = _dot_tn(ot_ref[...], wa_ref[...])
    gm = gm_ref[...].astype(F32)
    mix = _sigmoid(gm[:, :D_MODEL]) * ya + _sigmoid(gm[:, D_MODEL:]) * yc_ref[...].astype(F32)
    out_ref[...] = x_ref[...] + jnp.dot(mix.astype(BF16), wo_ref[...], preferred_element_type=F32)


def _merge(ot, yc2, gm2, x2, wa, wo):
    t = x2.shape[0]
    tm = ROW_TILE
    row = lambda n: pl.BlockSpec((tm, n), lambda i: (i, 0))
    full = lambda a: pl.BlockSpec(a.shape, lambda i: (0, 0))
    return pl.pallas_call(
        _merge_kernel,
        grid=(t // tm,),
        in_specs=[pl.BlockSpec((ATTN_WIDTH, tm), lambda i: (0, i)), row(D_MODEL), row(2 * D_MODEL), row(D_MODEL),
                  full(wa), full(wo)],
        out_specs=row(D_MODEL),
        out_shape=jax.ShapeDtypeStruct((t, D_MODEL), F32),
        compiler_params=_params(("parallel",)),
        name="merge_outproj",
    )(ot, yc2, gm2, x2, wa, wo)


def _ffn_kernel(x_ref, xh_ref, g_ref, wup_ref, cw_ref, cb_ref, wd_ref, *rest, tiles_per_seq, final_norm):
    if final_norm:
        fg_ref, out_ref, h_ref, a_ref, v_ref, gt_ref = rest
    else:
        out_ref, h_ref, a_ref, v_ref, gt_ref = rest
    tm, halo, tf = FFN_ROW_TILE, FFN_HALO, FFN_COL_TILE
    seq_start = pl.program_id(0) % tiles_per_seq == 0
    hh = _rms(xh_ref[...], g_ref[...])
    h_ref[0:halo, :] = jnp.where(seq_start, 0.0, hh).astype(BF16)
    h_ref[halo:, :] = _rms(x_ref[...], g_ref[...]).astype(BF16)
    h = h_ref[...]

    def conv(ref, w, b):
        out = b
        for k in range(FFN_CONV_WIDTH):
            off = halo - (FFN_CONV_WIDTH - 1) + k
            out = out + w[k:k + 1, :] * ref[off:off + tm, :]
        return out

    for j in range(FFN_DIM // tf):
        slot = j % 2
        ca_cols = slice(j * tf, (j + 1) * tf)
        cv_cols = slice(FFN_DIM + j * tf, FFN_DIM + (j + 1) * tf)
        a_ref[slot] = jnp.dot(h, wup_ref[:, ca_cols], preferred_element_type=F32)
        v_ref[slot] = jnp.dot(h, wup_ref[:, cv_cols], preferred_element_type=F32)
        ca = conv(a_ref.at[slot], cw_ref[:, ca_cols], cb_ref[:, ca_cols])
        cv = conv(v_ref.at[slot], cw_ref[:, cv_cols], cb_ref[:, cv_cols])
        gt_ref[:, ca_cols] = (ca * _sigmoid(ca) * cv).astype(BF16)
    y = x_ref[...] + jnp.dot(gt_ref[...], wd_ref[...], preferred_element_type=F32)
    out_ref[...] = _rms(y, fg_ref[...]) if final_norm else y


def _ffn(x2, g, wup, cw, cb, wd, seq, final_g=None):
    t = x2.shape[0]
    tm, halo, tf = FFN_ROW_TILE, FFN_HALO, FFN_COL_TILE
    assert FFN_DIM % tf == 0
    once = lambda a: pl.BlockSpec(a.shape, lambda i: (0,) * a.ndim, pipeline_mode=pl.Buffered(1))
    args = [x2, x2, g, wup, cw, cb, wd] + ([final_g] if final_g is not None else [])
    return pl.pallas_call(
        functools.partial(_ffn_kernel, tiles_per_seq=seq // tm, final_norm=final_g is not None),
        grid=(t // tm,),
        in_specs=[pl.BlockSpec((tm, D_MODEL), lambda i: (i, 0)),
                  pl.BlockSpec((halo, D_MODEL), lambda i: (jnp.maximum(i * (tm // halo) - 1, 0), 0))]
                 + [once(a) for a in args[2:]],
        out_specs=pl.BlockSpec((tm, D_MODEL), lambda i: (i, 0)),
        out_shape=jax.ShapeDtypeStruct((t, D_MODEL), F32),
        scratch_shapes=[pltpu.VMEM((tm + halo, D_MODEL), BF16), pltpu.VMEM((2, tm + halo, tf), F32),
                        pltpu.VMEM((2, tm + halo, tf), F32), pltpu.VMEM((tm, FFN_DIM), BF16)],
        compiler_params=_params(("parallel",)),
        name="conv_ffn",
    )(*args)


def _sel_map_t(seq):
    ncmp = (seq - CMP_BLOCK) // CMP_STRIDE + 1
    nr = seq // CMP_STRIDE
    nsel = seq // SEL_BLOCK
    cs = np.arange(ncmp) * CMP_STRIDE
    ce = cs + CMP_BLOCK - 1
    ss = np.arange(nsel) * SEL_BLOCK
    se = ss + SEL_BLOCK - 1
    ov = np.minimum(ce[:, None], se[None, :]) - np.maximum(cs[:, None], ss[None, :]) + 1
    m = np.zeros((nsel, nr), np.float32)
    m[:, :ncmp] = (np.clip(ov, 0, None).astype(np.float32) / CMP_BLOCK).T
    return jnp.asarray(m)


def _key_aug(seq):
    pos = np.arange(seq)
    nsel = seq // SEL_BLOCK
    a = np.zeros((seq, AUG), np.float32)
    a[:, AUG_POS + 0] = pos // SEL_BLOCK
    a[:, AUG_POS + 1] = pos % SEL_BLOCK
    a[:, AUG_POS + 2] = 1.0
    a[:, AUG_POS + 3] = 1.0
    s = a.copy()
    s[pos, pos // SEL_BLOCK] = 1.0
    assert nsel <= AUG_POS
    z = np.zeros((seq, HEAD_DIM), np.float32)
    return jnp.asarray(np.concatenate([z, s, z, s, z, a, z, a], axis=1))


def _layer(x2, batch, seq, p, final_g):
    kall, kc, vc, uc2, gm2, qt, vt, gnt = _inproj(x2, p["norm1_g"], p["wn"], p["wt"], _key_aug(seq), seq)
    oct, selb = _cmp_branch(kc, vc, p["pek"], p["pev"], p["w1k"], p["w2k"], p["w1v"], p["w2v"], qt,
                            _sel_map_t(seq), batch, seq)
    ot = _attention(qt, kall, vt, selb, oct, gnt, batch, seq)
    yc = _conformer(uc2.reshape(batch, seq, 2 * CONV_CH), p["dww"], p["dwb"], p["lng"], p["lnb"], p["wconv"])
    x2 = _merge(ot, yc.reshape(batch * seq, D_MODEL), gm2, x2, p["wattn"], p["wout"])
    return _ffn(x2, p["norm2_g"], p["wup"], p["cw"], p["cb"], p["wd"], seq, final_g)


def _prep_layer(l, norm1_g, w_in, cmp_pe_k, cmp_pe_v, cmp_k_w1, cmp_k_w2, cmp_v_w1, cmp_v_w2, w_attn_br,
                conv_dw_w, conv_dw_b, conv_ln_g, conv_ln_b, w_conv_br, w_out, norm2_g, ffn_w_up, ffn_dw_w,
                ffn_dw_b, ffn_w_down):
    w = w_in[l].astype(BF16)
    kvw = KV_WIDTH
    c_q = ATTN_WIDTH
    c_kc, c_vc, c_ks, c_vs, c_kw, c_vw = (c_q + i * kvw for i in range(6))
    c_gn = c_q + 6 * kvw
    c_uc = c_gn + N_GATES
    c_gm = c_uc + 2 * CONV_CH

    def widen(cols):
        k = cols.reshape(D_MODEL, N_KV_GROUPS, HEAD_DIM)
        return jnp.pad(k, ((0, 0), (0, 0), (0, AUG))).reshape(D_MODEL, N_KV_GROUPS * LANES)

    wn = jnp.concatenate([widen(w[:, c_ks:c_ks + kvw]), widen(w[:, c_kw:c_kw + kvw]),
                          w[:, c_kc:c_kc + 2 * kvw], w[:, c_uc:c_gm], w[:, c_gm:]], axis=1)
    gates = w[:, c_gn:c_uc].reshape(D_MODEL, N_KV_GROUPS, 3 * HEADS_PER_GROUP)
    gates = jnp.pad(gates, ((0, 0), (0, 0), (0, GATE_ROWS - 3 * HEADS_PER_GROUP)))
    wt = jnp.concatenate([w[:, :c_q], w[:, c_vs:c_vs + kvw], w[:, c_vw:c_vw + kvw],
                          gates.reshape(D_MODEL, N_KV_GROUPS * GATE_ROWS)], axis=1).T

    def per_group(a):
        z = jnp.zeros_like(a)
        return jnp.concatenate([jnp.concatenate([a, z], axis=-1), jnp.concatenate([z, a], axis=-1)], axis=-2)

    assert N_KV_GROUPS == 2
    w1 = lambda a: per_group(a.astype(BF16).reshape(CMP_BLOCK, HEAD_DIM, CMP_HIDDEN))
    pe = lambda a: jnp.concatenate([a] * N_KV_GROUPS, axis=1)
    return dict(
        norm1_g=norm1_g[l][None, :], wn=wn, wt=wt,
        pek=pe(cmp_pe_k[l]), pev=pe(cmp_pe_v[l]),
        w1k=w1(cmp_k_w1[l]), w2k=per_group(cmp_k_w2[l].astype(BF16)),
        w1v=w1(cmp_v_w1[l]), w2v=per_group(cmp_v_w2[l].astype(BF16)),
        wattn=w_attn_br[l].astype(BF16),
        dww=conv_dw_w[l], dwb=conv_dw_b[l][None, :], lng=conv_ln_g[l][None, :], lnb=conv_ln_b[l][None, :],
        wconv=w_conv_br[l].astype(BF16), wout=w_out[l].astype(BF16),
        norm2_g=norm2_g[l][None, :],
        wup=ffn_w_up[l].astype(BF16), cw=ffn_dw_w[l], cb=ffn_dw_b[l][None, :],
        wd=ffn_w_down[l].astype(BF16),
    )


def kernel(x, norm1_g, w_in, cmp_pe_k, cmp_pe_v, cmp_k_w1, cmp_k_w2, cmp_v_w1, cmp_v_w2, w_attn_br, conv_dw_w, conv_dw_b, conv_ln_g, conv_ln_b, w_conv_br, w_out, norm2_g, ffn_w_up, ffn_dw_w, ffn_dw_b, ffn_w_down, final_g):
    batch, seq, d = x.shape
    assert d == D_MODEL and seq % ROW_TILE == 0 and seq % FFN_ROW_TILE == 0 and seq % CMP_Q_TILE == 0
    assert seq // SEL_BLOCK == AUG_POS
    x2 = x.reshape(batch * seq, d)
    for l in range(w_in.shape[0]):
        p = _prep_layer(l, norm1_g, w_in, cmp_pe_k, cmp_pe_v, cmp_k_w1, cmp_k_w2, cmp_v_w1, cmp_v_w2, w_attn_br,
                        conv_dw_w, conv_dw_b, conv_ln_g, conv_ln_b, w_conv_br, w_out, norm2_g, ffn_w_up,
                        ffn_dw_w, ffn_dw_b, ffn_w_down)
        last = l == w_in.shape[0] - 1
        x2 = _layer(x2, batch, seq, p, final_g[None, :] if last else None)
    return x2.reshape(batch, seq, d)
```

```python
import functools
import math

import numpy as np
import jax
import jax.numpy as jnp
from jax import lax
from jax.experimental import pallas as pl
from jax.experimental.pallas import tpu as pltpu

F32 = jnp.float32
BF16 = jnp.bfloat16

D_MODEL = 1024
N_HEADS = 8
HEAD_DIM = 64
N_KV_GROUPS = 2
HEADS_PER_GROUP = N_HEADS // N_KV_GROUPS
ATTN_WIDTH = N_HEADS * HEAD_DIM
KV_WIDTH = N_KV_GROUPS * HEAD_DIM
CMP_BLOCK = 32
CMP_STRIDE = 16
CMP_HIDDEN = 128
SEL_BLOCK = 64
SEL_TOP = 16
SEL_SHIFT = 6
WINDOW = 512
FORCE_BONUS = 1.0e4
NEG_INF = -1.0e30
CONV_CH = 512
CONV_WIDTH = 31
FFN_DIM = 2816
FFN_CONV_WIDTH = 3
NORM_EPS = 1e-6
N_GATES = 3 * N_HEADS
ATTN_SCALE = HEAD_DIM ** -0.5
UNSELECTED = -(2.0 ** 99)

VMEM_LIMIT_BYTES = 56 * 1024 * 1024
LANES = 128

ROW_TILE = 512
FFN_ROW_TILE = 512
ATTN_TILE = 256
SOFTMAX_COLS = 128
CMP_Q_TILE = 512
CONV_TILE = 256
CONV_HALO = 32
FFN_COL_TILE = 2816
FFN_HALO = 8

AUG = LANES - HEAD_DIM
AUG_POS = N_HEADS * 4
GATE_ROWS = 16
Q_ROWS = HEADS_PER_GROUP * HEAD_DIM

assert AUG_POS == 2048 // SEL_BLOCK and AUG_POS + 4 <= AUG


def _params(sem):
    return pltpu.CompilerParams(dimension_semantics=sem, vmem_limit_bytes=VMEM_LIMIT_BYTES)


def _rms(x, g):
    y = x * lax.rsqrt(jnp.mean(x * x, axis=-1, keepdims=True) + NORM_EPS)
    return y * g


def _sigmoid(x):
    return 1.0 / (1.0 + jnp.exp(-x))


def _dot_nt(a, b, **kw):
    return lax.dot_general(a, b, (((1,), (1,)), ((), ())), preferred_element_type=F32, **kw)


def _dot_tn(a, b):
    return lax.dot_general(a, b, (((0,), (0,)), ((), ())), preferred_element_type=F32)


def _inproj_kernel(x_ref, g_ref, wn_ref, wt_ref, aug_ref,
                   kall_ref, kc_ref, vc_ref, uc_ref, gm_ref, qt_ref, vt_ref, gnt_ref):
    h = _rms(x_ref[...], g_ref[...]).astype(BF16)
    c = 0
    for ref, extra in ((kall_ref.at[0], aug_ref.at[:, 0:2 * LANES]), (kall_ref.at[1], aug_ref.at[:, 2 * LANES:4 * LANES]),
                       (kc_ref, None), (vc_ref, None), (uc_ref, None), (gm_ref, None)):
        n = ref.shape[1]
        y = jnp.dot(h, wn_ref[:, c:c + n], preferred_element_type=F32)
        if extra is not None:
            y = y + extra[...]
        ref[...] = y.astype(ref.dtype)
        c += n
    t = _dot_nt(wt_ref[...], h)
    nq, nv = qt_ref.shape[0], vt_ref.shape[0]
    qt_ref[...] = (t[0:nq] * ATTN_SCALE).astype(BF16)
    vt_ref[...] = t[nq:nq + nv].astype(BF16)
    gnt_ref[...] = t[nq + nv:]


def _inproj(x2, g, wn, wt, aug, seq):
    t = x2.shape[0]
    tm = ROW_TILE
    row = lambda n: pl.BlockSpec((tm, n), lambda i: (i, 0))
    col = lambda n: pl.BlockSpec((n, tm), lambda i: (0, i))
    full = lambda a: pl.BlockSpec(a.shape, lambda i: (0, 0))
    kw = N_KV_GROUPS * LANES
    nat = ((KV_WIDTH, F32), (KV_WIDTH, F32), (2 * CONV_CH, BF16), (2 * D_MODEL, BF16))
    trn = ((ATTN_WIDTH, BF16), (2 * KV_WIDTH, BF16), (N_KV_GROUPS * GATE_ROWS, F32))
    return pl.pallas_call(
        _inproj_kernel,
        grid=(t // tm,),
        in_specs=[row(D_MODEL), full(g), full(wn), full(wt),
                  pl.BlockSpec((tm, 2 * kw), lambda i: (i % (seq // tm), 0))],
        out_specs=[pl.BlockSpec((2, tm, kw), lambda i: (0, i, 0))] + [row(n) for n, _ in nat]
                  + [col(n) for n, _ in trn],
        out_shape=[jax.ShapeDtypeStruct((2, t, kw), BF16)] + [jax.ShapeDtypeStruct((t, n), d) for n, d in nat]
                  + [jax.ShapeDtypeStruct((n, t), d) for n, d in trn],
        compiler_params=_params(("parallel",)),
        name="inproj",
    )(x2, g, wn, wt, aug)


def _gelu_tanh(x):
    return 0.5 * x * (1.0 + jnp.tanh(math.sqrt(2.0 / math.pi) * (x + 0.044715 * (x * x * x))))


def _compress(x_ref, pe_ref, w1_ref, w2_ref, nr):
    half = CMP_BLOCK // 2
    a = b = None
    for l in range(half):
        x = x_ref[pl.ds(l, nr, stride=CMP_STRIDE), :]
        ta = jnp.dot((x + pe_ref[l:l + 1, :]).astype(BF16), w1_ref[l], preferred_element_type=F32)
        tb = jnp.dot((x + pe_ref[half + l:half + l + 1, :]).astype(BF16), w1_ref[half + l],
                     preferred_element_type=F32)
        a = ta if a is None else a + ta
        b = tb if b is None else b + tb
    pre = a + pltpu.roll(b, nr - 1, 0)
    hid = _gelu_tanh(pre).astype(BF16)
    return jnp.dot(hid, w2_ref[...], preferred_element_type=F32)


def _cmp_kernel(kc_ref, vc_ref, pek_ref, pev_ref, w1k_ref, w2k_ref, w1v_ref, w2v_ref,
                qt_ref, mapt_ref, oct_ref, selb_ref, *, seq):
    ng, nh, dh = N_KV_GROUPS, HEADS_PER_GROUP, HEAD_DIM
    nr = seq // CMP_STRIDE
    kc = _compress(kc_ref, pek_ref, w1k_ref, w2k_ref, nr).astype(BF16)
    vct = jnp.transpose(_compress(vc_ref, pev_ref, w1v_ref, w2v_ref, nr)).astype(BF16)
    nsel = seq // SEL_BLOCK
    tq = CMP_Q_TILE
    c_end = lax.broadcasted_iota(jnp.int32, (nr, 1), 0) * CMP_STRIDE + (CMP_BLOCK - 1)
    blk = lax.broadcasted_iota(jnp.int32, (nsel, 1), 0)
    zeros_q = jnp.zeros((dh, tq), BF16)

    n_top = min(SEL_TOP, nsel)
    for ti in range(seq // tq):
        t0, t1 = ti * tq, (ti + 1) * tq
        nrv, nbv = t1 // CMP_STRIDE, t1 // SEL_BLOCK
        pos = t0 + lax.broadcasted_iota(jnp.int32, (1, tq), 1)
        dist = pos - c_end[:nrv]
        valid = dist >= 0
        distf = dist.astype(F32)
        cur = pos >> SEL_SHIFT
        blk_v = blk[:nbv]
        forced = (blk_v == 0) | (blk_v == cur) | (blk_v == cur - 1)
        causal = blk_v <= cur
        for g in range(ng):
            psum = jnp.zeros((nrv, tq), F32)
            for j in range(nh):
                slope = 2.0 ** -(g * nh + j + 1)
                rows = slice((g * nh + j) * dh, (g * nh + j + 1) * dh)
                q = qt_ref[rows, t0:t1]
                q2 = jnp.concatenate([q, zeros_q] if g == 0 else [zeros_q, q], axis=0)
                s = jnp.dot(kc[:nrv], q2, preferred_element_type=F32)
                s = jnp.where(valid, s - slope * distf, NEG_INF)
                m = jnp.max(s, axis=0, keepdims=True)
                e = jnp.where(valid, jnp.exp(s - m), 0.0)
                l = jnp.sum(e, axis=0, keepdims=True)
                p = e / jnp.where(l > 0.0, l, 1.0)
                psum = psum + p
                pfull = p.astype(BF16)
                if nrv < nr:
                    pfull = jnp.concatenate([pfull, jnp.zeros((nr - nrv, tq), BF16)], axis=0)
                oct_ref[rows, t0:t1] = jnp.dot(vct[g * dh:(g + 1) * dh, :], pfull, preferred_element_type=F32)
            if nrv < nr:
                psum = jnp.concatenate([psum, jnp.zeros((nr - nrv, tq), F32)], axis=0)
            imp = jnp.dot(mapt_ref[0:nbv, :], psum, precision=lax.Precision.HIGHEST, preferred_element_type=F32)
            val = jnp.where(causal, imp + jnp.where(forced, FORCE_BONUS, 0.0), NEG_INF)
            cnt = jnp.zeros((nbv, tq), F32)
            for i in range(nbv):
                vi = val[i:i + 1, :]
                tie = (blk_v > i).astype(F32)
                cnt = cnt + jnp.where(vi > val, 1.0, jnp.where(vi == val, tie, 0.0))
            selb = jnp.where(causal & (cnt < n_top), 0.0, UNSELECTED)
            if nbv < nsel:
                selb = jnp.concatenate([selb, jnp.full((nsel - nbv, tq), UNSELECTED, F32)], axis=0)
            selb_ref[0, g, :, t0:t1] = selb.astype(BF16)


def _cmp_branch(kc, vc, pek, pev, w1k, w2k, w1v, w2v, qt, mapt, batch, seq):
    nsel = seq // SEL_BLOCK
    full = lambda a: pl.BlockSpec(a.shape, lambda i: (0,) * a.ndim)
    rows = pl.BlockSpec((seq, KV_WIDTH), lambda i: (i, 0))
    heads = pl.BlockSpec((ATTN_WIDTH, seq), lambda i: (0, i))
    return pl.pallas_call(
        functools.partial(_cmp_kernel, seq=seq),
        grid=(batch,),
        in_specs=[rows, rows, full(pek), full(pev), full(w1k), full(w2k), full(w1v), full(w2v), heads, full(mapt)],
        out_specs=[heads, pl.BlockSpec((1, N_KV_GROUPS, nsel, seq), lambda i: (i, 0, 0, 0))],
        out_shape=[jax.ShapeDtypeStruct((ATTN_WIDTH, batch * seq), F32),
                   jax.ShapeDtypeStruct((batch, N_KV_GROUPS, nsel, seq), BF16)],
        compiler_params=_params(("parallel",)),
        name="cmp_topk",
    )(kc, vc, pek, pev, w1k, w2k, w1v, w2v, qt, mapt)


def _attn_kernel(qt_ref, k_ref, v_ref, selb_ref, oct_ref, gnt_ref, o_ref,
                 qa_ref, s_ref, p_ref, a_ref, c_ref, m_ref, l_ref, acc_ref, out_ref):
    qi = pl.program_id(1)
    ta = ATTN_TILE
    ng, nh, dh = N_KV_GROUPS, HEADS_PER_GROUP, HEAD_DIM
    nsel = selb_ref.shape[2]
    groups = range(ng)
    pos = qi * ta + lax.broadcasted_iota(jnp.int32, (1, ta), 1)
    tb = (pos >> SEL_SHIFT).astype(F32)
    tr = (pos & (SEL_BLOCK - 1)).astype(F32)
    arow = lax.broadcasted_iota(jnp.int32, (16, 1), 0)
    gate = _sigmoid(gnt_ref[...])

    def gate_row(g, c):
        r = g * GATE_ROWS + c
        return jnp.concatenate([gate[r + 3 * j:r + 3 * j + 1, :] for j in range(nh)], axis=1)

    for g in groups:
        for j in range(nh):
            slope = 2.0 ** -(g * nh + j + 1)
            cols = slice(j * ta, (j + 1) * ta)
            rows = slice((g * nh + j) * dh, (g * nh + j + 1) * dh)
            qa_ref[g, 0:dh, cols] = qt_ref[rows, :]
            qa_ref[g, dh:dh + nsel, cols] = selb_ref[0, g]
            alibi = jnp.where(arow == 0, slope * SEL_BLOCK,
                              jnp.where(arow == 1, slope,
                                        jnp.where(arow == 2, -slope * SEL_BLOCK * tb,
                                                  jnp.where(arow == 3, -slope * tr, 0.0))))
            qa_ref[g, dh + nsel:dh + nsel + 16, cols] = alibi.astype(BF16)
            qa_ref[g, dh + nsel + 16:, cols] = jnp.zeros((LANES - dh - nsel - 16, ta), BF16)
            out_ref[g, :, cols] = gate[g * GATE_ROWS + 3 * j:g * GATE_ROWS + 3 * j + 1, :] * oct_ref[rows, :]

    krow = lax.broadcasted_iota(jnp.int32, (ta, 1), 0)
    qcol = lax.broadcasted_iota(jnp.int32, (1, ta), 1)
    lanes4 = lambda b: jnp.concatenate([b] * nh, axis=1)

    def qk(g, br, kt):
        k = k_ref[br, pl.ds(pl.multiple_of(kt * ta, ta), ta), g * LANES:(g + 1) * LANES]
        return jnp.dot(k, qa_ref[g], preferred_element_type=F32)

    def pv(g, br, kt, p):
        v = v_ref[br, g, :, pl.ds(pl.multiple_of(kt * ta, ta), ta)]
        return jnp.dot(v, p.astype(BF16), preferred_element_type=F32)

    far = WINDOW // ta
    causal = lanes4(jnp.where(qcol >= krow, 0.0, NEG_INF))
    kt_far = jnp.maximum(qi - far, 0)
    beyond = lanes4(jnp.where((qcol < krow) & (qi >= far), 0.0, NEG_INF))
    for g in groups:
        s_sel = qk(g, 0, qi) + causal
        m_sel = jnp.max(s_sel, axis=0, keepdims=True)
        p_sel = jnp.exp(s_sel - m_sel)
        m_ref[g, 0] = m_sel
        l_ref[g, 0] = jnp.sum(p_sel, axis=0, keepdims=True)
        acc_ref[g, 0] = pv(g, 0, qi, p_sel)
        s_win = qk(g, 1, qi) + causal
        s_far = qk(g, 1, kt_far) + beyond
        m_win = jnp.maximum(jnp.max(s_win, axis=0, keepdims=True), jnp.max(s_far, axis=0, keepdims=True))
        p_win = jnp.exp(s_win - m_win)
        p_far = jnp.exp(s_far - m_win)
        m_ref[g, 1] = m_win
        l_ref[g, 1] = jnp.sum(p_win, axis=0, keepdims=True) + jnp.sum(p_far, axis=0, keepdims=True)
        acc_ref[g, 1] = pv(g, 1, qi, p_win) + pv(g, 1, kt_far, p_far)

    n_tasks = qi + 1

    def task(n):
        n = jnp.minimum(n, n_tasks - 1)
        return jnp.where(n == 0, 1, 0), qi - jnp.maximum(n, 1)

    def scores(n, slot):
        br, kt = task(n)
        for g in groups:
            s = qk(g, br, kt)
            s_ref[g, slot] = s
            c_ref[g, slot] = jnp.max(s, axis=0, keepdims=True)

    def softmax(n, slot):
        br, _ = task(n)
        for c in range(nh * ta // SOFTMAX_COLS):
            cols = slice(c * SOFTMAX_COLS, (c + 1) * SOFTMAX_COLS)
            for g in groups:
                m_old = m_ref[g, br, :, cols]
                m_new = jnp.maximum(m_old, c_ref[g, slot, :, cols])
                alpha = jnp.exp(m_old - m_new)
                p = jnp.exp(s_ref[g, slot, :, cols] - m_new)
                l_ref[g, br, :, cols] = alpha * l_ref[g, br, :, cols] + jnp.sum(p, axis=0, keepdims=True)
                m_ref[g, br, :, cols] = m_new
                a_ref[g, slot, :, cols] = alpha
                p_ref[g, slot, :, cols] = p.astype(BF16)

    def values(n, slot):
        br, kt = task(n)
        for g in groups:
            acc_ref[g, br] = a_ref[g, slot] * acc_ref[g, br] + pv(g, br, kt, p_ref[g, slot])

    def stage(n, slot):
        scores(n + 2, slot)
        softmax(n + 1, 1 - slot)
        values(n, slot)

    @pl.when(qi >= 1)
    def _():
        scores(0, 0)
        scores(1, 1)
        softmax(0, 0)
        n_full = n_tasks - 1

        def pair(i, carry):
            stage(2 * i, 0)
            stage(2 * i + 1, 1)
            return carry

        lax.fori_loop(0, n_full // 2, pair, 0)

        @pl.when(n_full % 2 == 1)
        def _():
            stage(n_full - 1, 0)
            values(n_full, 1)

        @pl.when(n_full % 2 == 0)
        def _():
            values(n_full, 0)

    for g in groups:
        out = (out_ref[g] + gate_row(g, 1) * (acc_ref[g, 0] / l_ref[g, 0])
               + gate_row(g, 2) * (acc_ref[g, 1] / l_ref[g, 1]))
        for j in range(nh):
            rows = slice((g * nh + j) * dh, (g * nh + j + 1) * dh)
            o_ref[rows, :] = out[:, j * ta:(j + 1) * ta].astype(BF16)


def _attention(qt, kall, vt, selb, oct, gnt, batch, seq):
    ta = ATTN_TILE
    assert WINDOW % ta == 0
    nqt = seq // ta
    ng, nh, dh = N_KV_GROUPS, HEADS_PER_GROUP, HEAD_DIM
    nsel = selb.shape[2]
    qtile = lambda rows: pl.BlockSpec((rows, ta), lambda b, t: (0, b * nqt + t))
    vall = vt.reshape(2, ng, dh, batch * seq)
    wide = nh * ta
    return pl.pallas_call(
        _attn_kernel,
        grid=(batch, nqt),
        in_specs=[qtile(ATTN_WIDTH),
                  pl.BlockSpec((2, seq, ng * LANES), lambda b, t: (0, b, 0)),
                  pl.BlockSpec((2, ng, dh, seq), lambda b, t: (0, 0, 0, b)),
                  pl.BlockSpec((1, ng, nsel, ta), lambda b, t: (b, 0, 0, t)),
                  qtile(ATTN_WIDTH), qtile(ng * GATE_ROWS)],
        out_specs=qtile(ATTN_WIDTH),
        out_shape=jax.ShapeDtypeStruct((ATTN_WIDTH, batch * seq), BF16),
        scratch_shapes=[pltpu.VMEM((ng, LANES, wide), BF16),
                        pltpu.VMEM((ng, 2, ta, wide), F32),
                        pltpu.VMEM((ng, 2, ta, wide), BF16),
                        pltpu.VMEM((ng, 2, 1, wide), F32),
                        pltpu.VMEM((ng, 2, 1, wide), F32),
                        pltpu.VMEM((ng, 2, 1, wide), F32),
                        pltpu.VMEM((ng, 2, 1, wide), F32),
                        pltpu.VMEM((ng, 2, dh, wide), F32),
                        pltpu.VMEM((ng, dh, wide), F32)],
        compiler_params=_params(("parallel", "arbitrary")),
        name="nsa_flash",
    )(qt, kall, vall, selb, oct, gnt)


def _glu(u):
    u = u.astype(F32)
    return u[:, :CONV_CH] * _sigmoid(u[:, CONV_CH:])


def _conformer_kernel(cur_ref, prev_ref, dww_ref, dwb_ref, lng_ref, lnb_ref, wp_ref, y_ref, z_ref):
    ts, halo = CONV_TILE, CONV_HALO
    first = pl.program_id(1) == 0
    zp = _glu(prev_ref[0, ts - halo:, :])
    z_ref[0:halo, :] = jnp.where(first, 0.0, zp)
    z_ref[halo:, :] = _glu(cur_ref[0])
    base = halo - (CONV_WIDTH - 1)
    sub = 8
    acc = jnp.zeros((ts, CONV_CH), F32) + dwb_ref[...]
    for r in range(sub):
        part = None
        ext = ts + (sub if r else 0)
        for s in range(r, base + CONV_WIDTH, sub):
            if s < base:
                continue
            term = dww_ref[s - base:s - base + 1, :] * z_ref[s - r:s - r + ext, :]
            part = term if part is None else part + term
        acc = acc + part[r:r + ts, :]
    mu = jnp.mean(acc, axis=-1, keepdims=True)
    cen = acc - mu
    var = jnp.mean(cen * cen, axis=-1, keepdims=True)
    y = cen * lax.rsqrt(var + NORM_EPS) * lng_ref[...] + lnb_ref[...]
    y = y * _sigmoid(y)
    y_ref[0] = jnp.dot(y.astype(BF16), wp_ref[...], preferred_element_type=F32).astype(BF16)


def _conformer(u3, dww, dwb, lng, lnb, wp):
    b, seq, w = u3.shape
    ts = CONV_TILE
    full = lambda a: pl.BlockSpec(a.shape, lambda i, t: (0,) * a.ndim)
    return pl.pallas_call(
        _conformer_kernel,
        grid=(b, seq // ts),
        in_specs=[pl.BlockSpec((1, ts, w), lambda i, t: (i, t, 0)),
                  pl.BlockSpec((1, ts, w), lambda i, t: (i, jnp.maximum(t - 1, 0), 0)),
                  full(dww), full(dwb), full(lng), full(lnb), full(wp)],
        out_specs=pl.BlockSpec((1, ts, D_MODEL), lambda i, t: (i, t, 0)),
        out_shape=jax.ShapeDtypeStruct((b, seq, D_MODEL), BF16),
        scratch_shapes=[pltpu.VMEM((ts + CONV_HALO, CONV_CH), F32)],
        compiler_params=_params(("parallel", "arbitrary")),
        name="conformer_conv",
    )(u3, u3, dww, dwb, lng, lnb, wp)


def _merge_kernel(ot_ref, yc_ref, gm_ref, x_ref, wa_ref, wo_ref, out_ref):
    ya = _dot_tn(ot_ref[...], wa_ref[...])
    gm = gm_ref[...].astype(F32)
    mix = _sigmoid(gm[:, :D_MODEL]) * ya + _sigmoid(gm[:, D_MODEL:]) * yc_ref[...].astype(F32)
    out_ref[...] = x_ref[...] + jnp.dot(mix.astype(BF16), wo_ref[...], preferred_element_type=F32)


def _merge(ot, yc2, gm2, x2, wa, wo):
    t = x2.shape[0]
    tm = ROW_TILE
    row = lambda n: pl.BlockSpec((tm, n), lambda i: (i, 0))
    full = lambda a: pl.BlockSpec(a.shape, lambda i: (0, 0))
    return pl.pallas_call(
        _merge_kernel,
        grid=(t // tm,),
        in_specs=[pl.BlockSpec((ATTN_WIDTH, tm), lambda i: (0, i)), row(D_MODEL), row(2 * D_MODEL), row(D_MODEL),
                  full(wa), full(wo)],
        out_specs=row(D_MODEL),
        out_shape=jax.ShapeDtypeStruct((t, D_MODEL), F32),
        compiler_params=_params(("parallel",)),
        name="merge_outproj",
    )(ot, yc2, gm2, x2, wa, wo)


def _ffn_kernel(x_ref, xh_ref, g_ref, wup_ref, cw_ref, cb_ref, wd_ref, *rest, tiles_per_seq, final_norm):
    if final_norm:
        fg_ref, out_ref, h_ref, a_ref, v_ref, gt_ref = rest
    else:
        out_ref, h_ref, a_ref, v_ref, gt_ref = rest
    tm, halo, tf = FFN_ROW_TILE, FFN_HALO, FFN_COL_TILE
    seq_start = pl.program_id(0) % tiles_per_seq == 0
    hh = _rms(xh_ref[...], g_ref[...])
    h_ref[0:halo, :] = jnp.where(seq_start, 0.0, hh).astype(BF16)
    h_ref[halo:, :] = _rms(x_ref[...], g_ref[...]).astype(BF16)
    h = h_ref[...]

    def conv(ref, w, b):
        out = b
        for k in range(FFN_CONV_WIDTH):
            off = halo - (FFN_CONV_WIDTH - 1) + k
            out = out + w[k:k + 1, :] * ref[off:off + tm, :]
        return out

    for j in range(FFN_DIM // tf):
        slot = j % a_ref.shape[0]
        ca_cols = slice(j * tf, (j + 1) * tf)
        cv_cols = slice(FFN_DIM + j * tf, FFN_DIM + (j + 1) * tf)
        a_ref[slot] = jnp.dot(h, wup_ref[:, ca_cols], preferred_element_type=F32)
        v_ref[slot] = jnp.dot(h, wup_ref[:, cv_cols], preferred_element_type=F32)
        ca = conv(a_ref.at[slot], cw_ref[:, ca_cols], cb_ref[:, ca_cols])
        cv = conv(v_ref.at[slot], cw_ref[:, cv_cols], cb_ref[:, cv_cols])
        gt_ref[:, ca_cols] = (ca * _sigmoid(ca) * cv).astype(BF16)
    y = x_ref[...] + jnp.dot(gt_ref[...], wd_ref[...], preferred_element_type=F32)
    out_ref[...] = _rms(y, fg_ref[...]) if final_norm else y


def _ffn(x2, g, wup, cw, cb, wd, seq, final_g=None):
    t = x2.shape[0]
    tm, halo, tf = FFN_ROW_TILE, FFN_HALO, FFN_COL_TILE
    assert FFN_DIM % tf == 0
    slots = min(2, FFN_DIM // tf)
    once = lambda a: pl.BlockSpec(a.shape, lambda i: (0,) * a.ndim, pipeline_mode=pl.Buffered(1))
    args = [x2, x2, g, wup, cw, cb, wd] + ([final_g] if final_g is not None else [])
    return pl.pallas_call(
        functools.partial(_ffn_kernel, tiles_per_seq=seq // tm, final_norm=final_g is not None),
        grid=(t // tm,),
        in_specs=[pl.BlockSpec((tm, D_MODEL), lambda i: (i, 0)),
                  pl.BlockSpec((halo, D_MODEL), lambda i: (jnp.maximum(i * (tm // halo) - 1, 0), 0))]
                 + [once(a) for a in args[2:]],
        out_specs=pl.BlockSpec((tm, D_MODEL), lambda i: (i, 0)),
        out_shape=jax.ShapeDtypeStruct((t, D_MODEL), F32),
        scratch_shapes=[pltpu.VMEM((tm + halo, D_MODEL), BF16), pltpu.VMEM((slots, tm + halo, tf), F32),
                        pltpu.VMEM((slots, tm + halo, tf), F32), pltpu.VMEM((tm, FFN_DIM), BF16)],
        compiler_params=_params(("parallel",)),
        name="conv_ffn",
    )(*args)


def _sel_map_t(seq):
    ncmp = (seq - CMP_BLOCK) // CMP_STRIDE + 1
    nr = seq // CMP_STRIDE
    nsel = seq // SEL_BLOCK
    cs = np.arange(ncmp) * CMP_STRIDE
    ce = cs + CMP_BLOCK - 1
    ss = np.arange(nsel) * SEL_BLOCK
    se = ss + SEL_BLOCK - 1
    ov = np.minimum(ce[:, None], se[None, :]) - np.maximum(cs[:, None], ss[None, :]) + 1
    m = np.zeros((nsel, nr), np.float32)
    m[:, :ncmp] = (np.clip(ov, 0, None).astype(np.float32) / CMP_BLOCK).T
    return jnp.asarray(m)


def _key_aug(seq):
    pos = np.arange(seq)
    nsel = seq // SEL_BLOCK
    a = np.zeros((seq, AUG), np.float32)
    a[:, AUG_POS + 0] = pos // SEL_BLOCK
    a[:, AUG_POS + 1] = pos % SEL_BLOCK
    a[:, AUG_POS + 2] = 1.0
    a[:, AUG_POS + 3] = 1.0
    s = a.copy()
    s[pos, pos // SEL_BLOCK] = 1.0
    assert nsel <= AUG_POS
    z = np.zeros((seq, HEAD_DIM), np.float32)
    return jnp.asarray(np.concatenate([z, s, z, s, z, a, z, a], axis=1))


def _layer(x2, batch, seq, p, final_g):
    kall, kc, vc, uc2, gm2, qt, vt, gnt = _inproj(x2, p["norm1_g"], p["wn"], p["wt"], _key_aug(seq), seq)
    oct, selb = _cmp_branch(kc, vc, p["pek"], p["pev"], p["w1k"], p["w2k"], p["w1v"], p["w2v"], qt,
                            _sel_map_t(seq), batch, seq)
    ot = _attention(qt, kall, vt, selb, oct, gnt, batch, seq)
    yc = _conformer(uc2.reshape(batch, seq, 2 * CONV_CH), p["dww"], p["dwb"], p["lng"], p["lnb"], p["wconv"])
    x2 = _merge(ot, yc.reshape(batch * seq, D_MODEL), gm2, x2, p["wattn"], p["wout"])
    return _ffn(x2, p["norm2_g"], p["wup"], p["cw"], p["cb"], p["wd"], seq, final_g)


def _prep_layer(l, norm1_g, w_in, cmp_pe_k, cmp_pe_v, cmp_k_w1, cmp_k_w2, cmp_v_w1, cmp_v_w2, w_attn_br,
                conv_dw_w, conv_dw_b, conv_ln_g, conv_ln_b, w_conv_br, w_out, norm2_g, ffn_w_up, ffn_dw_w,
                ffn_dw_b, ffn_w_down):
    w = w_in[l].astype(BF16)
    kvw = KV_WIDTH
    c_q = ATTN_WIDTH
    c_kc, c_vc, c_ks, c_vs, c_kw, c_vw = (c_q + i * kvw for i in range(6))
    c_gn = c_q + 6 * kvw
    c_uc = c_gn + N_GATES
    c_gm = c_uc + 2 * CONV_CH

    def widen(cols):
        k = cols.reshape(D_MODEL, N_KV_GROUPS, HEAD_DIM)
        return jnp.pad(k, ((0, 0), (0, 0), (0, AUG))).reshape(D_MODEL, N_KV_GROUPS * LANES)

    wn = jnp.concatenate([widen(w[:, c_ks:c_ks + kvw]), widen(w[:, c_kw:c_kw + kvw]),
                          w[:, c_kc:c_kc + 2 * kvw], w[:, c_uc:c_gm], w[:, c_gm:]], axis=1)
    gates = w[:, c_gn:c_uc].reshape(D_MODEL, N_KV_GROUPS, 3 * HEADS_PER_GROUP)
    gates = jnp.pad(gates, ((0, 0), (0, 0), (0, GATE_ROWS - 3 * HEADS_PER_GROUP)))
    wt = jnp.concatenate([w[:, :c_q], w[:, c_vs:c_vs + kvw], w[:, c_vw:c_vw + kvw],
                          gates.reshape(D_MODEL, N_KV_GROUPS * GATE_ROWS)], axis=1).T

    def per_group(a):
        z = jnp.zeros_like(a)
        return jnp.concatenate([jnp.concatenate([a, z], axis=-1), jnp.concatenate([z, a], axis=-1)], axis=-2)

    assert N_KV_GROUPS == 2
    w1 = lambda a: per_group(a.astype(BF16).reshape(CMP_BLOCK, HEAD_DIM, CMP_HIDDEN))
    pe = lambda a: jnp.concatenate([a] * N_KV_GROUPS, axis=1)
    return dict(
        norm1_g=norm1_g[l][None, :], wn=wn, wt=wt,
        pek=pe(cmp_pe_k[l]), pev=pe(cmp_pe_v[l]),
        w1k=w1(cmp_k_w1[l]), w2k=per_group(cmp_k_w2[l].astype(BF16)),
        w1v=w1(cmp_v_w1[l]), w2v=per_group(cmp_v_w2[l].astype(BF16)),
        wattn=w_attn_br[l].astype(BF16),
        dww=conv_dw_w[l], dwb=conv_dw_b[l][None, :], lng=conv_ln_g[l][None, :], lnb=conv_ln_b[l][None, :],
        wconv=w_conv_br[l].astype(BF16), wout=w_out[l].astype(BF16),
        norm2_g=norm2_g[l][None, :],
        wup=ffn_w_up[l].astype(BF16), cw=ffn_dw_w[l], cb=ffn_dw_b[l][None, :],
        wd=ffn_w_down[l].astype(BF16),
    )


def kernel(x, norm1_g, w_in, cmp_pe_k, cmp_pe_v, cmp_k_w1, cmp_k_w2, cmp_v_w1, cmp_v_w2, w_attn_br, conv_dw_w, conv_dw_b, conv_ln_g, conv_ln_b, w_conv_br, w_out, norm2_g, ffn_w_up, ffn_dw_w, ffn_dw_b, ffn_w_down, final_g):
    batch, seq, d = x.shape
    assert d == D_MODEL and seq % ROW_TILE == 0 and seq % FFN_ROW_TILE == 0 and seq % CMP_Q_TILE == 0
    assert seq // SEL_BLOCK == AUG_POS
    x2 = x.reshape(batch * seq, d)
    for l in range(w_in.shape[0]):
        p = _prep_layer(l, norm1_g, w_in, cmp_pe_k, cmp_pe_v, cmp_k_w1, cmp_k_w2, cmp_v_w1, cmp_v_w2, w_attn_br,
                        conv_dw_w, conv_dw_b, conv_ln_g, conv_ln_b, w_conv_br, w_out, norm2_g, ffn_w_up,
                        ffn_dw_w, ffn_dw_b, ffn_w_down)
        last = l == w_in.shape[0] - 1
        x2 = _layer(x2, batch, seq, p, final_g[None, :] if last else None)
    return x2.reshape(batch, seq, d)
```

```python
import functools
import math

import numpy as np
import jax
import jax.numpy as jnp
from jax import lax
from jax.experimental import pallas as pl
from jax.experimental.pallas import tpu as pltpu

F32 = jnp.float32
BF16 = jnp.bfloat16

D_MODEL = 1024
N_HEADS = 8
HEAD_DIM = 64
N_KV_GROUPS = 2
HEADS_PER_GROUP = N_HEADS // N_KV_GROUPS
ATTN_WIDTH = N_HEADS * HEAD_DIM
KV_WIDTH = N_KV_GROUPS * HEAD_DIM
CMP_BLOCK = 32
CMP_STRIDE = 16
CMP_HIDDEN = 128
SEL_BLOCK = 64
SEL_TOP = 16
SEL_SHIFT = 6
WINDOW = 512
FORCE_BONUS = 1.0e4
NEG_INF = -1.0e30
CONV_CH = 512
CONV_WIDTH = 31
FFN_DIM = 2816
FFN_CONV_WIDTH = 3
NORM_EPS = 1e-6
N_GATES = 3 * N_HEADS
ATTN_SCALE = HEAD_DIM ** -0.5
UNSELECTED = -(2.0 ** 99)

VMEM_LIMIT_BYTES = 56 * 1024 * 1024
LANES = 128

ROW_TILE = 512
FFN_ROW_TILE = 512
ATTN_TILE = 256
SOFTMAX_COLS = 128
CMP_Q_TILE = 512
ANCHOR_LAG = 6
CONV_HALO = 32
FFN_COL_TILE = 2816
FFN_HALO = 8

AUG = LANES - HEAD_DIM
AUG_POS = N_HEADS * 4
GATE_ROWS = 16
Q_ROWS = HEADS_PER_GROUP * HEAD_DIM

assert AUG_POS == 2048 // SEL_BLOCK and AUG_POS + 4 <= AUG


def _params(sem, flags=None):
    return pltpu.CompilerParams(dimension_semantics=sem, vmem_limit_bytes=VMEM_LIMIT_BYTES, flags=flags)


def _rms(x, g):
    y = x * lax.rsqrt(jnp.mean(x * x, axis=-1, keepdims=True) + NORM_EPS)
    return y * g


def _sigmoid(x):
    return 1.0 / (1.0 + jnp.exp(-x))


def _dot_nt(a, b, **kw):
    return lax.dot_general(a, b, (((1,), (1,)), ((), ())), preferred_element_type=F32, **kw)


def _dot_tn(a, b):
    return lax.dot_general(a, b, (((0,), (0,)), ((), ())), preferred_element_type=F32)


def _zero_after(x):
    u = pltpu.bitcast(x.astype(F32), jnp.uint32)
    return pltpu.bitcast(lax.shift_right_logical(lax.shift_right_logical(u, jnp.uint32(16)), jnp.uint32(16)), F32)


def _conformer_rows(z_ref, dww_ref, dwb_ref, lng_ref, lnb_ref, wp_ref, r0, ts, between):
    base = CONV_HALO - (CONV_WIDTH - 1)
    sub = 8
    acc = jnp.zeros((ts, CONV_CH), F32) + dwb_ref[...]
    for r in range(sub):
        zero = between()
        part = None
        ext = ts + (sub if r else 0)
        for s in range(r, base + CONV_WIDTH, sub):
            if s < base:
                continue
            w = dww_ref[s - base:s - base + 1, :]
            if part is None and zero is not None:
                w = w + zero
            term = w * z_ref[r0 + s - r:r0 + s - r + ext, :]
            part = term if part is None else part + term
        acc = acc + part[r:r + ts, :]
    mu = jnp.mean(acc, axis=-1, keepdims=True)
    cen = acc - mu
    var = jnp.mean(cen * cen, axis=-1, keepdims=True)
    y = cen * lax.rsqrt(var + NORM_EPS) * lng_ref[...] + lnb_ref[...]
    y = y * _sigmoid(y)
    return jnp.dot(y.astype(BF16), wp_ref[...], preferred_element_type=F32)


def _inproj_kernel(x_ref, g_ref, wn_ref, wt_ref, aug_ref, dww_ref, dwb_ref, lng_ref, lnb_ref, wp_ref,
                   kall_ref, kc_ref, vc_ref, yc_ref, gm_ref, qt_ref, vt_ref, gnt_ref, z_ref, zprev_ref,
                   *, tiles_per_seq):
    tm, halo = ROW_TILE, CONV_HALO
    h = _rms(x_ref[...], g_ref[...]).astype(BF16)
    kw, kvw = N_KV_GROUPS * LANES, KV_WIDTH
    c_kc = 2 * kw
    c_uc = c_kc + 2 * kvw
    c_gm = c_uc + 2 * CONV_CH
    proj = lambda lo, hi: jnp.dot(h, wn_ref[:, lo:hi], preferred_element_type=F32)

    u = proj(c_uc, c_gm)
    seq_start = pl.program_id(0) % tiles_per_seq == 0
    z_ref[0:halo, :] = jnp.where(seq_start, 0.0, zprev_ref[...])
    z_ref[halo:, :] = u[:, :CONV_CH] * _sigmoid(u[:, CONV_CH:])
    zprev_ref[...] = z_ref[tm:tm + halo, :]

    step = 2 * LANES
    nq, nv = qt_ref.shape[0], vt_ref.shape[0]

    def key_piece(br):
        def run():
            y = proj(br * kw, (br + 1) * kw)
            kall_ref[br] = (y + aug_ref[:, br * kw:(br + 1) * kw]).astype(BF16)
            return y
        return run

    def cmp_piece():
        kc_ref[...] = proj(c_kc, c_kc + kvw)
        y = proj(c_kc + kvw, c_uc)
        vc_ref[...] = y
        return y

    def gate_piece(c):
        def run():
            y = proj(c_gm + c, c_gm + c + step)
            gm_ref[:, c:c + step] = y.astype(BF16)
            return y
        return run

    def t_piece(lo, hi):
        def run():
            t = _dot_nt(wt_ref[lo:hi, :], h)
            if hi <= nq:
                qt_ref[lo:hi, :] = (t * ATTN_SCALE).astype(BF16)
            elif hi <= nq + nv:
                vt_ref[lo - nq:hi - nq, :] = t.astype(BF16)
            else:
                gnt_ref[...] = t
            return t
        return run

    pieces = ([key_piece(0), key_piece(1), cmp_piece] + [gate_piece(c) for c in range(0, 2 * D_MODEL, step)]
              + [t_piece(lo, lo + step) for lo in range(0, nq + nv, step)] + [t_piece(nq + nv, wt_ref.shape[0])])

    n_quarters, groups_per_quarter = 4, 8
    rq = tm // n_quarters
    slots = n_quarters * groups_per_quarter
    emitted = [0, 0]

    pending = []

    def between():
        if emitted[1] * slots < emitted[0] * len(pieces):
            done = pieces[emitted[1]]()
            emitted[1] += 1
            row = _zero_after(done[-8:, -LANES:])[0:1, :]
            pending.append((emitted[0] + ANCHOR_LAG, jnp.concatenate([row] * (CONV_CH // LANES), axis=1)))
        emitted[0] += 1
        zero = None
        while pending and (pending[0][0] < emitted[0] or emitted[0] == slots):
            z = pending.pop(0)[1]
            zero = z if zero is None else zero + z
        return zero

    for k in range(n_quarters):
        yq = _conformer_rows(z_ref, dww_ref, dwb_ref, lng_ref, lnb_ref, wp_ref, k * rq, rq, between)
        yc_ref[k * rq:(k + 1) * rq, :] = yq.astype(BF16)
    assert emitted[1] == len(pieces)


def _inproj(x2, g, wn, wt, aug, dww, dwb, lng, lnb, wp, seq):
    t = x2.shape[0]
    tm = ROW_TILE
    row = lambda n: pl.BlockSpec((tm, n), lambda i: (i, 0))
    col = lambda n: pl.BlockSpec((n, tm), lambda i: (0, i))
    full = lambda a: pl.BlockSpec(a.shape, lambda i: (0, 0))
    kw = N_KV_GROUPS * LANES
    nat = ((KV_WIDTH, F32), (KV_WIDTH, F32), (D_MODEL, BF16), (2 * D_MODEL, BF16))
    trn = ((ATTN_WIDTH, BF16), (2 * KV_WIDTH, BF16), (N_KV_GROUPS * GATE_ROWS, F32))
    return pl.pallas_call(
        functools.partial(_inproj_kernel, tiles_per_seq=seq // tm),
        grid=(t // tm,),
        in_specs=[row(D_MODEL), full(g), full(wn), full(wt),
                  pl.BlockSpec((tm, 2 * kw), lambda i: (i % (seq // tm), 0)),
                  full(dww), full(dwb), full(lng), full(lnb), full(wp)],
        out_specs=[pl.BlockSpec((2, tm, kw), lambda i: (0, i, 0))] + [row(n) for n, _ in nat]
                  + [col(n) for n, _ in trn],
        out_shape=[jax.ShapeDtypeStruct((2, t, kw), BF16)] + [jax.ShapeDtypeStruct((t, n), d) for n, d in nat]
                  + [jax.ShapeDtypeStruct((n, t), d) for n, d in trn],
        scratch_shapes=[pltpu.VMEM((tm + CONV_HALO, CONV_CH), F32), pltpu.VMEM((CONV_HALO, CONV_CH), F32)],
        compiler_params=_params(("arbitrary",)),
        name="inproj",
    )(x2, g, wn, wt, aug, dww, dwb, lng, lnb, wp)


def _gelu_tanh(x):
    return 0.5 * x * (1.0 + jnp.tanh(math.sqrt(2.0 / math.pi) * (x + 0.044715 * (x * x * x))))


def _compress(x_ref, pe_ref, w1_ref, w2_ref, nr):
    half = CMP_BLOCK // 2
    a = b = None
    for l in range(half):
        x = x_ref[pl.ds(l, nr, stride=CMP_STRIDE), :]
        ta = jnp.dot((x + pe_ref[l:l + 1, :]).astype(BF16), w1_ref[l], preferred_element_type=F32)
        tb = jnp.dot((x + pe_ref[half + l:half + l + 1, :]).astype(BF16), w1_ref[half + l],
                     preferred_element_type=F32)
        a = ta if a is None else a + ta
        b = tb if b is None else b + tb
    pre = a + pltpu.roll(b, nr - 1, 0)
    hid = _gelu_tanh(pre).astype(BF16)
    return jnp.dot(hid, w2_ref[...], preferred_element_type=F32)


def _cmp_kernel(kc_ref, vc_ref, pek_ref, pev_ref, w1k_ref, w2k_ref, w1v_ref, w2v_ref,
                qt_ref, mapt_ref, oct_ref, selb_ref, *, seq):
    ng, nh, dh = N_KV_GROUPS, HEADS_PER_GROUP, HEAD_DIM
    nr = seq // CMP_STRIDE
    kc = _compress(kc_ref, pek_ref, w1k_ref, w2k_ref, nr).astype(BF16)
    vct = jnp.transpose(_compress(vc_ref, pev_ref, w1v_ref, w2v_ref, nr)).astype(BF16)
    nsel = seq // SEL_BLOCK
    tq = CMP_Q_TILE
    c_end = lax.broadcasted_iota(jnp.int32, (nr, 1), 0) * CMP_STRIDE + (CMP_BLOCK - 1)
    blk = lax.broadcasted_iota(jnp.int32, (nsel, 1), 0)
    zeros_q = jnp.zeros((dh, tq), BF16)

    n_top = min(SEL_TOP, nsel)
    for ti in range(seq // tq):
        t0, t1 = ti * tq, (ti + 1) * tq
        nrv, nbv = t1 // CMP_STRIDE, t1 // SEL_BLOCK
        pos = t0 + lax.broadcasted_iota(jnp.int32, (1, tq), 1)
        dist = pos - c_end[:nrv]
        valid = dist >= 0
        distf = dist.astype(F32)
        cur = pos >> SEL_SHIFT
        blk_v = blk[:nbv]
        forced = (blk_v == 0) | (blk_v == cur) | (blk_v == cur - 1)
        causal = blk_v <= cur
        for g in range(ng):
            psum = jnp.zeros((nrv, tq), F32)
            for j in range(nh):
                slope = 2.0 ** -(g * nh + j + 1)
                rows = slice((g * nh + j) * dh, (g * nh + j + 1) * dh)
                q = qt_ref[rows, t0:t1]
                q2 = jnp.concatenate([q, zeros_q] if g == 0 else [zeros_q, q], axis=0)
                s = jnp.dot(kc[:nrv], q2, preferred_element_type=F32)
                s = jnp.where(valid, s - slope * distf, NEG_INF)
                m = jnp.max(s, axis=0, keepdims=True)
                e = jnp.where(valid, jnp.exp(s - m), 0.0)
                l = jnp.sum(e, axis=0, keepdims=True)
                p = e / jnp.where(l > 0.0, l, 1.0)
                psum = psum + p
                pfull = p.astype(BF16)
                if nrv < nr:
                    pfull = jnp.concatenate([pfull, jnp.zeros((nr - nrv, tq), BF16)], axis=0)
                oct_ref[rows, t0:t1] = jnp.dot(vct[g * dh:(g + 1) * dh, :], pfull, preferred_element_type=F32)
            if nrv < nr:
                psum = jnp.concatenate([psum, jnp.zeros((nr - nrv, tq), F32)], axis=0)
            imp = jnp.dot(mapt_ref[0:nbv, :], psum, precision=lax.Precision.HIGHEST, preferred_element_type=F32)
            val = jnp.where(causal, imp + jnp.where(forced, FORCE_BONUS, 0.0), NEG_INF)
            sub = 8
            cnts = []
            for c0 in range(0, nbv, sub):
                vc_ = val[c0:c0 + sub, :]
                cnt = jnp.zeros((sub, tq), F32)
                for i in range(nbv):
                    vi = val[i:i + 1, :]
                    if i < c0:
                        cnt = cnt + jnp.where(vi >= vc_, 1.0, 0.0)
                    elif i >= c0 + sub:
                        cnt = cnt + jnp.where(vi > vc_, 1.0, 0.0)
                    else:
                        tie = (blk_v[c0:c0 + sub] > i).astype(F32)
                        cnt = cnt + jnp.where(vi > vc_, 1.0, jnp.where(vi == vc_, tie, 0.0))
                cnts.append(cnt)
            cnt = jnp.concatenate(cnts, axis=0) if len(cnts) > 1 else cnts[0]
            selb = jnp.where(causal & (cnt < n_top), 0.0, UNSELECTED)
            if nbv < nsel:
                selb = jnp.concatenate([selb, jnp.full((nsel - nbv, tq), UNSELECTED, F32)], axis=0)
            selb_ref[0, g, :, t0:t1] = selb.astype(BF16)


def _cmp_branch(kc, vc, pek, pev, w1k, w2k, w1v, w2v, qt, mapt, batch, seq):
    nsel = seq // SEL_BLOCK
    full = lambda a: pl.BlockSpec(a.shape, lambda i: (0,) * a.ndim)
    rows = pl.BlockSpec((seq, KV_WIDTH), lambda i: (i, 0))
    heads = pl.BlockSpec((ATTN_WIDTH, seq), lambda i: (0, i))
    return pl.pallas_call(
        functools.partial(_cmp_kernel, seq=seq),
        grid=(batch,),
        in_specs=[rows, rows, full(pek), full(pev), full(w1k), full(w2k), full(w1v), full(w2v), heads, full(mapt)],
        out_specs=[heads, pl.BlockSpec((1, N_KV_GROUPS, nsel, seq), lambda i: (i, 0, 0, 0))],
        out_shape=[jax.ShapeDtypeStruct((ATTN_WIDTH, batch * seq), F32),
                   jax.ShapeDtypeStruct((batch, N_KV_GROUPS, nsel, seq), BF16)],
        compiler_params=_params(("parallel",)),
        name="cmp_topk",
    )(kc, vc, pek, pev, w1k, w2k, w1v, w2v, qt, mapt)


def _attn_kernel(qt_ref, k_ref, v_ref, selb_ref, oct_ref, gnt_ref, o_ref,
                 qa_ref, s_ref, p_ref, a_ref, c_ref, m_ref, l_ref, acc_ref, out_ref):
    qi = pl.program_id(1)
    ta = ATTN_TILE
    ng, nh, dh = N_KV_GROUPS, HEADS_PER_GROUP, HEAD_DIM
    nsel = selb_ref.shape[2]
    groups = range(ng)
    pos = qi * ta + lax.broadcasted_iota(jnp.int32, (1, ta), 1)
    tb = (pos >> SEL_SHIFT).astype(F32)
    tr = (pos & (SEL_BLOCK - 1)).astype(F32)
    arow = lax.broadcasted_iota(jnp.int32, (16, 1), 0)
    gate = _sigmoid(gnt_ref[...])

    def gate_row(g, c):
        r = g * GATE_ROWS + c
        return jnp.concatenate([gate[r + 3 * j:r + 3 * j + 1, :] for j in range(nh)], axis=1)

    for g in groups:
        for j in range(nh):
            slope = 2.0 ** -(g * nh + j + 1)
            cols = slice(j * ta, (j + 1) * ta)
            rows = slice((g * nh + j) * dh, (g * nh + j + 1) * dh)
            qa_ref[g, 0:dh, cols] = qt_ref[rows, :]
            qa_ref[g, dh:dh + nsel, cols] = selb_ref[0, g]
            alibi = jnp.where(arow == 0, slope * SEL_BLOCK,
                              jnp.where(arow == 1, slope,
                                        jnp.where(arow == 2, -slope * SEL_BLOCK * tb,
                                                  jnp.where(arow == 3, -slope * tr, 0.0))))
            qa_ref[g, dh + nsel:dh + nsel + 16, cols] = alibi.astype(BF16)
            qa_ref[g, dh + nsel + 16:, cols] = jnp.zeros((LANES - dh - nsel - 16, ta), BF16)
            out_ref[g, :, cols] = gate[g * GATE_ROWS + 3 * j:g * GATE_ROWS + 3 * j + 1, :] * oct_ref[rows, :]

    krow = lax.broadcasted_iota(jnp.int32, (ta, 1), 0)
    qcol = lax.broadcasted_iota(jnp.int32, (1, ta), 1)
    lanes4 = lambda b: jnp.concatenate([b] * nh, axis=1)

    def qk(g, br, kt):
        k = k_ref[br, pl.ds(pl.multiple_of(kt * ta, ta), ta), g * LANES:(g + 1) * LANES]
        return jnp.dot(k, qa_ref[g], preferred_element_type=F32)

    def pv(g, br, kt, p):
        v = v_ref[br, g, :, pl.ds(pl.multiple_of(kt * ta, ta), ta)]
        return jnp.dot(v, p.astype(BF16), preferred_element_type=F32)

    far = WINDOW // ta
    causal = lanes4(jnp.where(qcol >= krow, 0.0, NEG_INF))
    kt_far = jnp.maximum(qi - far, 0)
    beyond = lanes4(jnp.where((qcol < krow) & (qi >= far), 0.0, NEG_INF))
    for g in groups:
        s_sel = qk(g, 0, qi) + causal
        m_sel = jnp.max(s_sel, axis=0, keepdims=True)
        p_sel = jnp.exp(s_sel - m_sel)
        m_ref[g, 0] = m_sel
        l_ref[g, 0] = jnp.sum(p_sel, axis=0, keepdims=True)
        acc_ref[g, 0] = pv(g, 0, qi, p_sel)
        s_win = qk(g, 1, qi) + causal
        s_far = qk(g, 1, kt_far) + beyond
        m_win = jnp.maximum(jnp.max(s_win, axis=0, keepdims=True), jnp.max(s_far, axis=0, keepdims=True))
        p_win = jnp.exp(s_win - m_win)
        p_far = jnp.exp(s_far - m_win)
        m_ref[g, 1] = m_win
        l_ref[g, 1] = jnp.sum(p_win, axis=0, keepdims=True) + jnp.sum(p_far, axis=0, keepdims=True)
        acc_ref[g, 1] = pv(g, 1, qi, p_win) + pv(g, 1, kt_far, p_far)

    n_tasks = qi + 1

    def task(n):
        n = jnp.minimum(n, n_tasks - 1)
        return jnp.where(n == 0, 1, 0), qi - jnp.maximum(n, 1)

    def scores(n, slot):
        br, kt = task(n)
        for g in groups:
            s = qk(g, br, kt)
            s_ref[g, slot] = s
            c_ref[g, slot] = jnp.max(s, axis=0, keepdims=True)

    def softmax(n, slot):
        br, _ = task(n)
        for c in range(nh * ta // SOFTMAX_COLS):
            cols = slice(c * SOFTMAX_COLS, (c + 1) * SOFTMAX_COLS)
            for g in groups:
                m_old = m_ref[g, br, :, cols]
                m_new = jnp.maximum(m_old, c_ref[g, slot, :, cols])
                alpha = jnp.exp(m_old - m_new)
                p = jnp.exp(s_ref[g, slot, :, cols] - m_new)
                l_ref[g, br, :, cols] = alpha * l_ref[g, br, :, cols] + jnp.sum(p, axis=0, keepdims=True)
                m_ref[g, br, :, cols] = m_new
                a_ref[g, slot, :, cols] = alpha
                p_ref[g, slot, :, cols] = p.astype(BF16)

    def values(n, slot):
        br, kt = task(n)
        for g in groups:
            acc_ref[g, br] = a_ref[g, slot] * acc_ref[g, br] + pv(g, br, kt, p_ref[g, slot])

    def stage(n, slot):
        scores(n + 2, slot)
        softmax(n + 1, 1 - slot)
        values(n, slot)

    @pl.when(qi >= 1)
    def _():
        scores(0, 0)
        scores(1, 1)
        softmax(0, 0)
        n_full = n_tasks - 1

        def pair(i, carry):
            stage(2 * i, 0)
            stage(2 * i + 1, 1)
            return carry

        lax.fori_loop(0, n_full // 2, pair, 0)

        @pl.when(n_full % 2 == 1)
        def _():
            stage(n_full - 1, 0)
            values(n_full, 1)

        @pl.when(n_full % 2 == 0)
        def _():
            values(n_full, 0)

    for g in groups:
        out = (out_ref[g] + gate_row(g, 1) * (acc_ref[g, 0] / l_ref[g, 0])
               + gate_row(g, 2) * (acc_ref[g, 1] / l_ref[g, 1]))
        for j in range(nh):
            rows = slice((g * nh + j) * dh, (g * nh + j + 1) * dh)
            o_ref[rows, :] = out[:, j * ta:(j + 1) * ta].astype(BF16)


def _attention(qt, kall, vt, selb, oct, gnt, batch, seq):
    ta = ATTN_TILE
    assert WINDOW % ta == 0
    nqt = seq // ta
    ng, nh, dh = N_KV_GROUPS, HEADS_PER_GROUP, HEAD_DIM
    nsel = selb.shape[2]
    qtile = lambda rows: pl.BlockSpec((rows, ta), lambda b, t: (0, b * nqt + t))
    vall = vt.reshape(2, ng, dh, batch * seq)
    wide = nh * ta
    return pl.pallas_call(
        _attn_kernel,
        grid=(batch, nqt),
        in_specs=[qtile(ATTN_WIDTH),
                  pl.BlockSpec((2, seq, ng * LANES), lambda b, t: (0, b, 0)),
                  pl.BlockSpec((2, ng, dh, seq), lambda b, t: (0, 0, 0, b)),
                  pl.BlockSpec((1, ng, nsel, ta), lambda b, t: (b, 0, 0, t)),
                  qtile(ATTN_WIDTH), qtile(ng * GATE_ROWS)],
        out_specs=qtile(ATTN_WIDTH),
        out_shape=jax.ShapeDtypeStruct((ATTN_WIDTH, batch * seq), BF16),
        scratch_shapes=[pltpu.VMEM((ng, LANES, wide), BF16),
                        pltpu.VMEM((ng, 2, ta, wide), F32),
                        pltpu.VMEM((ng, 2, ta, wide), BF16),
                        pltpu.VMEM((ng, 2, 1, wide), F32),
                        pltpu.VMEM((ng, 2, 1, wide), F32),
                        pltpu.VMEM((ng, 2, 1, wide), F32),
                        pltpu.VMEM((ng, 2, 1, wide), F32),
                        pltpu.VMEM((ng, 2, dh, wide), F32),
                        pltpu.VMEM((ng, dh, wide), F32)],
        compiler_params=_params(("parallel", "arbitrary")),
        name="nsa_flash",
    )(qt, kall, vall, selb, oct, gnt)


def _merge_kernel(ot_ref, yc_ref, gm_ref, x_ref, wa_ref, wo_ref, out_ref):
    ya = _dot_tn(ot_ref[...], wa_ref[...])
    gm = gm_ref[...].astype(F32)
    mix = _sigmoid(gm[:, :D_MODEL]) * ya + _sigmoid(gm[:, D_MODEL:]) * yc_ref[...].astype(F32)
    out_ref[...] = x_ref[...] + jnp.dot(mix.astype(BF16), wo_ref[...], preferred_element_type=F32)


def _merge(ot, yc2, gm2, x2, wa, wo):
    t = x2.shape[0]
    tm = ROW_TILE
    row = lambda n: pl.BlockSpec((tm, n), lambda i: (i, 0))
    full = lambda a: pl.BlockSpec(a.shape, lambda i: (0, 0))
    return pl.pallas_call(
        _merge_kernel,
        grid=(t // tm,),
        in_specs=[pl.BlockSpec((ATTN_WIDTH, tm), lambda i: (0, i)), row(D_MODEL), row(2 * D_MODEL), row(D_MODEL),
                  full(wa), full(wo)],
        out_specs=row(D_MODEL),
        out_shape=jax.ShapeDtypeStruct((t, D_MODEL), F32),
        compiler_params=_params(("parallel",)),
        name="merge_outproj",
    )(ot, yc2, gm2, x2, wa, wo)


def _ffn_kernel(x_ref, xh_ref, g_ref, wup_ref, cw_ref, cb_ref, wd_ref, *rest, tiles_per_seq, final_norm):
    if final_norm:
        fg_ref, out_ref, h_ref, a_ref, v_ref, gt_ref = rest
    else:
        out_ref, h_ref, a_ref, v_ref, gt_ref = rest
    tm, halo, tf = FFN_ROW_TILE, FFN_HALO, FFN_COL_TILE
    seq_start = pl.program_id(0) % tiles_per_seq == 0
    hh = _rms(xh_ref[...], g_ref[...])
    h_ref[0:halo, :] = jnp.where(seq_start, 0.0, hh).astype(BF16)
    h_ref[halo:, :] = _rms(x_ref[...], g_ref[...]).astype(BF16)
    h = h_ref[...]

    def conv(ref, w, b):
        out = b
        for k in range(FFN_CONV_WIDTH):
            off = halo - (FFN_CONV_WIDTH - 1) + k
            out = out + w[k:k + 1, :] * ref[off:off + tm, :]
        return out

    for j in range(FFN_DIM // tf):
        slot = j % a_ref.shape[0]
        ca_cols = slice(j * tf, (j + 1) * tf)
        cv_cols = slice(FFN_DIM + j * tf, FFN_DIM + (j + 1) * tf)
        a_ref[slot] = jnp.dot(h, wup_ref[:, ca_cols], preferred_element_type=F32)
        v_ref[slot] = jnp.dot(h, wup_ref[:, cv_cols], preferred_element_type=F32)
        ca = conv(a_ref.at[slot], cw_ref[:, ca_cols], cb_ref[:, ca_cols])
        cv = conv(v_ref.at[slot], cw_ref[:, cv_cols], cb_ref[:, cv_cols])
        gt_ref[:, ca_cols] = (ca * _sigmoid(ca) * cv).astype(BF16)
    y = x_ref[...] + jnp.dot(gt_ref[...], wd_ref[...], preferred_element_type=F32)
    out_ref[...] = _rms(y, fg_ref[...]) if final_norm else y


def _ffn(x2, g, wup, cw, cb, wd, seq, final_g=None):
    t = x2.shape[0]
    tm, halo, tf = FFN_ROW_TILE, FFN_HALO, FFN_COL_TILE
    assert FFN_DIM % tf == 0
    slots = min(2, FFN_DIM // tf)
    once = lambda a: pl.BlockSpec(a.shape, lambda i: (0,) * a.ndim, pipeline_mode=pl.Buffered(1))
    args = [x2, x2, g, wup, cw, cb, wd] + ([final_g] if final_g is not None else [])
    return pl.pallas_call(
        functools.partial(_ffn_kernel, tiles_per_seq=seq // tm, final_norm=final_g is not None),
        grid=(t // tm,),
        in_specs=[pl.BlockSpec((tm, D_MODEL), lambda i: (i, 0)),
                  pl.BlockSpec((halo, D_MODEL), lambda i: (jnp.maximum(i * (tm // halo) - 1, 0), 0))]
                 + [once(a) for a in args[2:]],
        out_specs=pl.BlockSpec((tm, D_MODEL), lambda i: (i, 0)),
        out_shape=jax.ShapeDtypeStruct((t, D_MODEL), F32),
        scratch_shapes=[pltpu.VMEM((tm + halo, D_MODEL), BF16), pltpu.VMEM((slots, tm + halo, tf), F32),
                        pltpu.VMEM((slots, tm + halo, tf), F32), pltpu.VMEM((tm, FFN_DIM), BF16)],
        compiler_params=_params(("parallel",)),
        name="conv_ffn",
    )(*args)


def _sel_map_t(seq):
    ncmp = (seq - CMP_BLOCK) // CMP_STRIDE + 1
    nr = seq // CMP_STRIDE
    nsel = seq // SEL_BLOCK
    cs = np.arange(ncmp) * CMP_STRIDE
    ce = cs + CMP_BLOCK - 1
    ss = np.arange(nsel) * SEL_BLOCK
    se = ss + SEL_BLOCK - 1
    ov = np.minimum(ce[:, None], se[None, :]) - np.maximum(cs[:, None], ss[None, :]) + 1
    m = np.zeros((nsel, nr), np.float32)
    m[:, :ncmp] = (np.clip(ov, 0, None).astype(np.float32) / CMP_BLOCK).T
    return jnp.asarray(m)


def _key_aug(seq):
    pos = np.arange(seq)
    nsel = seq // SEL_BLOCK
    a = np.zeros((seq, AUG), np.float32)
    a[:, AUG_POS + 0] = pos // SEL_BLOCK
    a[:, AUG_POS + 1] = pos % SEL_BLOCK
    a[:, AUG_POS + 2] = 1.0
    a[:, AUG_POS + 3] = 1.0
    s = a.copy()
    s[pos, pos // SEL_BLOCK] = 1.0
    assert nsel <= AUG_POS
    z = np.zeros((seq, HEAD_DIM), np.float32)
    return jnp.asarray(np.concatenate([z, s, z, s, z, a, z, a], axis=1))


def _layer(x2, batch, seq, p, final_g):
    kall, kc, vc, yc, gm2, qt, vt, gnt = _inproj(x2, p["norm1_g"], p["wn"], p["wt"], _key_aug(seq), p["dww"],
                                                 p["dwb"], p["lng"], p["lnb"], p["wconv"], seq)
    oct, selb = _cmp_branch(kc, vc, p["pek"], p["pev"], p["w1k"], p["w2k"], p["w1v"], p["w2v"], qt,
                            _sel_map_t(seq), batch, seq)
    ot = _attention(qt, kall, vt, selb, oct, gnt, batch, seq)
    x2 = _merge(ot, yc, gm2, x2, p["wattn"], p["wout"])
    return _ffn(x2, p["norm2_g"], p["wup"], p["cw"], p["cb"], p["wd"], seq, final_g)


def _prep_layer(l, norm1_g, w_in, cmp_pe_k, cmp_pe_v, cmp_k_w1, cmp_k_w2, cmp_v_w1, cmp_v_w2, w_attn_br,
                conv_dw_w, conv_dw_b, conv_ln_g, conv_ln_b, w_conv_br, w_out, norm2_g, ffn_w_up, ffn_dw_w,
                ffn_dw_b, ffn_w_down):
    w = w_in[l].astype(BF16)
    kvw = KV_WIDTH
    c_q = ATTN_WIDTH
    c_kc, c_vc, c_ks, c_vs, c_kw, c_vw = (c_q + i * kvw for i in range(6))
    c_gn = c_q + 6 * kvw
    c_uc = c_gn + N_GATES
    c_gm = c_uc + 2 * CONV_CH

    def widen(cols):
        k = cols.reshape(D_MODEL, N_KV_GROUPS, HEAD_DIM)
        return jnp.pad(k, ((0, 0), (0, 0), (0, AUG))).reshape(D_MODEL, N_KV_GROUPS * LANES)

    wn = jnp.concatenate([widen(w[:, c_ks:c_ks + kvw]), widen(w[:, c_kw:c_kw + kvw]),
                          w[:, c_kc:c_kc + 2 * kvw], w[:, c_uc:c_gm], w[:, c_gm:]], axis=1)
    gates = w[:, c_gn:c_uc].reshape(D_MODEL, N_KV_GROUPS, 3 * HEADS_PER_GROUP)
    gates = jnp.pad(gates, ((0, 0), (0, 0), (0, GATE_ROWS - 3 * HEADS_PER_GROUP)))
    wt = jnp.concatenate([w[:, :c_q], w[:, c_vs:c_vs + kvw], w[:, c_vw:c_vw + kvw],
                          gates.reshape(D_MODEL, N_KV_GROUPS * GATE_ROWS)], axis=1).T

    def per_group(a):
        z = jnp.zeros_like(a)
        return jnp.concatenate([jnp.concatenate([a, z], axis=-1), jnp.concatenate([z, a], axis=-1)], axis=-2)

    assert N_KV_GROUPS == 2
    w1 = lambda a: per_group(a.astype(BF16).reshape(CMP_BLOCK, HEAD_DIM, CMP_HIDDEN))
    pe = lambda a: jnp.concatenate([a] * N_KV_GROUPS, axis=1)
    return dict(
        norm1_g=norm1_g[l][None, :], wn=wn, wt=wt,
        pek=pe(cmp_pe_k[l]), pev=pe(cmp_pe_v[l]),
        w1k=w1(cmp_k_w1[l]), w2k=per_group(cmp_k_w2[l].astype(BF16)),
        w1v=w1(cmp_v_w1[l]), w2v=per_group(cmp_v_w2[l].astype(BF16)),
        wattn=w_attn_br[l].astype(BF16),
        dww=conv_dw_w[l], dwb=conv_dw_b[l][None, :], lng=conv_ln_g[l][None, :], lnb=conv_ln_b[l][None, :],
        wconv=w_conv_br[l].astype(BF16), wout=w_out[l].astype(BF16),
        norm2_g=norm2_g[l][None, :],
        wup=ffn_w_up[l].astype(BF16), cw=ffn_dw_w[l], cb=ffn_dw_b[l][None, :],
        wd=ffn_w_down[l].astype(BF16),
    )


def kernel(x, norm1_g, w_in, cmp_pe_k, cmp_pe_v, cmp_k_w1, cmp_k_w2, cmp_v_w1, cmp_v_w2, w_attn_br, conv_dw_w, conv_dw_b, conv_ln_g, conv_ln_b, w_conv_br, w_out, norm2_g, ffn_w_up, ffn_dw_w, ffn_dw_b, ffn_w_down, final_g):
    batch, seq, d = x.shape
    assert d == D_MODEL and seq % ROW_TILE == 0 and seq % FFN_ROW_TILE == 0 and seq % CMP_Q_TILE == 0
    assert seq // SEL_BLOCK == AUG_POS
    x2 = x.reshape(batch * seq, d)
    for l in range(w_in.shape[0]):
        p = _prep_layer(l, norm1_g, w_in, cmp_pe_k, cmp_pe_v, cmp_k_w1, cmp_k_w2, cmp_v_w1, cmp_v_w2, w_attn_br,
                        conv_dw_w, conv_dw_b, conv_ln_g, conv_ln_b, w_conv_br, w_out, norm2_g, ffn_w_up,
                        ffn_dw_w, ffn_dw_b, ffn_w_down)
        last = l == w_in.shape[0] - 1
        x2 = _layer(x2, batch, seq, p, final_g[None, :] if last else None)
    return x2.reshape(batch, seq, d)
```

```python
import functools
import math

import numpy as np
import jax
import jax.numpy as jnp
from jax import lax
from jax.experimental import pallas as pl
from jax.experimental.pallas import tpu as pltpu

F32 = jnp.float32
BF16 = jnp.bfloat16

D_MODEL = 1024
N_HEADS = 8
HEAD_DIM = 64
N_KV_GROUPS = 2
HEADS_PER_GROUP = N_HEADS // N_KV_GROUPS
ATTN_WIDTH = N_HEADS * HEAD_DIM
KV_WIDTH = N_KV_GROUPS * HEAD_DIM
CMP_BLOCK = 32
CMP_STRIDE = 16
CMP_HIDDEN = 128
SEL_BLOCK = 64
SEL_TOP = 16
SEL_SHIFT = 6
WINDOW = 512
FORCE_BONUS = 1.0e4
NEG_INF = -1.0e30
CONV_CH = 512
CONV_WIDTH = 31
FFN_DIM = 2816
FFN_CONV_WIDTH = 3
NORM_EPS = 1e-6
N_GATES = 3 * N_HEADS
ATTN_SCALE = HEAD_DIM ** -0.5
UNSELECTED = -(2.0 ** 99)

VMEM_LIMIT_BYTES = 56 * 1024 * 1024
LANES = 128

ROW_TILE = 512
FFN_ROW_TILE = 512
ATTN_TILE = 256
SOFTMAX_COLS = 128
CMP_Q_TILE = 512
ANCHOR_LAG = 6
ATTN_ANCHOR_LAG = 8
CONV_HALO = 32
FFN_COL_TILE = 2816
FFN_HALO = 8

AUG = LANES - HEAD_DIM
AUG_POS = N_HEADS * 4
GATE_ROWS = 16
Q_ROWS = HEADS_PER_GROUP * HEAD_DIM

assert AUG_POS == 2048 // SEL_BLOCK and AUG_POS + 4 <= AUG


def _params(sem, flags=None):
    return pltpu.CompilerParams(dimension_semantics=sem, vmem_limit_bytes=VMEM_LIMIT_BYTES, flags=flags)


def _rms(x, g):
    y = x * lax.rsqrt(jnp.mean(x * x, axis=-1, keepdims=True) + NORM_EPS)
    return y * g


def _sigmoid(x):
    return 1.0 / (1.0 + jnp.exp(-x))


def _dot_nt(a, b, **kw):
    return lax.dot_general(a, b, (((1,), (1,)), ((), ())), preferred_element_type=F32, **kw)


def _dot_tn(a, b):
    return lax.dot_general(a, b, (((0,), (0,)), ((), ())), preferred_element_type=F32)


def _zero_after(x):
    u = pltpu.bitcast(x.astype(F32), jnp.uint32)
    return pltpu.bitcast(lax.shift_right_logical(lax.shift_right_logical(u, jnp.uint32(16)), jnp.uint32(16)), F32)


def _conformer_rows(z_ref, dww_ref, dwb_ref, lng_ref, lnb_ref, wp_ref, r0, ts, between):
    base = CONV_HALO - (CONV_WIDTH - 1)
    sub = 8
    acc = jnp.zeros((ts, CONV_CH), F32) + dwb_ref[...]
    for r in range(sub):
        zero = between()
        part = None
        ext = ts + (sub if r else 0)
        for s in range(r, base + CONV_WIDTH, sub):
            if s < base:
                continue
            w = dww_ref[s - base:s - base + 1, :]
            if part is None and zero is not None:
                w = w + zero
            term = w * z_ref[r0 + s - r:r0 + s - r + ext, :]
            part = term if part is None else part + term
        acc = acc + part[r:r + ts, :]
    mu = jnp.mean(acc, axis=-1, keepdims=True)
    cen = acc - mu
    var = jnp.mean(cen * cen, axis=-1, keepdims=True)
    y = cen * lax.rsqrt(var + NORM_EPS) * lng_ref[...] + lnb_ref[...]
    y = y * _sigmoid(y)
    return jnp.dot(y.astype(BF16), wp_ref[...], preferred_element_type=F32)


def _inproj_kernel(x_ref, g_ref, wn_ref, wt_ref, aug_ref, dww_ref, dwb_ref, lng_ref, lnb_ref, wp_ref,
                   kall_ref, kc_ref, vc_ref, yc_ref, gm_ref, qt_ref, vt_ref, gnt_ref, z_ref, zprev_ref,
                   *, tiles_per_seq):
    tm, halo = ROW_TILE, CONV_HALO
    h = _rms(x_ref[...], g_ref[...]).astype(BF16)
    kw, kvw = N_KV_GROUPS * LANES, KV_WIDTH
    c_kc = 2 * kw
    c_uc = c_kc + 2 * kvw
    c_gm = c_uc + 2 * CONV_CH
    proj = lambda lo, hi: jnp.dot(h, wn_ref[:, lo:hi], preferred_element_type=F32)

    u = proj(c_uc, c_gm)
    seq_start = pl.program_id(0) % tiles_per_seq == 0
    z_ref[0:halo, :] = jnp.where(seq_start, 0.0, zprev_ref[...])
    z_ref[halo:, :] = u[:, :CONV_CH] * _sigmoid(u[:, CONV_CH:])
    zprev_ref[...] = z_ref[tm:tm + halo, :]

    step = 2 * LANES
    nq, nv = qt_ref.shape[0], vt_ref.shape[0]

    def key_piece(br):
        def run():
            y = proj(br * kw, (br + 1) * kw)
            kall_ref[br] = (y + aug_ref[:, br * kw:(br + 1) * kw]).astype(BF16)
            return y
        return run

    def cmp_piece():
        kc_ref[...] = proj(c_kc, c_kc + kvw)
        y = proj(c_kc + kvw, c_uc)
        vc_ref[...] = y
        return y

    def gate_piece(c):
        def run():
            y = proj(c_gm + c, c_gm + c + step)
            gm_ref[:, c:c + step] = y.astype(BF16)
            return y
        return run

    def t_piece(lo, hi):
        def run():
            t = _dot_nt(wt_ref[lo:hi, :], h)
            if hi <= nq:
                qt_ref[lo:hi, :] = (t * ATTN_SCALE).astype(BF16)
            elif hi <= nq + nv:
                vt_ref[lo - nq:hi - nq, :] = t.astype(BF16)
            else:
                gnt_ref[...] = t
            return t
        return run

    pieces = ([key_piece(0), key_piece(1), cmp_piece] + [gate_piece(c) for c in range(0, 2 * D_MODEL, step)]
              + [t_piece(lo, lo + step) for lo in range(0, nq + nv, step)] + [t_piece(nq + nv, wt_ref.shape[0])])

    n_quarters, groups_per_quarter = 4, 8
    rq = tm // n_quarters
    slots = n_quarters * groups_per_quarter
    emitted = [0, 0]

    pending = []

    def between():
        if emitted[1] * slots < emitted[0] * len(pieces):
            done = pieces[emitted[1]]()
            emitted[1] += 1
            row = _zero_after(done[-8:, -LANES:])[0:1, :]
            pending.append((emitted[0] + ANCHOR_LAG, jnp.concatenate([row] * (CONV_CH // LANES), axis=1)))
        emitted[0] += 1
        zero = None
        while pending and (pending[0][0] < emitted[0] or emitted[0] == slots):
            z = pending.pop(0)[1]
            zero = z if zero is None else zero + z
        return zero

    for k in range(n_quarters):
        yq = _conformer_rows(z_ref, dww_ref, dwb_ref, lng_ref, lnb_ref, wp_ref, k * rq, rq, between)
        yc_ref[k * rq:(k + 1) * rq, :] = yq.astype(BF16)
    assert emitted[1] == len(pieces)


def _inproj(x2, g, wn, wt, aug, dww, dwb, lng, lnb, wp, seq):
    t = x2.shape[0]
    tm = ROW_TILE
    row = lambda n: pl.BlockSpec((tm, n), lambda i: (i, 0))
    col = lambda n: pl.BlockSpec((n, tm), lambda i: (0, i))
    full = lambda a: pl.BlockSpec(a.shape, lambda i: (0, 0))
    kw = N_KV_GROUPS * LANES
    nat = ((KV_WIDTH, F32), (KV_WIDTH, F32), (D_MODEL, BF16), (2 * D_MODEL, BF16))
    trn = ((ATTN_WIDTH, BF16), (2 * KV_WIDTH, BF16), (N_KV_GROUPS * GATE_ROWS, F32))
    return pl.pallas_call(
        functools.partial(_inproj_kernel, tiles_per_seq=seq // tm),
        grid=(t // tm,),
        in_specs=[row(D_MODEL), full(g), full(wn), full(wt),
                  pl.BlockSpec((tm, 2 * kw), lambda i: (i % (seq // tm), 0)),
                  full(dww), full(dwb), full(lng), full(lnb), full(wp)],
        out_specs=[pl.BlockSpec((2, tm, kw), lambda i: (0, i, 0))] + [row(n) for n, _ in nat]
                  + [col(n) for n, _ in trn],
        out_shape=[jax.ShapeDtypeStruct((2, t, kw), BF16)] + [jax.ShapeDtypeStruct((t, n), d) for n, d in nat]
                  + [jax.ShapeDtypeStruct((n, t), d) for n, d in trn],
        scratch_shapes=[pltpu.VMEM((tm + CONV_HALO, CONV_CH), F32), pltpu.VMEM((CONV_HALO, CONV_CH), F32)],
        compiler_params=_params(("arbitrary",)),
        name="inproj",
    )(x2, g, wn, wt, aug, dww, dwb, lng, lnb, wp)


def _gelu_tanh(x):
    return 0.5 * x * (1.0 + jnp.tanh(math.sqrt(2.0 / math.pi) * (x + 0.044715 * (x * x * x))))


def _compress(x_ref, pe_ref, w1_ref, w2_ref, nr):
    half = CMP_BLOCK // 2
    a = b = None
    for l in range(half):
        x = x_ref[pl.ds(l, nr, stride=CMP_STRIDE), :]
        ta = jnp.dot((x + pe_ref[l:l + 1, :]).astype(BF16), w1_ref[l], preferred_element_type=F32)
        tb = jnp.dot((x + pe_ref[half + l:half + l + 1, :]).astype(BF16), w1_ref[half + l],
                     preferred_element_type=F32)
        a = ta if a is None else a + ta
        b = tb if b is None else b + tb
    pre = a + pltpu.roll(b, nr - 1, 0)
    hid = _gelu_tanh(pre).astype(BF16)
    return jnp.dot(hid, w2_ref[...], preferred_element_type=F32)


def _cmp_kernel(kc_ref, vc_ref, pek_ref, pev_ref, w1k_ref, w2k_ref, w1v_ref, w2v_ref,
                qt_ref, mapt_ref, oct_ref, selb_ref, *, seq):
    ng, nh, dh = N_KV_GROUPS, HEADS_PER_GROUP, HEAD_DIM
    nr = seq // CMP_STRIDE
    kc = _compress(kc_ref, pek_ref, w1k_ref, w2k_ref, nr).astype(BF16)
    vct = jnp.transpose(_compress(vc_ref, pev_ref, w1v_ref, w2v_ref, nr)).astype(BF16)
    nsel = seq // SEL_BLOCK
    tq = CMP_Q_TILE
    c_end = lax.broadcasted_iota(jnp.int32, (nr, 1), 0) * CMP_STRIDE + (CMP_BLOCK - 1)
    blk = lax.broadcasted_iota(jnp.int32, (nsel, 1), 0)
    zeros_q = jnp.zeros((dh, tq), BF16)

    n_top = min(SEL_TOP, nsel)
    for ti in range(seq // tq):
        t0, t1 = ti * tq, (ti + 1) * tq
        nrv, nbv = t1 // CMP_STRIDE, t1 // SEL_BLOCK
        pos = t0 + lax.broadcasted_iota(jnp.int32, (1, tq), 1)
        dist = pos - c_end[:nrv]
        valid = dist >= 0
        distf = dist.astype(F32)
        cur = pos >> SEL_SHIFT
        blk_v = blk[:nbv]
        forced = (blk_v == 0) | (blk_v == cur) | (blk_v == cur - 1)
        causal = blk_v <= cur
        for g in range(ng):
            psum = jnp.zeros((nrv, tq), F32)
            for j in range(nh):
                slope = 2.0 ** -(g * nh + j + 1)
                rows = slice((g * nh + j) * dh, (g * nh + j + 1) * dh)
                q = qt_ref[rows, t0:t1]
                q2 = jnp.concatenate([q, zeros_q] if g == 0 else [zeros_q, q], axis=0)
                s = jnp.dot(kc[:nrv], q2, preferred_element_type=F32)
                s = jnp.where(valid, s - slope * distf, NEG_INF)
                m = jnp.max(s, axis=0, keepdims=True)
                e = jnp.where(valid, jnp.exp(s - m), 0.0)
                l = jnp.sum(e, axis=0, keepdims=True)
                p = e / jnp.where(l > 0.0, l, 1.0)
                psum = psum + p
                pfull = p.astype(BF16)
                if nrv < nr:
                    pfull = jnp.concatenate([pfull, jnp.zeros((nr - nrv, tq), BF16)], axis=0)
                oct_ref[rows, t0:t1] = jnp.dot(vct[g * dh:(g + 1) * dh, :], pfull, preferred_element_type=F32)
            if nrv < nr:
                psum = jnp.concatenate([psum, jnp.zeros((nr - nrv, tq), F32)], axis=0)
            imp = jnp.dot(mapt_ref[0:nbv, :], psum, precision=lax.Precision.HIGHEST, preferred_element_type=F32)
            val = jnp.where(causal, imp + jnp.where(forced, FORCE_BONUS, 0.0), NEG_INF)
            sub = 8
            cnts = []
            for c0 in range(0, nbv, sub):
                vc_ = val[c0:c0 + sub, :]
                cnt = jnp.zeros((sub, tq), F32)
                for i in range(nbv):
                    vi = val[i:i + 1, :]
                    if i < c0:
                        cnt = cnt + jnp.where(vi >= vc_, 1.0, 0.0)
                    elif i >= c0 + sub:
                        cnt = cnt + jnp.where(vi > vc_, 1.0, 0.0)
                    else:
                        tie = (blk_v[c0:c0 + sub] > i).astype(F32)
                        cnt = cnt + jnp.where(vi > vc_, 1.0, jnp.where(vi == vc_, tie, 0.0))
                cnts.append(cnt)
            cnt = jnp.concatenate(cnts, axis=0) if len(cnts) > 1 else cnts[0]
            selb = jnp.where(causal & (cnt < n_top), 0.0, UNSELECTED)
            if nbv < nsel:
                selb = jnp.concatenate([selb, jnp.full((nsel - nbv, tq), UNSELECTED, F32)], axis=0)
            selb_ref[0, g, :, t0:t1] = selb.astype(BF16)


def _cmp_branch(kc, vc, pek, pev, w1k, w2k, w1v, w2v, qt, mapt, batch, seq):
    nsel = seq // SEL_BLOCK
    full = lambda a: pl.BlockSpec(a.shape, lambda i: (0,) * a.ndim)
    rows = pl.BlockSpec((seq, KV_WIDTH), lambda i: (i, 0))
    heads = pl.BlockSpec((ATTN_WIDTH, seq), lambda i: (0, i))
    return pl.pallas_call(
        functools.partial(_cmp_kernel, seq=seq),
        grid=(batch,),
        in_specs=[rows, rows, full(pek), full(pev), full(w1k), full(w2k), full(w1v), full(w2v), heads, full(mapt)],
        out_specs=[heads, pl.BlockSpec((1, N_KV_GROUPS, nsel, seq), lambda i: (i, 0, 0, 0))],
        out_shape=[jax.ShapeDtypeStruct((ATTN_WIDTH, batch * seq), F32),
                   jax.ShapeDtypeStruct((batch, N_KV_GROUPS, nsel, seq), BF16)],
        compiler_params=_params(("parallel",)),
        name="cmp_topk",
    )(kc, vc, pek, pev, w1k, w2k, w1v, w2v, qt, mapt)


def _attn_kernel(qt_ref, k_ref, v_ref, selb_ref, oct_ref, gnt_ref, o_ref,
                 qa_ref, s_ref, p_ref, a_ref, c_ref, m_ref, l_ref, acc_ref, out_ref):
    qi = pl.program_id(1)
    ta = ATTN_TILE
    ng, nh, dh = N_KV_GROUPS, HEADS_PER_GROUP, HEAD_DIM
    nsel = selb_ref.shape[2]
    groups = range(ng)
    pos = qi * ta + lax.broadcasted_iota(jnp.int32, (1, ta), 1)
    tb = (pos >> SEL_SHIFT).astype(F32)
    tr = (pos & (SEL_BLOCK - 1)).astype(F32)
    arow = lax.broadcasted_iota(jnp.int32, (16, 1), 0)
    gate = _sigmoid(gnt_ref[...])

    def gate_row(g, c):
        r = g * GATE_ROWS + c
        return jnp.concatenate([gate[r + 3 * j:r + 3 * j + 1, :] for j in range(nh)], axis=1)

    for g in groups:
        for j in range(nh):
            slope = 2.0 ** -(g * nh + j + 1)
            cols = slice(j * ta, (j + 1) * ta)
            rows = slice((g * nh + j) * dh, (g * nh + j + 1) * dh)
            qa_ref[g, 0:dh, cols] = qt_ref[rows, :]
            qa_ref[g, dh:dh + nsel, cols] = selb_ref[0, g]
            alibi = jnp.where(arow == 0, slope * SEL_BLOCK,
                              jnp.where(arow == 1, slope,
                                        jnp.where(arow == 2, -slope * SEL_BLOCK * tb,
                                                  jnp.where(arow == 3, -slope * tr, 0.0))))
            qa_ref[g, dh + nsel:dh + nsel + 16, cols] = alibi.astype(BF16)
            qa_ref[g, dh + nsel + 16:, cols] = jnp.zeros((LANES - dh - nsel - 16, ta), BF16)
            out_ref[g, :, cols] = gate[g * GATE_ROWS + 3 * j:g * GATE_ROWS + 3 * j + 1, :] * oct_ref[rows, :]

    krow = lax.broadcasted_iota(jnp.int32, (ta, 1), 0)
    qcol = lax.broadcasted_iota(jnp.int32, (1, ta), 1)
    lanes4 = lambda b: jnp.concatenate([b] * nh, axis=1)

    def qk(g, br, kt):
        k = k_ref[br, pl.ds(pl.multiple_of(kt * ta, ta), ta), g * LANES:(g + 1) * LANES]
        return jnp.dot(k, qa_ref[g], preferred_element_type=F32)

    def pv(g, br, kt, p):
        v = v_ref[br, g, :, pl.ds(pl.multiple_of(kt * ta, ta), ta)]
        return jnp.dot(v, p.astype(BF16), preferred_element_type=F32)

    far = WINDOW // ta
    causal = lanes4(jnp.where(qcol >= krow, 0.0, NEG_INF))
    kt_far = jnp.maximum(qi - far, 0)
    beyond = lanes4(jnp.where((qcol < krow) & (qi >= far), 0.0, NEG_INF))
    for g in groups:
        s_sel = qk(g, 0, qi) + causal
        m_sel = jnp.max(s_sel, axis=0, keepdims=True)
        p_sel = jnp.exp(s_sel - m_sel)
        m_ref[g, 0] = m_sel
        l_ref[g, 0] = jnp.sum(p_sel, axis=0, keepdims=True)
        acc_ref[g, 0] = pv(g, 0, qi, p_sel)
        s_win = qk(g, 1, qi) + causal
        s_far = qk(g, 1, kt_far) + beyond
        m_win = jnp.maximum(jnp.max(s_win, axis=0, keepdims=True), jnp.max(s_far, axis=0, keepdims=True))
        p_win = jnp.exp(s_win - m_win)
        p_far = jnp.exp(s_far - m_win)
        m_ref[g, 1] = m_win
        l_ref[g, 1] = jnp.sum(p_win, axis=0, keepdims=True) + jnp.sum(p_far, axis=0, keepdims=True)
        acc_ref[g, 1] = pv(g, 1, qi, p_win) + pv(g, 1, kt_far, p_far)

    n_tasks = qi + 1

    def task(n):
        n = jnp.minimum(n, n_tasks - 1)
        return jnp.where(n == 0, 1, 0), qi - jnp.maximum(n, 1)

    def scores(n, slot):
        br, kt = task(n)
        for g in groups:
            s = qk(g, br, kt)
            s_ref[g, slot] = s
            c_ref[g, slot] = jnp.max(s, axis=0, keepdims=True)

    def softmax(n, slot):
        br, _ = task(n)
        for c in range(nh * ta // SOFTMAX_COLS):
            cols = slice(c * SOFTMAX_COLS, (c + 1) * SOFTMAX_COLS)
            for g in groups:
                m_old = m_ref[g, br, :, cols]
                m_new = jnp.maximum(m_old, c_ref[g, slot, :, cols])
                alpha = jnp.exp(m_old - m_new)
                p = jnp.exp(s_ref[g, slot, :, cols] - m_new)
                l_ref[g, br, :, cols] = alpha * l_ref[g, br, :, cols] + jnp.sum(p, axis=0, keepdims=True)
                m_ref[g, br, :, cols] = m_new
                a_ref[g, slot, :, cols] = alpha
                p_ref[g, slot, :, cols] = p.astype(BF16)

    def values(n, slot):
        br, kt = task(n)
        for g in groups:
            acc_ref[g, br] = a_ref[g, slot] * acc_ref[g, br] + pv(g, br, kt, p_ref[g, slot])

    def stage(n, slot):
        br2, kt2 = task(n + 2)
        br1, _ = task(n + 1)
        oslot = 1 - slot
        piece_cols = 2 * LANES
        pieces = [(g, c) for c in range(nh * ta // piece_cols) for g in groups]
        blocks = [(c, g) for c in range(nh * ta // SOFTMAX_COLS) for g in groups]
        per = len(blocks) // len(pieces)
        pending = []
        for bi, (c, g) in enumerate(blocks):
            if bi % per == 0:
                pg, pc = pieces[bi // per]
                pcols = slice(pc * piece_cols, (pc + 1) * piece_cols)
                k = k_ref[br2, pl.ds(pl.multiple_of(kt2 * ta, ta), ta), pg * LANES:(pg + 1) * LANES]
                s = jnp.dot(k, qa_ref[pg, :, pcols], preferred_element_type=F32)
                s_ref[pg, slot, :, pcols] = s
                cmax = jnp.max(s, axis=0, keepdims=True)
                c_ref[pg, slot, :, pcols] = cmax
                pending.append((bi + ATTN_ANCHOR_LAG, _zero_after(cmax[:, :SOFTMAX_COLS])))
            zero = None
            while pending and (pending[0][0] <= bi or bi == len(blocks) - 1):
                z = pending.pop(0)[1]
                zero = z if zero is None else zero + z
            cols = slice(c * SOFTMAX_COLS, (c + 1) * SOFTMAX_COLS)
            m_old = m_ref[g, br1, :, cols]
            m_new = jnp.maximum(m_old, c_ref[g, oslot, :, cols])
            if zero is not None:
                m_new = m_new + zero
            alpha = jnp.exp(m_old - m_new)
            p = jnp.exp(s_ref[g, oslot, :, cols] - m_new)
            l_ref[g, br1, :, cols] = alpha * l_ref[g, br1, :, cols] + jnp.sum(p, axis=0, keepdims=True)
            m_ref[g, br1, :, cols] = m_new
            a_ref[g, oslot, :, cols] = alpha
            p_ref[g, oslot, :, cols] = p.astype(BF16)
        values(n, slot)

    @pl.when(qi >= 1)
    def _():
        scores(0, 0)
        scores(1, 1)
        softmax(0, 0)
        n_full = n_tasks - 1

        def pair(i, carry):
            stage(2 * i, 0)
            stage(2 * i + 1, 1)
            return carry

        lax.fori_loop(0, n_full // 2, pair, 0)

        @pl.when(n_full % 2 == 1)
        def _():
            stage(n_full - 1, 0)
            values(n_full, 1)

        @pl.when(n_full % 2 == 0)
        def _():
            values(n_full, 0)

    for g in groups:
        out = (out_ref[g] + gate_row(g, 1) * (acc_ref[g, 0] / l_ref[g, 0])
               + gate_row(g, 2) * (acc_ref[g, 1] / l_ref[g, 1]))
        for j in range(nh):
            rows = slice((g * nh + j) * dh, (g * nh + j + 1) * dh)
            o_ref[rows, :] = out[:, j * ta:(j + 1) * ta].astype(BF16)


def _attention(qt, kall, vt, selb, oct, gnt, batch, seq):
    ta = ATTN_TILE
    assert WINDOW % ta == 0
    nqt = seq // ta
    ng, nh, dh = N_KV_GROUPS, HEADS_PER_GROUP, HEAD_DIM
    nsel = selb.shape[2]
    qtile = lambda rows: pl.BlockSpec((rows, ta), lambda b, t: (0, b * nqt + t))
    vall = vt.reshape(2, ng, dh, batch * seq)
    wide = nh * ta
    return pl.pallas_call(
        _attn_kernel,
        grid=(batch, nqt),
        in_specs=[qtile(ATTN_WIDTH),
                  pl.BlockSpec((2, seq, ng * LANES), lambda b, t: (0, b, 0)),
                  pl.BlockSpec((2, ng, dh, seq), lambda b, t: (0, 0, 0, b)),
                  pl.BlockSpec((1, ng, nsel, ta), lambda b, t: (b, 0, 0, t)),
                  qtile(ATTN_WIDTH), qtile(ng * GATE_ROWS)],
        out_specs=qtile(ATTN_WIDTH),
        out_shape=jax.ShapeDtypeStruct((ATTN_WIDTH, batch * seq), BF16),
        scratch_shapes=[pltpu.VMEM((ng, LANES, wide), BF16),
                        pltpu.VMEM((ng, 2, ta, wide), F32),
                        pltpu.VMEM((ng, 2, ta, wide), BF16),
                        pltpu.VMEM((ng, 2, 1, wide), F32),
                        pltpu.VMEM((ng, 2, 1, wide), F32),
                        pltpu.VMEM((ng, 2, 1, wide), F32),
                        pltpu.VMEM((ng, 2, 1, wide), F32),
                        pltpu.VMEM((ng, 2, dh, wide), F32),
                        pltpu.VMEM((ng, dh, wide), F32)],
        compiler_params=_params(("parallel", "arbitrary")),
        name="nsa_flash",
    )(qt, kall, vall, selb, oct, gnt)


def _merge_kernel(ot_ref, yc_ref, gm_ref, x_ref, wa_ref, wo_ref, out_ref):
    ya = _dot_tn(ot_ref[...], wa_ref[...])
    gm = gm_ref[...].astype(F32)
    mix = _sigmoid(gm[:, :D_MODEL]) * ya + _sigmoid(gm[:, D_MODEL:]) * yc_ref[...].astype(F32)
    out_ref[...] = x_ref[...] + jnp.dot(mix.astype(BF16), wo_ref[...], preferred_element_type=F32)


def _merge(ot, yc2, gm2, x2, wa, wo):
    t = x2.shape[0]
    tm = ROW_TILE
    row = lambda n: pl.BlockSpec((tm, n), lambda i: (i, 0))
    full = lambda a: pl.BlockSpec(a.shape, lambda i: (0, 0))
    return pl.pallas_call(
        _merge_kernel,
        grid=(t // tm,),
        in_specs=[pl.BlockSpec((ATTN_WIDTH, tm), lambda i: (0, i)), row(D_MODEL), row(2 * D_MODEL), row(D_MODEL),
                  full(wa), full(wo)],
        out_specs=row(D_MODEL),
        out_shape=jax.ShapeDtypeStruct((t, D_MODEL), F32),
        compiler_params=_params(("parallel",)),
        name="merge_outproj",
    )(ot, yc2, gm2, x2, wa, wo)


def _ffn_kernel(x_ref, xh_ref, g_ref, wup_ref, cw_ref, cb_ref, wd_ref, *rest, tiles_per_seq, final_norm):
    if final_norm:
        fg_ref, out_ref, h_ref, a_ref, v_ref, gt_ref = rest
    else:
        out_ref, h_ref, a_ref, v_ref, gt_ref = rest
    tm, halo, tf = FFN_ROW_TILE, FFN_HALO, FFN_COL_TILE
    seq_start = pl.program_id(0) % tiles_per_seq == 0
    hh = _rms(xh_ref[...], g_ref[...])
    h_ref[0:halo, :] = jnp.where(seq_start, 0.0, hh).astype(BF16)
    h_ref[halo:, :] = _rms(x_ref[...], g_ref[...]).astype(BF16)
    h = h_ref[...]

    def conv(ref, w, b):
        out = b
        for k in range(FFN_CONV_WIDTH):
            off = halo - (FFN_CONV_WIDTH - 1) + k
            out = out + w[k:k + 1, :] * ref[off:off + tm, :]
        return out

    for j in range(FFN_DIM // tf):
        slot = j % a_ref.shape[0]
        ca_cols = slice(j * tf, (j + 1) * tf)
        cv_cols = slice(FFN_DIM + j * tf, FFN_DIM + (j + 1) * tf)
        a_ref[slot] = jnp.dot(h, wup_ref[:, ca_cols], preferred_element_type=F32)
        v_ref[slot] = jnp.dot(h, wup_ref[:, cv_cols], preferred_element_type=F32)
        ca = conv(a_ref.at[slot], cw_ref[:, ca_cols], cb_ref[:, ca_cols])
        cv = conv(v_ref.at[slot], cw_ref[:, cv_cols], cb_ref[:, cv_cols])
        gt_ref[:, ca_cols] = (ca * _sigmoid(ca) * cv).astype(BF16)
    y = x_ref[...] + jnp.dot(gt_ref[...], wd_ref[...], preferred_element_type=F32)
    out_ref[...] = _rms(y, fg_ref[...]) if final_norm else y


def _ffn(x2, g, wup, cw, cb, wd, seq, final_g=None):
    t = x2.shape[0]
    tm, halo, tf = FFN_ROW_TILE, FFN_HALO, FFN_COL_TILE
    assert FFN_DIM % tf == 0
    slots = min(2, FFN_DIM // tf)
    once = lambda a: pl.BlockSpec(a.shape, lambda i: (0,) * a.ndim, pipeline_mode=pl.Buffered(1))
    args = [x2, x2, g, wup, cw, cb, wd] + ([final_g] if final_g is not None else [])
    return pl.pallas_call(
        functools.partial(_ffn_kernel, tiles_per_seq=seq // tm, final_norm=final_g is not None),
        grid=(t // tm,),
        in_specs=[pl.BlockSpec((tm, D_MODEL), lambda i: (i, 0)),
                  pl.BlockSpec((halo, D_MODEL), lambda i: (jnp.maximum(i * (tm // halo) - 1, 0), 0))]
                 + [once(a) for a in args[2:]],
        out_specs=pl.BlockSpec((tm, D_MODEL), lambda i: (i, 0)),
        out_shape=jax.ShapeDtypeStruct((t, D_MODEL), F32),
        scratch_shapes=[pltpu.VMEM((tm + halo, D_MODEL), BF16), pltpu.VMEM((slots, tm + halo, tf), F32),
                        pltpu.VMEM((slots, tm + halo, tf), F32), pltpu.VMEM((tm, FFN_DIM), BF16)],
        compiler_params=_params(("parallel",)),
        name="conv_ffn",
    )(*args)


def _sel_map_t(seq):
    ncmp = (seq - CMP_BLOCK) // CMP_STRIDE + 1
    nr = seq // CMP_STRIDE
    nsel = seq // SEL_BLOCK
    cs = np.arange(ncmp) * CMP_STRIDE
    ce = cs + CMP_BLOCK - 1
    ss = np.arange(nsel) * SEL_BLOCK
    se = ss + SEL_BLOCK - 1
    ov = np.minimum(ce[:, None], se[None, :]) - np.maximum(cs[:, None], ss[None, :]) + 1
    m = np.zeros((nsel, nr), np.float32)
    m[:, :ncmp] = (np.clip(ov, 0, None).astype(np.float32) / CMP_BLOCK).T
    return jnp.asarray(m)


def _key_aug(seq):
    pos = np.arange(seq)
    nsel = seq // SEL_BLOCK
    a = np.zeros((seq, AUG), np.float32)
    a[:, AUG_POS + 0] = pos // SEL_BLOCK
    a[:, AUG_POS + 1] = pos % SEL_BLOCK
    a[:, AUG_POS + 2] = 1.0
    a[:, AUG_POS + 3] = 1.0
    s = a.copy()
    s[pos, pos // SEL_BLOCK] = 1.0
    assert nsel <= AUG_POS
    z = np.zeros((seq, HEAD_DIM), np.float32)
    return jnp.asarray(np.concatenate([z, s, z, s, z, a, z, a], axis=1))


def _layer(x2, batch, seq, p, final_g):
    kall, kc, vc, yc, gm2, qt, vt, gnt = _inproj(x2, p["norm1_g"], p["wn"], p["wt"], _key_aug(seq), p["dww"],
                                                 p["dwb"], p["lng"], p["lnb"], p["wconv"], seq)
    oct, selb = _cmp_branch(kc, vc, p["pek"], p["pev"], p["w1k"], p["w2k"], p["w1v"], p["w2v"], qt,
                            _sel_map_t(seq), batch, seq)
    ot = _attention(qt, kall, vt, selb, oct, gnt, batch, seq)
    x2 = _merge(ot, yc, gm2, x2, p["wattn"], p["wout"])
    return _ffn(x2, p["norm2_g"], p["wup"], p["cw"], p["cb"], p["wd"], seq, final_g)


def _prep_layer(l, norm1_g, w_in, cmp_pe_k, cmp_pe_v, cmp_k_w1, cmp_k_w2, cmp_v_w1, cmp_v_w2, w_attn_br,
                conv_dw_w, conv_dw_b, conv_ln_g, conv_ln_b, w_conv_br, w_out, norm2_g, ffn_w_up, ffn_dw_w,
                ffn_dw_b, ffn_w_down):
    w = w_in[l].astype(BF16)
    kvw = KV_WIDTH
    c_q = ATTN_WIDTH
    c_kc, c_vc, c_ks, c_vs, c_kw, c_vw = (c_q + i * kvw for i in range(6))
    c_gn = c_q + 6 * kvw
    c_uc = c_gn + N_GATES
    c_gm = c_uc + 2 * CONV_CH

    def widen(cols):
        k = cols.reshape(D_MODEL, N_KV_GROUPS, HEAD_DIM)
        return jnp.pad(k, ((0, 0), (0, 0), (0, AUG))).reshape(D_MODEL, N_KV_GROUPS * LANES)

    wn = jnp.concatenate([widen(w[:, c_ks:c_ks + kvw]), widen(w[:, c_kw:c_kw + kvw]),
                          w[:, c_kc:c_kc + 2 * kvw], w[:, c_uc:c_gm], w[:, c_gm:]], axis=1)
    gates = w[:, c_gn:c_uc].reshape(D_MODEL, N_KV_GROUPS, 3 * HEADS_PER_GROUP)
    gates = jnp.pad(gates, ((0, 0), (0, 0), (0, GATE_ROWS - 3 * HEADS_PER_GROUP)))
    wt = jnp.concatenate([w[:, :c_q], w[:, c_vs:c_vs + kvw], w[:, c_vw:c_vw + kvw],
                          gates.reshape(D_MODEL, N_KV_GROUPS * GATE_ROWS)], axis=1).T

    def per_group(a):
        z = jnp.zeros_like(a)
        return jnp.concatenate([jnp.concatenate([a, z], axis=-1), jnp.concatenate([z, a], axis=-1)], axis=-2)

    assert N_KV_GROUPS == 2
    w1 = lambda a: per_group(a.astype(BF16).reshape(CMP_BLOCK, HEAD_DIM, CMP_HIDDEN))
    pe = lambda a: jnp.concatenate([a] * N_KV_GROUPS, axis=1)
    return dict(
        norm1_g=norm1_g[l][None, :], wn=wn, wt=wt,
        pek=pe(cmp_pe_k[l]), pev=pe(cmp_pe_v[l]),
        w1k=w1(cmp_k_w1[l]), w2k=per_group(cmp_k_w2[l].astype(BF16)),
        w1v=w1(cmp_v_w1[l]), w2v=per_group(cmp_v_w2[l].astype(BF16)),
        wattn=w_attn_br[l].astype(BF16),
        dww=conv_dw_w[l], dwb=conv_dw_b[l][None, :], lng=conv_ln_g[l][None, :], lnb=conv_ln_b[l][None, :],
        wconv=w_conv_br[l].astype(BF16), wout=w_out[l].astype(BF16),
        norm2_g=norm2_g[l][None, :],
        wup=ffn_w_up[l].astype(BF16), cw=ffn_dw_w[l], cb=ffn_dw_b[l][None, :],
        wd=ffn_w_down[l].astype(BF16),
    )


def kernel(x, norm1_g, w_in, cmp_pe_k, cmp_pe_v, cmp_k_w1, cmp_k_w2, cmp_v_w1, cmp_v_w2, w_attn_br, conv_dw_w, conv_dw_b, conv_ln_g, conv_ln_b, w_conv_br, w_out, norm2_g, ffn_w_up, ffn_dw_w, ffn_dw_b, ffn_w_down, final_g):
    batch, seq, d = x.shape
    assert d == D_MODEL and seq % ROW_TILE == 0 and seq % FFN_ROW_TILE == 0 and seq % CMP_Q_TILE == 0
    assert seq // SEL_BLOCK == AUG_POS
    x2 = x.reshape(batch * seq, d)
    for l in range(w_in.shape[0]):
        p = _prep_layer(l, norm1_g, w_in, cmp_pe_k, cmp_pe_v, cmp_k_w1, cmp_k_w2, cmp_v_w1, cmp_v_w2, w_attn_br,
                        conv_dw_w, conv_dw_b, conv_ln_g, conv_ln_b, w_conv_br, w_out, norm2_g, ffn_w_up,
                        ffn_dw_w, ffn_dw_b, ffn_w_down)
        last = l == w_in.shape[0] - 1
        x2 = _layer(x2, batch, seq, p, final_g[None, :] if last else None)
    return x2.reshape(batch, seq, d)
```

```python
import functools
import math

import numpy as np
import jax
import jax.numpy as jnp
from jax import lax
from jax.experimental import pallas as pl
from jax.experimental.pallas import tpu as pltpu

F32 = jnp.float32
BF16 = jnp.bfloat16

D_MODEL = 1024
N_HEADS = 8
HEAD_DIM = 64
N_KV_GROUPS = 2
HEADS_PER_GROUP = N_HEADS // N_KV_GROUPS
ATTN_WIDTH = N_HEADS * HEAD_DIM
KV_WIDTH = N_KV_GROUPS * HEAD_DIM
CMP_BLOCK = 32
CMP_STRIDE = 16
CMP_HIDDEN = 128
SEL_BLOCK = 64
SEL_TOP = 16
SEL_SHIFT = 6
WINDOW = 512
FORCE_BONUS = 1.0e4
NEG_INF = -1.0e30
CONV_CH = 512
CONV_WIDTH = 31
FFN_DIM = 2816
FFN_CONV_WIDTH = 3
NORM_EPS = 1e-6
N_GATES = 3 * N_HEADS
ATTN_SCALE = HEAD_DIM ** -0.5
UNSELECTED = -(2.0 ** 99)

VMEM_LIMIT_BYTES = 56 * 1024 * 1024
LANES = 128

ROW_TILE = 512
FFN_ROW_TILE = 512
ATTN_TILE = 256
SOFTMAX_COLS = 128
CMP_Q_TILE = 512
ANCHOR_LAG = 6
CONV_HALO = 32
FFN_COL_TILE = 2816
FFN_HALO = 8

AUG = LANES - HEAD_DIM
AUG_POS = N_HEADS * 4
GATE_ROWS = 16
Q_ROWS = HEADS_PER_GROUP * HEAD_DIM

assert AUG_POS == 2048 // SEL_BLOCK and AUG_POS + 4 <= AUG


def _params(sem, flags=None):
    return pltpu.CompilerParams(dimension_semantics=sem, vmem_limit_bytes=VMEM_LIMIT_BYTES, flags=flags)


def _rms(x, g):
    y = x * lax.rsqrt(jnp.mean(x * x, axis=-1, keepdims=True) + NORM_EPS)
    return y * g


def _sigmoid(x):
    return 1.0 / (1.0 + jnp.exp(-x))


def _dot_nt(a, b, **kw):
    return lax.dot_general(a, b, (((1,), (1,)), ((), ())), preferred_element_type=F32, **kw)


def _dot_tn(a, b):
    return lax.dot_general(a, b, (((0,), (0,)), ((), ())), preferred_element_type=F32)


def _zero_after(x):
    u = pltpu.bitcast(x.astype(F32), jnp.uint32)
    return pltpu.bitcast(lax.shift_right_logical(lax.shift_right_logical(u, jnp.uint32(16)), jnp.uint32(16)), F32)


def _conformer_rows(z_ref, dww_ref, dwb_ref, lng_ref, lnb_ref, wp_ref, r0, ts, between):
    base = CONV_HALO - (CONV_WIDTH - 1)
    sub = 8
    acc = jnp.zeros((ts, CONV_CH), F32) + dwb_ref[...]
    for r in range(sub):
        zero = between()
        part = None
        ext = ts + (sub if r else 0)
        for s in range(r, base + CONV_WIDTH, sub):
            if s < base:
                continue
            w = dww_ref[s - base:s - base + 1, :]
            if part is None and zero is not None:
                w = w + zero
            term = w * z_ref[r0 + s - r:r0 + s - r + ext, :]
            part = term if part is None else part + term
        acc = acc + part[r:r + ts, :]
    mu = jnp.mean(acc, axis=-1, keepdims=True)
    cen = acc - mu
    var = jnp.mean(cen * cen, axis=-1, keepdims=True)
    y = cen * lax.rsqrt(var + NORM_EPS) * lng_ref[...] + lnb_ref[...]
    y = y * _sigmoid(y)
    return jnp.dot(y.astype(BF16), wp_ref[...], preferred_element_type=F32)


def _inproj_kernel(x_ref, g_ref, wn_ref, wt_ref, aug_ref, dww_ref, dwb_ref, lng_ref, lnb_ref, wp_ref,
                   kall_ref, kc_ref, vc_ref, yc_ref, gm_ref, qt_ref, vt_ref, gnt_ref, z_ref, zprev_ref,
                   *, tiles_per_seq):
    tm, halo = ROW_TILE, CONV_HALO
    h = _rms(x_ref[...], g_ref[...]).astype(BF16)
    kw, kvw = N_KV_GROUPS * LANES, KV_WIDTH
    c_kc = 2 * kw
    c_uc = c_kc + 2 * kvw
    c_gm = c_uc + 2 * CONV_CH
    proj = lambda lo, hi: jnp.dot(h, wn_ref[:, lo:hi], preferred_element_type=F32)

    u = proj(c_uc, c_gm)
    seq_start = pl.program_id(0) % tiles_per_seq == 0
    z_ref[0:halo, :] = jnp.where(seq_start, 0.0, zprev_ref[...])
    z_ref[halo:, :] = u[:, :CONV_CH] * _sigmoid(u[:, CONV_CH:])
    zprev_ref[...] = z_ref[tm:tm + halo, :]

    step = 2 * LANES
    nq, nv = qt_ref.shape[0], vt_ref.shape[0]

    def key_piece(br):
        def run():
            y = proj(br * kw, (br + 1) * kw)
            kall_ref[br] = (y + aug_ref[:, br * kw:(br + 1) * kw]).astype(BF16)
            return y
        return run

    def cmp_piece():
        kc_ref[...] = proj(c_kc, c_kc + kvw)
        y = proj(c_kc + kvw, c_uc)
        vc_ref[...] = y
        return y

    def gate_piece(c):
        def run():
            y = proj(c_gm + c, c_gm + c + step)
            gm_ref[:, c:c + step] = y.astype(BF16)
            return y
        return run

    def t_piece(lo, hi):
        def run():
            t = _dot_nt(wt_ref[lo:hi, :], h)
            if hi <= nq:
                qt_ref[lo:hi, :] = (t * ATTN_SCALE).astype(BF16)
            elif hi <= nq + nv:
                vt_ref[lo - nq:hi - nq, :] = t.astype(BF16)
            else:
                gnt_ref[...] = t
            return t
        return run

    pieces = ([key_piece(0), key_piece(1), cmp_piece] + [gate_piece(c) for c in range(0, 2 * D_MODEL, step)]
              + [t_piece(lo, lo + step) for lo in range(0, nq + nv, step)] + [t_piece(nq + nv, wt_ref.shape[0])])

    n_quarters, groups_per_quarter = 4, 8
    rq = tm // n_quarters
    slots = n_quarters * groups_per_quarter
    emitted = [0, 0]

    pending = []

    def between():
        if emitted[1] * slots < emitted[0] * len(pieces):
            done = pieces[emitted[1]]()
            emitted[1] += 1
            row = _zero_after(done[-8:, -LANES:])[0:1, :]
            pending.append((emitted[0] + ANCHOR_LAG, jnp.concatenate([row] * (CONV_CH // LANES), axis=1)))
        emitted[0] += 1
        zero = None
        while pending and (pending[0][0] < emitted[0] or emitted[0] == slots):
            z = pending.pop(0)[1]
            zero = z if zero is None else zero + z
        return zero

    for k in range(n_quarters):
        yq = _conformer_rows(z_ref, dww_ref, dwb_ref, lng_ref, lnb_ref, wp_ref, k * rq, rq, between)
        yc_ref[k * rq:(k + 1) * rq, :] = yq.astype(BF16)
    assert emitted[1] == len(pieces)


def _inproj(x2, g, wn, wt, aug, dww, dwb, lng, lnb, wp, seq):
    t = x2.shape[0]
    tm = ROW_TILE
    row = lambda n: pl.BlockSpec((tm, n), lambda i: (i, 0))
    col = lambda n: pl.BlockSpec((n, tm), lambda i: (0, i))
    full = lambda a: pl.BlockSpec(a.shape, lambda i: (0, 0))
    kw = N_KV_GROUPS * LANES
    nat = ((KV_WIDTH, F32), (KV_WIDTH, F32), (D_MODEL, BF16), (2 * D_MODEL, BF16))
    trn = ((ATTN_WIDTH, BF16), (2 * KV_WIDTH, BF16), (N_KV_GROUPS * GATE_ROWS, F32))
    return pl.pallas_call(
        functools.partial(_inproj_kernel, tiles_per_seq=seq // tm),
        grid=(t // tm,),
        in_specs=[row(D_MODEL), full(g), full(wn), full(wt),
                  pl.BlockSpec((tm, 2 * kw), lambda i: (i % (seq // tm), 0)),
                  full(dww), full(dwb), full(lng), full(lnb), full(wp)],
        out_specs=[pl.BlockSpec((2, tm, kw), lambda i: (0, i, 0))] + [row(n) for n, _ in nat]
                  + [col(n) for n, _ in trn],
        out_shape=[jax.ShapeDtypeStruct((2, t, kw), BF16)] + [jax.ShapeDtypeStruct((t, n), d) for n, d in nat]
                  + [jax.ShapeDtypeStruct((n, t), d) for n, d in trn],
        scratch_shapes=[pltpu.VMEM((tm + CONV_HALO, CONV_CH), F32), pltpu.VMEM((CONV_HALO, CONV_CH), F32)],
        compiler_params=_params(("arbitrary",)),
        name="inproj",
    )(x2, g, wn, wt, aug, dww, dwb, lng, lnb, wp)


def _gelu_tanh(x):
    return 0.5 * x * (1.0 + jnp.tanh(math.sqrt(2.0 / math.pi) * (x + 0.044715 * (x * x * x))))


def _compress(x_ref, pe_ref, w1_ref, w2_ref, nr):
    half = CMP_BLOCK // 2
    a = b = None
    for l in range(half):
        x = x_ref[pl.ds(l, nr, stride=CMP_STRIDE), :]
        ta = jnp.dot((x + pe_ref[l:l + 1, :]).astype(BF16), w1_ref[l], preferred_element_type=F32)
        tb = jnp.dot((x + pe_ref[half + l:half + l + 1, :]).astype(BF16), w1_ref[half + l],
                     preferred_element_type=F32)
        a = ta if a is None else a + ta
        b = tb if b is None else b + tb
    pre = a + pltpu.roll(b, nr - 1, 0)
    hid = _gelu_tanh(pre).astype(BF16)
    return jnp.dot(hid, w2_ref[...], preferred_element_type=F32)


def _cmp_kernel(kc_ref, vc_ref, pek_ref, pev_ref, w1k_ref, w2k_ref, w1v_ref, w2v_ref,
                qt_ref, mapt_ref, oct_ref, selb_ref, *, seq):
    ng, nh, dh = N_KV_GROUPS, HEADS_PER_GROUP, HEAD_DIM
    nr = seq // CMP_STRIDE
    kc = _compress(kc_ref, pek_ref, w1k_ref, w2k_ref, nr).astype(BF16)
    vct = jnp.transpose(_compress(vc_ref, pev_ref, w1v_ref, w2v_ref, nr)).astype(BF16)
    nsel = seq // SEL_BLOCK
    tq = CMP_Q_TILE
    c_end = lax.broadcasted_iota(jnp.int32, (nr, 1), 0) * CMP_STRIDE + (CMP_BLOCK - 1)
    blk = lax.broadcasted_iota(jnp.int32, (nsel, 1), 0)
    zeros_q = jnp.zeros((dh, tq), BF16)

    n_top = min(SEL_TOP, nsel)
    for ti in range(seq // tq):
        t0, t1 = ti * tq, (ti + 1) * tq
        nrv, nbv = t1 // CMP_STRIDE, t1 // SEL_BLOCK
        pos = t0 + lax.broadcasted_iota(jnp.int32, (1, tq), 1)
        dist = pos - c_end[:nrv]
        valid = dist >= 0
        distf = dist.astype(F32)
        cur = pos >> SEL_SHIFT
        blk_v = blk[:nbv]
        forced = (blk_v == 0) | (blk_v == cur) | (blk_v == cur - 1)
        causal = blk_v <= cur
        for g in range(ng):
            psum = jnp.zeros((nrv, tq), F32)
            for j in range(nh):
                slope = 2.0 ** -(g * nh + j + 1)
                rows = slice((g * nh + j) * dh, (g * nh + j + 1) * dh)
                q = qt_ref[rows, t0:t1]
                q2 = jnp.concatenate([q, zeros_q] if g == 0 else [zeros_q, q], axis=0)
                s = jnp.dot(kc[:nrv], q2, preferred_element_type=F32)
                s = jnp.where(valid, s - slope * distf, NEG_INF)
                m = jnp.max(s, axis=0, keepdims=True)
                e = jnp.where(valid, jnp.exp(s - m), 0.0)
                l = jnp.sum(e, axis=0, keepdims=True)
                p = e / jnp.where(l > 0.0, l, 1.0)
                psum = psum + p
                pfull = p.astype(BF16)
                if nrv < nr:
                    pfull = jnp.concatenate([pfull, jnp.zeros((nr - nrv, tq), BF16)], axis=0)
                oct_ref[rows, t0:t1] = jnp.dot(vct[g * dh:(g + 1) * dh, :], pfull, preferred_element_type=F32)
            if nrv < nr:
                psum = jnp.concatenate([psum, jnp.zeros((nr - nrv, tq), F32)], axis=0)
            imp = jnp.dot(mapt_ref[0:nbv, :], psum, precision=lax.Precision.HIGHEST, preferred_element_type=F32)
            val = jnp.where(causal, imp + jnp.where(forced, FORCE_BONUS, 0.0), NEG_INF)
            sub = 8
            cnts = []
            for c0 in range(0, nbv, sub):
                vc_ = val[c0:c0 + sub, :]
                cnt = jnp.zeros((sub, tq), F32)
                for i in range(nbv):
                    vi = val[i:i + 1, :]
                    if i < c0:
                        cnt = cnt + jnp.where(vi >= vc_, 1.0, 0.0)
                    elif i >= c0 + sub:
                        cnt = cnt + jnp.where(vi > vc_, 1.0, 0.0)
                    else:
                        tie = (blk_v[c0:c0 + sub] > i).astype(F32)
                        cnt = cnt + jnp.where(vi > vc_, 1.0, jnp.where(vi == vc_, tie, 0.0))
                cnts.append(cnt)
            cnt = jnp.concatenate(cnts, axis=0) if len(cnts) > 1 else cnts[0]
            selb = jnp.where(causal & (cnt < n_top), 0.0, UNSELECTED)
            if nbv < nsel:
                selb = jnp.concatenate([selb, jnp.full((nsel - nbv, tq), UNSELECTED, F32)], axis=0)
            selb_ref[0, g, :, t0:t1] = selb.astype(BF16)


def _cmp_branch(kc, vc, pek, pev, w1k, w2k, w1v, w2v, qt, mapt, batch, seq):
    nsel = seq // SEL_BLOCK
    full = lambda a: pl.BlockSpec(a.shape, lambda i: (0,) * a.ndim)
    rows = pl.BlockSpec((seq, KV_WIDTH), lambda i: (i, 0))
    heads = pl.BlockSpec((ATTN_WIDTH, seq), lambda i: (0, i))
    return pl.pallas_call(
        functools.partial(_cmp_kernel, seq=seq),
        grid=(batch,),
        in_specs=[rows, rows, full(pek), full(pev), full(w1k), full(w2k), full(w1v), full(w2v), heads, full(mapt)],
        out_specs=[heads, pl.BlockSpec((1, N_KV_GROUPS, nsel, seq), lambda i: (i, 0, 0, 0))],
        out_shape=[jax.ShapeDtypeStruct((ATTN_WIDTH, batch * seq), F32),
                   jax.ShapeDtypeStruct((batch, N_KV_GROUPS, nsel, seq), BF16)],
        compiler_params=_params(("parallel",)),
        name="cmp_topk",
    )(kc, vc, pek, pev, w1k, w2k, w1v, w2v, qt, mapt)


def _attn_kernel(qt_ref, k_ref, v_ref, selb_ref, oct_ref, gnt_ref, o_ref,
                 qa_ref, s_ref, p_ref, a_ref, c_ref, m_ref, l_ref, acc_ref, out_ref, sm_ref, pm_ref):
    qi = pl.program_id(1)
    ta = ATTN_TILE
    ng, nh, dh = N_KV_GROUPS, HEADS_PER_GROUP, HEAD_DIM
    nsel = selb_ref.shape[2]
    groups = range(ng)
    pos = qi * ta + lax.broadcasted_iota(jnp.int32, (1, ta), 1)
    tb = (pos >> SEL_SHIFT).astype(F32)
    tr = (pos & (SEL_BLOCK - 1)).astype(F32)
    arow = lax.broadcasted_iota(jnp.int32, (16, 1), 0)
    gate = _sigmoid(gnt_ref[...])

    def gate_row(g, c):
        r = g * GATE_ROWS + c
        return jnp.concatenate([gate[r + 3 * j:r + 3 * j + 1, :] for j in range(nh)], axis=1)

    for g in groups:
        for j in range(nh):
            slope = 2.0 ** -(g * nh + j + 1)
            cols = slice(j * ta, (j + 1) * ta)
            rows = slice((g * nh + j) * dh, (g * nh + j + 1) * dh)
            qa_ref[g, 0:dh, cols] = qt_ref[rows, :]
            qa_ref[g, dh:dh + nsel, cols] = selb_ref[0, g]
            alibi = jnp.where(arow == 0, slope * SEL_BLOCK,
                              jnp.where(arow == 1, slope,
                                        jnp.where(arow == 2, -slope * SEL_BLOCK * tb,
                                                  jnp.where(arow == 3, -slope * tr, 0.0))))
            qa_ref[g, dh + nsel:dh + nsel + 16, cols] = alibi.astype(BF16)
            qa_ref[g, dh + nsel + 16:, cols] = jnp.zeros((LANES - dh - nsel - 16, ta), BF16)
            out_ref[g, :, cols] = gate[g * GATE_ROWS + 3 * j:g * GATE_ROWS + 3 * j + 1, :] * oct_ref[rows, :]

    krow = lax.broadcasted_iota(jnp.int32, (ta, 1), 0)
    qcol = lax.broadcasted_iota(jnp.int32, (1, ta), 1)
    lanes4 = lambda b: jnp.concatenate([b] * nh, axis=1)

    def qk(g, br, kt):
        k = k_ref[br, pl.ds(pl.multiple_of(kt * ta, ta), ta), g * LANES:(g + 1) * LANES]
        return jnp.dot(k, qa_ref[g], preferred_element_type=F32)

    def pv(g, br, kt, p):
        v = v_ref[br, g, :, pl.ds(pl.multiple_of(kt * ta, ta), ta)]
        return jnp.dot(v, p.astype(BF16), preferred_element_type=F32)

    far = WINDOW // ta
    causal = jnp.where(qcol >= krow, 0.0, NEG_INF)
    kt_far = jnp.maximum(qi - far, 0)
    beyond = jnp.where((qcol < krow) & (qi >= far), 0.0, NEG_INF)
    tiles = ((0, qi, causal), (1, qi, causal), (1, kt_far, beyond))
    for g in groups:
        for t, (br, kt, _) in enumerate(tiles):
            sm_ref[g, t] = qk(g, br, kt)
    for c in range(nh * ta // SOFTMAX_COLS):
        cols = slice(c * SOFTMAX_COLS, (c + 1) * SOFTMAX_COLS)
        mcols = slice(c * SOFTMAX_COLS % ta, c * SOFTMAX_COLS % ta + SOFTMAX_COLS)
        for g in groups:
            sb = [sm_ref[g, t, :, cols] + bias[:, mcols] for t, (_, _, bias) in enumerate(tiles)]
            cm = [jnp.max(x, axis=0, keepdims=True) for x in sb]
            m = [cm[0], jnp.maximum(cm[1], cm[2])]
            lsum = [None, None]
            for t, (br, _, _) in enumerate(tiles):
                p = jnp.exp(sb[t] - m[br])
                pm_ref[g, t, :, cols] = p.astype(BF16)
                part = jnp.sum(p, axis=0, keepdims=True)
                lsum[br] = part if lsum[br] is None else lsum[br] + part
            for br in range(2):
                m_ref[g, br, :, cols] = m[br]
                l_ref[g, br, :, cols] = lsum[br]
    for g in groups:
        acc_ref[g, 0] = pv(g, 0, qi, pm_ref[g, 0])
        acc_ref[g, 1] = pv(g, 1, qi, pm_ref[g, 1]) + pv(g, 1, kt_far, pm_ref[g, 2])

    n_tasks = qi + 1

    def task(n):
        n = jnp.minimum(n, n_tasks - 1)
        return jnp.where(n == 0, 1, 0), qi - jnp.maximum(n, 1)

    def scores(n, slot):
        br, kt = task(n)
        for g in groups:
            s = qk(g, br, kt)
            s_ref[g, slot] = s
            c_ref[g, slot] = jnp.max(s, axis=0, keepdims=True)

    def softmax(n, slot):
        br, _ = task(n)
        for c in range(nh * ta // SOFTMAX_COLS):
            cols = slice(c * SOFTMAX_COLS, (c + 1) * SOFTMAX_COLS)
            for g in groups:
                m_old = m_ref[g, br, :, cols]
                m_new = jnp.maximum(m_old, c_ref[g, slot, :, cols])
                alpha = jnp.exp(m_old - m_new)
                p = jnp.exp(s_ref[g, slot, :, cols] - m_new)
                l_ref[g, br, :, cols] = alpha * l_ref[g, br, :, cols] + jnp.sum(p, axis=0, keepdims=True)
                m_ref[g, br, :, cols] = m_new
                a_ref[g, slot, :, cols] = alpha
                p_ref[g, slot, :, cols] = p.astype(BF16)

    def values(n, slot):
        br, kt = task(n)
        for g in groups:
            acc_ref[g, br] = a_ref[g, slot] * acc_ref[g, br] + pv(g, br, kt, p_ref[g, slot])

    def stage(n, slot):
        scores(n + 2, slot)
        softmax(n + 1, 1 - slot)
        values(n, slot)

    @pl.when(qi >= 1)
    def _():
        scores(0, 0)
        scores(1, 1)
        softmax(0, 0)
        n_full = n_tasks - 1

        def pair(i, carry):
            stage(2 * i, 0)
            stage(2 * i + 1, 1)
            return carry

        lax.fori_loop(0, n_full // 2, pair, 0)

        @pl.when(n_full % 2 == 1)
        def _():
            stage(n_full - 1, 0)
            values(n_full, 1)

        @pl.when(n_full % 2 == 0)
        def _():
            values(n_full, 0)

    for g in groups:
        out = (out_ref[g] + gate_row(g, 1) * (acc_ref[g, 0] / l_ref[g, 0])
               + gate_row(g, 2) * (acc_ref[g, 1] / l_ref[g, 1]))
        for j in range(nh):
            rows = slice((g * nh + j) * dh, (g * nh + j + 1) * dh)
            o_ref[rows, :] = out[:, j * ta:(j + 1) * ta].astype(BF16)


def _attention(qt, kall, vt, selb, oct, gnt, batch, seq):
    ta = ATTN_TILE
    assert WINDOW % ta == 0
    nqt = seq // ta
    ng, nh, dh = N_KV_GROUPS, HEADS_PER_GROUP, HEAD_DIM
    nsel = selb.shape[2]
    qtile = lambda rows: pl.BlockSpec((rows, ta), lambda b, t: (0, b * nqt + t))
    vall = vt.reshape(2, ng, dh, batch * seq)
    wide = nh * ta
    return pl.pallas_call(
        _attn_kernel,
        grid=(batch, nqt),
        in_specs=[qtile(ATTN_WIDTH),
                  pl.BlockSpec((2, seq, ng * LANES), lambda b, t: (0, b, 0)),
                  pl.BlockSpec((2, ng, dh, seq), lambda b, t: (0, 0, 0, b)),
                  pl.BlockSpec((1, ng, nsel, ta), lambda b, t: (b, 0, 0, t)),
                  qtile(ATTN_WIDTH), qtile(ng * GATE_ROWS)],
        out_specs=qtile(ATTN_WIDTH),
        out_shape=jax.ShapeDtypeStruct((ATTN_WIDTH, batch * seq), BF16),
        scratch_shapes=[pltpu.VMEM((ng, LANES, wide), BF16),
                        pltpu.VMEM((ng, 2, ta, wide), F32),
                        pltpu.VMEM((ng, 2, ta, wide), BF16),
                        pltpu.VMEM((ng, 2, 1, wide), F32),
                        pltpu.VMEM((ng, 2, 1, wide), F32),
                        pltpu.VMEM((ng, 2, 1, wide), F32),
                        pltpu.VMEM((ng, 2, 1, wide), F32),
                        pltpu.VMEM((ng, 2, dh, wide), F32),
                        pltpu.VMEM((ng, dh, wide), F32),
                        pltpu.VMEM((ng, 3, ta, wide), F32),
                        pltpu.VMEM((ng, 3, ta, wide), BF16)],
        compiler_params=_params(("parallel", "arbitrary")),
        name="nsa_flash",
    )(qt, kall, vall, selb, oct, gnt)


def _merge_kernel(ot_ref, yc_ref, gm_ref, x_ref, wa_ref, wo_ref, out_ref):
    ya = _dot_tn(ot_ref[...], wa_ref[...])
    gm = gm_ref[...].astype(F32)
    mix = _sigmoid(gm[:, :D_MODEL]) * ya + _sigmoid(gm[:, D_MODEL:]) * yc_ref[...].astype(F32)
    out_ref[...] = x_ref[...] + jnp.dot(mix.astype(BF16), wo_ref[...], preferred_element_type=F32)


def _merge(ot, yc2, gm2, x2, wa, wo):
    t = x2.shape[0]
    tm = ROW_TILE
    row = lambda n: pl.BlockSpec((tm, n), lambda i: (i, 0))
    full = lambda a: pl.BlockSpec(a.shape, lambda i: (0, 0))
    return pl.pallas_call(
        _merge_kernel,
        grid=(t // tm,),
        in_specs=[pl.BlockSpec((ATTN_WIDTH, tm), lambda i: (0, i)), row(D_MODEL), row(2 * D_MODEL), row(D_MODEL),
                  full(wa), full(wo)],
        out_specs=row(D_MODEL),
        out_shape=jax.ShapeDtypeStruct((t, D_MODEL), F32),
        compiler_params=_params(("parallel",)),
        name="merge_outproj",
    )(ot, yc2, gm2, x2, wa, wo)


def _ffn_kernel(x_ref, xh_ref, g_ref, wup_ref, cw_ref, cb_ref, wd_ref, *rest, tiles_per_seq, final_norm):
    if final_norm:
        fg_ref, out_ref, h_ref, a_ref, v_ref, gt_ref = rest
    else:
        out_ref, h_ref, a_ref, v_ref, gt_ref = rest
    tm, halo, tf = FFN_ROW_TILE, FFN_HALO, FFN_COL_TILE
    seq_start = pl.program_id(0) % tiles_per_seq == 0
    hh = _rms(xh_ref[...], g_ref[...])
    h_ref[0:halo, :] = jnp.where(seq_start, 0.0, hh).astype(BF16)
    h_ref[halo:, :] = _rms(x_ref[...], g_ref[...]).astype(BF16)
    h = h_ref[...]

    def conv(ref, w, b):
        out = b
        for k in range(FFN_CONV_WIDTH):
            off = halo - (FFN_CONV_WIDTH - 1) + k
            out = out + w[k:k + 1, :] * ref[off:off + tm, :]
        return out

    for j in range(FFN_DIM // tf):
        slot = j % a_ref.shape[0]
        ca_cols = slice(j * tf, (j + 1) * tf)
        cv_cols = slice(FFN_DIM + j * tf, FFN_DIM + (j + 1) * tf)
        a_ref[slot] = jnp.dot(h, wup_ref[:, ca_cols], preferred_element_type=F32)
        v_ref[slot] = jnp.dot(h, wup_ref[:, cv_cols], preferred_element_type=F32)
        ca = conv(a_ref.at[slot], cw_ref[:, ca_cols], cb_ref[:, ca_cols])
        cv = conv(v_ref.at[slot], cw_ref[:, cv_cols], cb_ref[:, cv_cols])
        gt_ref[:, ca_cols] = (ca * _sigmoid(ca) * cv).astype(BF16)
    y = x_ref[...] + jnp.dot(gt_ref[...], wd_ref[...], preferred_element_type=F32)
    out_ref[...] = _rms(y, fg_ref[...]) if final_norm else y


def _ffn(x2, g, wup, cw, cb, wd, seq, final_g=None):
    t = x2.shape[0]
    tm, halo, tf = FFN_ROW_TILE, FFN_HALO, FFN_COL_TILE
    assert FFN_DIM % tf == 0
    slots = min(2, FFN_DIM // tf)
    once = lambda a: pl.BlockSpec(a.shape, lambda i: (0,) * a.ndim, pipeline_mode=pl.Buffered(1))
    args = [x2, x2, g, wup, cw, cb, wd] + ([final_g] if final_g is not None else [])
    return pl.pallas_call(
        functools.partial(_ffn_kernel, tiles_per_seq=seq // tm, final_norm=final_g is not None),
        grid=(t // tm,),
        in_specs=[pl.BlockSpec((tm, D_MODEL), lambda i: (i, 0)),
                  pl.BlockSpec((halo, D_MODEL), lambda i: (jnp.maximum(i * (tm // halo) - 1, 0), 0))]
                 + [once(a) for a in args[2:]],
        out_specs=pl.BlockSpec((tm, D_MODEL), lambda i: (i, 0)),
        out_shape=jax.ShapeDtypeStruct((t, D_MODEL), F32),
        scratch_shapes=[pltpu.VMEM((tm + halo, D_MODEL), BF16), pltpu.VMEM((slots, tm + halo, tf), F32),
                        pltpu.VMEM((slots, tm + halo, tf), F32), pltpu.VMEM((tm, FFN_DIM), BF16)],
        compiler_params=_params(("parallel",)),
        name="conv_ffn",
    )(*args)


def _sel_map_t(seq):
    ncmp = (seq - CMP_BLOCK) // CMP_STRIDE + 1
    nr = seq // CMP_STRIDE
    nsel = seq // SEL_BLOCK
    cs = np.arange(ncmp) * CMP_STRIDE
    ce = cs + CMP_BLOCK - 1
    ss = np.arange(nsel) * SEL_BLOCK
    se = ss + SEL_BLOCK - 1
    ov = np.minimum(ce[:, None], se[None, :]) - np.maximum(cs[:, None], ss[None, :]) + 1
    m = np.zeros((nsel, nr), np.float32)
    m[:, :ncmp] = (np.clip(ov, 0, None).astype(np.float32) / CMP_BLOCK).T
    return jnp.asarray(m)


def _key_aug(seq):
    pos = np.arange(seq)
    nsel = seq // SEL_BLOCK
    a = np.zeros((seq, AUG), np.float32)
    a[:, AUG_POS + 0] = pos // SEL_BLOCK
    a[:, AUG_POS + 1] = pos % SEL_BLOCK
    a[:, AUG_POS + 2] = 1.0
    a[:, AUG_POS + 3] = 1.0
    s = a.copy()
    s[pos, pos // SEL_BLOCK] = 1.0
    assert nsel <= AUG_POS
    z = np.zeros((seq, HEAD_DIM), np.float32)
    return jnp.asarray(np.concatenate([z, s, z, s, z, a, z, a], axis=1))


def _layer(x2, batch, seq, p, final_g):
    kall, kc, vc, yc, gm2, qt, vt, gnt = _inproj(x2, p["norm1_g"], p["wn"], p["wt"], _key_aug(seq), p["dww"],
                                                 p["dwb"], p["lng"], p["lnb"], p["wconv"], seq)
    oct, selb = _cmp_branch(kc, vc, p["pek"], p["pev"], p["w1k"], p["w2k"], p["w1v"], p["w2v"], qt,
                            _sel_map_t(seq), batch, seq)
    ot = _attention(qt, kall, vt, selb, oct, gnt, batch, seq)
    x2 = _merge(ot, yc, gm2, x2, p["wattn"], p["wout"])
    return _ffn(x2, p["norm2_g"], p["wup"], p["cw"], p["cb"], p["wd"], seq, final_g)


def _prep_layer(l, norm1_g, w_in, cmp_pe_k, cmp_pe_v, cmp_k_w1, cmp_k_w2, cmp_v_w1, cmp_v_w2, w_attn_br,
                conv_dw_w, conv_dw_b, conv_ln_g, conv_ln_b, w_conv_br, w_out, norm2_g, ffn_w_up, ffn_dw_w,
                ffn_dw_b, ffn_w_down):
    w = w_in[l].astype(BF16)
    kvw = KV_WIDTH
    c_q = ATTN_WIDTH
    c_kc, c_vc, c_ks, c_vs, c_kw, c_vw = (c_q + i * kvw for i in range(6))
    c_gn = c_q + 6 * kvw
    c_uc = c_gn + N_GATES
    c_gm = c_uc + 2 * CONV_CH

    def widen(cols):
        k = cols.reshape(D_MODEL, N_KV_GROUPS, HEAD_DIM)
        return jnp.pad(k, ((0, 0), (0, 0), (0, AUG))).reshape(D_MODEL, N_KV_GROUPS * LANES)

    wn = jnp.concatenate([widen(w[:, c_ks:c_ks + kvw]), widen(w[:, c_kw:c_kw + kvw]),
                          w[:, c_kc:c_kc + 2 * kvw], w[:, c_uc:c_gm], w[:, c_gm:]], axis=1)
    gates = w[:, c_gn:c_uc].reshape(D_MODEL, N_KV_GROUPS, 3 * HEADS_PER_GROUP)
    gates = jnp.pad(gates, ((0, 0), (0, 0), (0, GATE_ROWS - 3 * HEADS_PER_GROUP)))
    wt = jnp.concatenate([w[:, :c_q], w[:, c_vs:c_vs + kvw], w[:, c_vw:c_vw + kvw],
                          gates.reshape(D_MODEL, N_KV_GROUPS * GATE_ROWS)], axis=1).T

    def per_group(a):
        z = jnp.zeros_like(a)
        return jnp.concatenate([jnp.concatenate([a, z], axis=-1), jnp.concatenate([z, a], axis=-1)], axis=-2)

    assert N_KV_GROUPS == 2
    w1 = lambda a: per_group(a.astype(BF16).reshape(CMP_BLOCK, HEAD_DIM, CMP_HIDDEN))
    pe = lambda a: jnp.concatenate([a] * N_KV_GROUPS, axis=1)
    return dict(
        norm1_g=norm1_g[l][None, :], wn=wn, wt=wt,
        pek=pe(cmp_pe_k[l]), pev=pe(cmp_pe_v[l]),
        w1k=w1(cmp_k_w1[l]), w2k=per_group(cmp_k_w2[l].astype(BF16)),
        w1v=w1(cmp_v_w1[l]), w2v=per_group(cmp_v_w2[l].astype(BF16)),
        wattn=w_attn_br[l].astype(BF16),
        dww=conv_dw_w[l], dwb=conv_dw_b[l][None, :], lng=conv_ln_g[l][None, :], lnb=conv_ln_b[l][None, :],
        wconv=w_conv_br[l].astype(BF16), wout=w_out[l].astype(BF16),
        norm2_g=norm2_g[l][None, :],
        wup=ffn_w_up[l].astype(BF16), cw=ffn_dw_w[l], cb=ffn_dw_b[l][None, :],
        wd=ffn_w_down[l].astype(BF16),
    )


def kernel(x, norm1_g, w_in, cmp_pe_k, cmp_pe_v, cmp_k_w1, cmp_k_w2, cmp_v_w1, cmp_v_w2, w_attn_br, conv_dw_w, conv_dw_b, conv_ln_g, conv_ln_b, w_conv_br, w_out, norm2_g, ffn_w_up, ffn_dw_w, ffn_dw_b, ffn_w_down, final_g):
    batch, seq, d = x.shape
    assert d == D_MODEL and seq % ROW_TILE == 0 and seq % FFN_ROW_TILE == 0 and seq % CMP_Q_TILE == 0
    assert seq // SEL_BLOCK == AUG_POS
    x2 = x.reshape(batch * seq, d)
    for l in range(w_in.shape[0]):
        p = _prep_layer(l, norm1_g, w_in, cmp_pe_k, cmp_pe_v, cmp_k_w1, cmp_k_w2, cmp_v_w1, cmp_v_w2, w_attn_br,
                        conv_dw_w, conv_dw_b, conv_ln_g, conv_ln_b, w_conv_br, w_out, norm2_g, ffn_w_up,
                        ffn_dw_w, ffn_dw_b, ffn_w_down)
        last = l == w_in.shape[0] - 1
        x2 = _layer(x2, batch, seq, p, final_g[None, :] if last else None)
    return x2.reshape(batch, seq, d)
```

```python
import functools
import math

import numpy as np
import jax
import jax.numpy as jnp
from jax import lax
from jax.experimental import pallas as pl
from jax.experimental.pallas import tpu as pltpu

F32 = jnp.float32
BF16 = jnp.bfloat16

D_MODEL = 1024
N_HEADS = 8
HEAD_DIM = 64
N_KV_GROUPS = 2
HEADS_PER_GROUP = N_HEADS // N_KV_GROUPS
ATTN_WIDTH = N_HEADS * HEAD_DIM
KV_WIDTH = N_KV_GROUPS * HEAD_DIM
CMP_BLOCK = 32
CMP_STRIDE = 16
CMP_HIDDEN = 128
SEL_BLOCK = 64
SEL_TOP = 16
SEL_SHIFT = 6
WINDOW = 512
FORCE_BONUS = 1.0e4
NEG_INF = -1.0e30
CONV_CH = 512
CONV_WIDTH = 31
FFN_DIM = 2816
FFN_CONV_WIDTH = 3
NORM_EPS = 1e-6
N_GATES = 3 * N_HEADS
ATTN_SCALE = HEAD_DIM ** -0.5
UNSELECTED = -(2.0 ** 99)

VMEM_LIMIT_BYTES = 56 * 1024 * 1024
LANES = 128

ROW_TILE = 512
FFN_ROW_TILE = 512
ATTN_TILE = 256
SOFTMAX_COLS = 128
CMP_Q_TILE = 512
ANCHOR_LAG = 6
CONV_HALO = 32
FFN_COL_TILE = 2816
FFN_HALO = 8

AUG = LANES - HEAD_DIM
AUG_POS = N_HEADS * 4
GATE_ROWS = 16
Q_ROWS = HEADS_PER_GROUP * HEAD_DIM

assert AUG_POS == 2048 // SEL_BLOCK and AUG_POS + 4 <= AUG


def _params(sem, flags=None):
    return pltpu.CompilerParams(dimension_semantics=sem, vmem_limit_bytes=VMEM_LIMIT_BYTES, flags=flags)


def _rms(x, g):
    y = x * lax.rsqrt(jnp.mean(x * x, axis=-1, keepdims=True) + NORM_EPS)
    return y * g


def _sigmoid(x):
    return 1.0 / (1.0 + jnp.exp(-x))


def _dot_nt(a, b, **kw):
    return lax.dot_general(a, b, (((1,), (1,)), ((), ())), preferred_element_type=F32, **kw)


def _dot_tn(a, b):
    return lax.dot_general(a, b, (((0,), (0,)), ((), ())), preferred_element_type=F32)


def _zero_after(x):
    u = pltpu.bitcast(x.astype(F32), jnp.uint32)
    return pltpu.bitcast(lax.shift_right_logical(lax.shift_right_logical(u, jnp.uint32(16)), jnp.uint32(16)), F32)


def _conformer_rows(z_ref, dww_ref, dwb_ref, lng_ref, lnb_ref, wp_ref, r0, ts, between):
    base = CONV_HALO - (CONV_WIDTH - 1)
    sub = 8
    acc = jnp.zeros((ts, CONV_CH), F32) + dwb_ref[...]
    for r in range(sub):
        zero = between()
        part = None
        ext = ts + (sub if r else 0)
        for s in range(r, base + CONV_WIDTH, sub):
            if s < base:
                continue
            w = dww_ref[s - base:s - base + 1, :]
            if part is None and zero is not None:
                w = w + zero
            term = w * z_ref[r0 + s - r:r0 + s - r + ext, :]
            part = term if part is None else part + term
        acc = acc + part[r:r + ts, :]
    mu = jnp.mean(acc, axis=-1, keepdims=True)
    cen = acc - mu
    var = jnp.mean(cen * cen, axis=-1, keepdims=True)
    y = cen * lax.rsqrt(var + NORM_EPS) * lng_ref[...] + lnb_ref[...]
    y = y * _sigmoid(y)
    return jnp.dot(y.astype(BF16), wp_ref[...], preferred_element_type=F32)


def _inproj_kernel(x_ref, g_ref, wn_ref, wt_ref, aug_ref, dww_ref, dwb_ref, lng_ref, lnb_ref, wp_ref,
                   kall_ref, kc_ref, vc_ref, yc_ref, gm_ref, qt_ref, vt_ref, gnt_ref, z_ref, zprev_ref,
                   *, tiles_per_seq):
    tm, halo = ROW_TILE, CONV_HALO
    h = _rms(x_ref[...], g_ref[...]).astype(BF16)
    kw, kvw = N_KV_GROUPS * LANES, KV_WIDTH
    c_kc = 2 * kw
    c_uc = c_kc + 2 * kvw
    c_gm = c_uc + 2 * CONV_CH
    proj = lambda lo, hi: jnp.dot(h, wn_ref[:, lo:hi], preferred_element_type=F32)

    u = proj(c_uc, c_gm)
    seq_start = pl.program_id(0) % tiles_per_seq == 0
    z_ref[0:halo, :] = jnp.where(seq_start, 0.0, zprev_ref[...])
    z_ref[halo:, :] = u[:, :CONV_CH] * _sigmoid(u[:, CONV_CH:])
    zprev_ref[...] = z_ref[tm:tm + halo, :]

    step = 2 * LANES
    nq, nv = qt_ref.shape[0], vt_ref.shape[0]

    def key_piece(br):
        def run():
            y = proj(br * kw, (br + 1) * kw)
            kall_ref[br] = (y + aug_ref[:, br * kw:(br + 1) * kw]).astype(BF16)
            return y
        return run

    def cmp_piece():
        kc_ref[...] = proj(c_kc, c_kc + kvw)
        y = proj(c_kc + kvw, c_uc)
        vc_ref[...] = y
        return y

    def gate_piece(c):
        def run():
            y = proj(c_gm + c, c_gm + c + step)
            gm_ref[:, c:c + step] = y.astype(BF16)
            return y
        return run

    def t_piece(lo, hi):
        def run():
            t = _dot_nt(wt_ref[lo:hi, :], h)
            if hi <= nq:
                qt_ref[lo:hi, :] = (t * ATTN_SCALE).astype(BF16)
            elif hi <= nq + nv:
                vt_ref[lo - nq:hi - nq, :] = t.astype(BF16)
            else:
                gnt_ref[...] = t
            return t
        return run

    pieces = ([key_piece(0), key_piece(1), cmp_piece] + [gate_piece(c) for c in range(0, 2 * D_MODEL, step)]
              + [t_piece(lo, lo + step) for lo in range(0, nq + nv, step)] + [t_piece(nq + nv, wt_ref.shape[0])])

    n_quarters, groups_per_quarter = 4, 8
    rq = tm // n_quarters
    slots = n_quarters * groups_per_quarter
    emitted = [0, 0]

    pending = []

    def between():
        if emitted[1] * slots < emitted[0] * len(pieces):
            done = pieces[emitted[1]]()
            emitted[1] += 1
            row = _zero_after(done[-8:, -LANES:])[0:1, :]
            pending.append((emitted[0] + ANCHOR_LAG, jnp.concatenate([row] * (CONV_CH // LANES), axis=1)))
        emitted[0] += 1
        zero = None
        while pending and (pending[0][0] < emitted[0] or emitted[0] == slots):
            z = pending.pop(0)[1]
            zero = z if zero is None else zero + z
        return zero

    for k in range(n_quarters):
        yq = _conformer_rows(z_ref, dww_ref, dwb_ref, lng_ref, lnb_ref, wp_ref, k * rq, rq, between)
        yc_ref[k * rq:(k + 1) * rq, :] = yq.astype(BF16)
    assert emitted[1] == len(pieces)


def _inproj(x2, g, wn, wt, aug, dww, dwb, lng, lnb, wp, seq):
    t = x2.shape[0]
    tm = ROW_TILE
    row = lambda n: pl.BlockSpec((tm, n), lambda i: (i, 0))
    col = lambda n: pl.BlockSpec((n, tm), lambda i: (0, i))
    full = lambda a: pl.BlockSpec(a.shape, lambda i: (0, 0))
    kw = N_KV_GROUPS * LANES
    nat = ((KV_WIDTH, F32), (KV_WIDTH, F32), (D_MODEL, BF16), (2 * D_MODEL, BF16))
    trn = ((ATTN_WIDTH, BF16), (2 * KV_WIDTH, BF16), (N_KV_GROUPS * GATE_ROWS, F32))
    return pl.pallas_call(
        functools.partial(_inproj_kernel, tiles_per_seq=seq // tm),
        grid=(t // tm,),
        in_specs=[row(D_MODEL), full(g), full(wn), full(wt),
                  pl.BlockSpec((tm, 2 * kw), lambda i: (i % (seq // tm), 0)),
                  full(dww), full(dwb), full(lng), full(lnb), full(wp)],
        out_specs=[pl.BlockSpec((2, tm, kw), lambda i: (0, i, 0))] + [row(n) for n, _ in nat]
                  + [col(n) for n, _ in trn],
        out_shape=[jax.ShapeDtypeStruct((2, t, kw), BF16)] + [jax.ShapeDtypeStruct((t, n), d) for n, d in nat]
                  + [jax.ShapeDtypeStruct((n, t), d) for n, d in trn],
        scratch_shapes=[pltpu.VMEM((tm + CONV_HALO, CONV_CH), F32), pltpu.VMEM((CONV_HALO, CONV_CH), F32)],
        compiler_params=_params(("arbitrary",)),
        name="inproj",
    )(x2, g, wn, wt, aug, dww, dwb, lng, lnb, wp)


def _gelu_tanh(x):
    return 0.5 * x * (1.0 + jnp.tanh(math.sqrt(2.0 / math.pi) * (x + 0.044715 * (x * x * x))))


def _compress(x_ref, pe_ref, w1_ref, w2_ref, nr):
    half = CMP_BLOCK // 2
    a = b = None
    for l in range(half):
        x = x_ref[pl.ds(l, nr, stride=CMP_STRIDE), :]
        ta = jnp.dot((x + pe_ref[l:l + 1, :]).astype(BF16), w1_ref[l], preferred_element_type=F32)
        tb = jnp.dot((x + pe_ref[half + l:half + l + 1, :]).astype(BF16), w1_ref[half + l],
                     preferred_element_type=F32)
        a = ta if a is None else a + ta
        b = tb if b is None else b + tb
    pre = a + pltpu.roll(b, nr - 1, 0)
    hid = _gelu_tanh(pre).astype(BF16)
    return jnp.dot(hid, w2_ref[...], preferred_element_type=F32)


def _cmp_kernel(kc_ref, vc_ref, pek_ref, pev_ref, w1k_ref, w2k_ref, w1v_ref, w2v_ref,
                qt_ref, mapt_ref, oct_ref, selb_ref, *, seq):
    ng, nh, dh = N_KV_GROUPS, HEADS_PER_GROUP, HEAD_DIM
    nr = seq // CMP_STRIDE
    kc = _compress(kc_ref, pek_ref, w1k_ref, w2k_ref, nr).astype(BF16)
    vct = jnp.transpose(_compress(vc_ref, pev_ref, w1v_ref, w2v_ref, nr)).astype(BF16)
    nsel = seq // SEL_BLOCK
    tq = CMP_Q_TILE
    c_end = lax.broadcasted_iota(jnp.int32, (nr, 1), 0) * CMP_STRIDE + (CMP_BLOCK - 1)
    blk = lax.broadcasted_iota(jnp.int32, (nsel, 1), 0)
    zeros_q = jnp.zeros((dh, tq), BF16)

    n_top = min(SEL_TOP, nsel)
    for ti in range(seq // tq):
        t0, t1 = ti * tq, (ti + 1) * tq
        nrv, nbv = t1 // CMP_STRIDE, t1 // SEL_BLOCK
        pos = t0 + lax.broadcasted_iota(jnp.int32, (1, tq), 1)
        dist = pos - c_end[:nrv]
        valid = dist >= 0
        distf = dist.astype(F32)
        cur = pos >> SEL_SHIFT
        blk_v = blk[:nbv]
        forced = (blk_v == 0) | (blk_v == cur) | (blk_v == cur - 1)
        causal = blk_v <= cur
        for g in range(ng):
            psum = jnp.zeros((nrv, tq), F32)
            for j in range(nh):
                slope = 2.0 ** -(g * nh + j + 1)
                rows = slice((g * nh + j) * dh, (g * nh + j + 1) * dh)
                q = qt_ref[rows, t0:t1]
                q2 = jnp.concatenate([q, zeros_q] if g == 0 else [zeros_q, q], axis=0)
                s = jnp.dot(kc[:nrv], q2, preferred_element_type=F32)
                s = jnp.where(valid, s - slope * distf, NEG_INF)
                m = jnp.max(s, axis=0, keepdims=True)
                e = jnp.where(valid, jnp.exp(s - m), 0.0)
                l = jnp.sum(e, axis=0, keepdims=True)
                p = e / jnp.where(l > 0.0, l, 1.0)
                psum = psum + p
                pfull = p.astype(BF16)
                if nrv < nr:
                    pfull = jnp.concatenate([pfull, jnp.zeros((nr - nrv, tq), BF16)], axis=0)
                oct_ref[rows, t0:t1] = jnp.dot(vct[g * dh:(g + 1) * dh, :], pfull, preferred_element_type=F32)
            if nrv < nr:
                psum = jnp.concatenate([psum, jnp.zeros((nr - nrv, tq), F32)], axis=0)
            imp = jnp.dot(mapt_ref[0:nbv, :], psum, precision=lax.Precision.HIGHEST, preferred_element_type=F32)
            val = jnp.where(causal, imp + jnp.where(forced, FORCE_BONUS, 0.0), NEG_INF)
            sub = 8
            cnts = []
            for c0 in range(0, nbv, sub):
                vc_ = val[c0:c0 + sub, :]
                cnt = jnp.zeros((sub, tq), F32)
                for i in range(nbv):
                    vi = val[i:i + 1, :]
                    if i < c0:
                        cnt = cnt + jnp.where(vi >= vc_, 1.0, 0.0)
                    elif i >= c0 + sub:
                        cnt = cnt + jnp.where(vi > vc_, 1.0, 0.0)
                    else:
                        tie = (blk_v[c0:c0 + sub] > i).astype(F32)
                        cnt = cnt + jnp.where(vi > vc_, 1.0, jnp.where(vi == vc_, tie, 0.0))
                cnts.append(cnt)
            cnt = jnp.concatenate(cnts, axis=0) if len(cnts) > 1 else cnts[0]
            selb = jnp.where(causal & (cnt < n_top), 0.0, UNSELECTED)
            if nbv < nsel:
                selb = jnp.concatenate([selb, jnp.full((nsel - nbv, tq), UNSELECTED, F32)], axis=0)
            selb_ref[0, g, :, t0:t1] = selb.astype(BF16)


def _cmp_branch(kc, vc, pek, pev, w1k, w2k, w1v, w2v, qt, mapt, batch, seq):
    nsel = seq // SEL_BLOCK
    full = lambda a: pl.BlockSpec(a.shape, lambda i: (0,) * a.ndim)
    rows = pl.BlockSpec((seq, KV_WIDTH), lambda i: (i, 0))
    heads = pl.BlockSpec((ATTN_WIDTH, seq), lambda i: (0, i))
    return pl.pallas_call(
        functools.partial(_cmp_kernel, seq=seq),
        grid=(batch,),
        in_specs=[rows, rows, full(pek), full(pev), full(w1k), full(w2k), full(w1v), full(w2v), heads, full(mapt)],
        out_specs=[heads, pl.BlockSpec((1, N_KV_GROUPS, nsel, seq), lambda i: (i, 0, 0, 0))],
        out_shape=[jax.ShapeDtypeStruct((ATTN_WIDTH, batch * seq), F32),
                   jax.ShapeDtypeStruct((batch, N_KV_GROUPS, nsel, seq), BF16)],
        compiler_params=_params(("parallel",)),
        name="cmp_topk",
    )(kc, vc, pek, pev, w1k, w2k, w1v, w2v, qt, mapt)


def _attn_kernel(qt_ref, k_ref, v_ref, selb_ref, oct_ref, gnt_ref, o_ref,
                 qa_ref, s_ref, c_ref, m_ref, l_ref, acc_ref, out_ref):
    qi = pl.program_id(1)
    ta = ATTN_TILE
    ng, nh, dh = N_KV_GROUPS, HEADS_PER_GROUP, HEAD_DIM
    nsel = selb_ref.shape[2]
    groups = range(ng)
    pos = qi * ta + lax.broadcasted_iota(jnp.int32, (1, ta), 1)
    tb = (pos >> SEL_SHIFT).astype(F32)
    tr = (pos & (SEL_BLOCK - 1)).astype(F32)
    arow = lax.broadcasted_iota(jnp.int32, (16, 1), 0)
    gate = _sigmoid(gnt_ref[...])

    def gate_row(g, c):
        r = g * GATE_ROWS + c
        return jnp.concatenate([gate[r + 3 * j:r + 3 * j + 1, :] for j in range(nh)], axis=1)

    for g in groups:
        for j in range(nh):
            slope = 2.0 ** -(g * nh + j + 1)
            cols = slice(j * ta, (j + 1) * ta)
            rows = slice((g * nh + j) * dh, (g * nh + j + 1) * dh)
            qa_ref[g, 0:dh, cols] = qt_ref[rows, :]
            qa_ref[g, dh:dh + nsel, cols] = selb_ref[0, g]
            alibi = jnp.where(arow == 0, slope * SEL_BLOCK,
                              jnp.where(arow == 1, slope,
                                        jnp.where(arow == 2, -slope * SEL_BLOCK * tb,
                                                  jnp.where(arow == 3, -slope * tr, 0.0))))
            qa_ref[g, dh + nsel:dh + nsel + 16, cols] = alibi.astype(BF16)
            qa_ref[g, dh + nsel + 16:, cols] = jnp.zeros((LANES - dh - nsel - 16, ta), BF16)
            out_ref[g, :, cols] = gate[g * GATE_ROWS + 3 * j:g * GATE_ROWS + 3 * j + 1, :] * oct_ref[rows, :]

    krow = lax.broadcasted_iota(jnp.int32, (ta, 1), 0)
    qcol = lax.broadcasted_iota(jnp.int32, (1, ta), 1)
    lanes4 = lambda b: jnp.concatenate([b] * nh, axis=1)

    def qk(g, br, kt):
        k = k_ref[br, pl.ds(pl.multiple_of(kt * ta, ta), ta), g * LANES:(g + 1) * LANES]
        return jnp.dot(k, qa_ref[g], preferred_element_type=F32)

    def pv(g, br, kt, p):
        v = v_ref[br, g, :, pl.ds(pl.multiple_of(kt * ta, ta), ta)]
        return jnp.dot(v, p.astype(BF16), preferred_element_type=F32)

    far = WINDOW // ta
    causal = lanes4(jnp.where(qcol >= krow, 0.0, NEG_INF))
    kt_far = jnp.maximum(qi - far, 0)
    beyond = lanes4(jnp.where((qcol < krow) & (qi >= far), 0.0, NEG_INF))
    for g in groups:
        s_sel = qk(g, 0, qi) + causal
        m_sel = jnp.max(s_sel, axis=0, keepdims=True)
        p_sel = jnp.exp(s_sel - m_sel)
        m_ref[g, 0] = m_sel
        l_ref[g, 0] = jnp.sum(p_sel, axis=0, keepdims=True)
        acc_ref[g, 0] = pv(g, 0, qi, p_sel)
        s_win = qk(g, 1, qi) + causal
        s_far = qk(g, 1, kt_far) + beyond
        m_win = jnp.maximum(jnp.max(s_win, axis=0, keepdims=True), jnp.max(s_far, axis=0, keepdims=True))
        p_win = jnp.exp(s_win - m_win)
        p_far = jnp.exp(s_far - m_win)
        m_ref[g, 1] = m_win
        l_ref[g, 1] = jnp.sum(p_win, axis=0, keepdims=True) + jnp.sum(p_far, axis=0, keepdims=True)
        acc_ref[g, 1] = pv(g, 1, qi, p_win) + pv(g, 1, kt_far, p_far)

    n_tasks = qi + 1

    def task(n):
        n = jnp.minimum(n, n_tasks - 1)
        return jnp.where(n == 0, 1, 0), qi - jnp.maximum(n, 1)

    def scores(n, slot):
        br, kt = task(n)
        for g in groups:
            s = qk(g, br, kt)
            s_ref[g, slot] = s
            c_ref[g, slot] = jnp.max(s, axis=0, keepdims=True)

    def absorb(n, slot):
        br, kt = task(n)
        for c in range(nh):
            cols = slice(c * ta, (c + 1) * ta)
            for g in groups:
                m_old = m_ref[g, br, :, cols]
                m_new = jnp.maximum(m_old, c_ref[g, slot, :, cols])
                alpha = jnp.exp(m_old - m_new)
                p = jnp.exp(s_ref[g, slot, :, cols] - m_new)
                l_ref[g, br, :, cols] = alpha * l_ref[g, br, :, cols] + jnp.sum(p, axis=0, keepdims=True)
                m_ref[g, br, :, cols] = m_new
                acc_ref[g, br, :, cols] = alpha * acc_ref[g, br, :, cols] + pv(g, br, kt, p)

    @pl.when(qi >= 1)
    def _():
        scores(0, 0)

        def pair(i, carry):
            n = 2 * i
            scores(n + 1, 1)
            absorb(n, 0)
            scores(n + 2, 0)
            absorb(n + 1, 1)
            return carry

        lax.fori_loop(0, n_tasks // 2, pair, 0)

        @pl.when(n_tasks % 2 == 1)
        def _():
            absorb(n_tasks - 1, 0)

    for g in groups:
        out = (out_ref[g] + gate_row(g, 1) * (acc_ref[g, 0] / l_ref[g, 0])
               + gate_row(g, 2) * (acc_ref[g, 1] / l_ref[g, 1]))
        for j in range(nh):
            rows = slice((g * nh + j) * dh, (g * nh + j + 1) * dh)
            o_ref[rows, :] = out[:, j * ta:(j + 1) * ta].astype(BF16)


def _attention(qt, kall, vt, selb, oct, gnt, batch, seq):
    ta = ATTN_TILE
    assert WINDOW % ta == 0
    nqt = seq // ta
    ng, nh, dh = N_KV_GROUPS, HEADS_PER_GROUP, HEAD_DIM
    nsel = selb.shape[2]
    qtile = lambda rows: pl.BlockSpec((rows, ta), lambda b, t: (0, b * nqt + t))
    vall = vt.reshape(2, ng, dh, batch * seq)
    wide = nh * ta
    return pl.pallas_call(
        _attn_kernel,
        grid=(batch, nqt),
        in_specs=[qtile(ATTN_WIDTH),
                  pl.BlockSpec((2, seq, ng * LANES), lambda b, t: (0, b, 0)),
                  pl.BlockSpec((2, ng, dh, seq), lambda b, t: (0, 0, 0, b)),
                  pl.BlockSpec((1, ng, nsel, ta), lambda b, t: (b, 0, 0, t)),
                  qtile(ATTN_WIDTH), qtile(ng * GATE_ROWS)],
        out_specs=qtile(ATTN_WIDTH),
        out_shape=jax.ShapeDtypeStruct((ATTN_WIDTH, batch * seq), BF16),
        scratch_shapes=[pltpu.VMEM((ng, LANES, wide), BF16),
                        pltpu.VMEM((ng, 2, ta, wide), F32),
                        pltpu.VMEM((ng, 2, 1, wide), F32),
                        pltpu.VMEM((ng, 2, 1, wide), F32),
                        pltpu.VMEM((ng, 2, 1, wide), F32),
                        pltpu.VMEM((ng, 2, dh, wide), F32),
                        pltpu.VMEM((ng, dh, wide), F32)],
        compiler_params=_params(("parallel", "arbitrary")),
        name="nsa_flash",
    )(qt, kall, vall, selb, oct, gnt)


def _merge_kernel(ot_ref, yc_ref, gm_ref, x_ref, wa_ref, wo_ref, out_ref):
    ya = _dot_tn(ot_ref[...], wa_ref[...])
    gm = gm_ref[...].astype(F32)
    mix = _sigmoid(gm[:, :D_MODEL]) * ya + _sigmoid(gm[:, D_MODEL:]) * yc_ref[...].astype(F32)
    out_ref[...] = x_ref[...] + jnp.dot(mix.astype(BF16), wo_ref[...], preferred_element_type=F32)


def _merge(ot, yc2, gm2, x2, wa, wo):
    t = x2.shape[0]
    tm = ROW_TILE
    row = lambda n: pl.BlockSpec((tm, n), lambda i: (i, 0))
    full = lambda a: pl.BlockSpec(a.shape, lambda i: (0, 0))
    return pl.pallas_call(
        _merge_kernel,
        grid=(t // tm,),
        in_specs=[pl.BlockSpec((ATTN_WIDTH, tm), lambda i: (0, i)), row(D_MODEL), row(2 * D_MODEL), row(D_MODEL),
                  full(wa), full(wo)],
        out_specs=row(D_MODEL),
        out_shape=jax.ShapeDtypeStruct((t, D_MODEL), F32),
        compiler_params=_params(("parallel",)),
        name="merge_outproj",
    )(ot, yc2, gm2, x2, wa, wo)


def _ffn_kernel(x_ref, xh_ref, g_ref, wup_ref, cw_ref, cb_ref, wd_ref, *rest, tiles_per_seq, final_norm):
    if final_norm:
        fg_ref, out_ref, h_ref, a_ref, v_ref, gt_ref = rest
    else:
        out_ref, h_ref, a_ref, v_ref, gt_ref = rest
    tm, halo, tf = FFN_ROW_TILE, FFN_HALO, FFN_COL_TILE
    seq_start = pl.program_id(0) % tiles_per_seq == 0
    hh = _rms(xh_ref[...], g_ref[...])
    h_ref[0:halo, :] = jnp.where(seq_start, 0.0, hh).astype(BF16)
    h_ref[halo:, :] = _rms(x_ref[...], g_ref[...]).astype(BF16)
    h = h_ref[...]

    def conv(ref, w, b):
        out = b
        for k in range(FFN_CONV_WIDTH):
            off = halo - (FFN_CONV_WIDTH - 1) + k
            out = out + w[k:k + 1, :] * ref[off:off + tm, :]
        return out

    for j in range(FFN_DIM // tf):
        slot = j % a_ref.shape[0]
        ca_cols = slice(j * tf, (j + 1) * tf)
        cv_cols = slice(FFN_DIM + j * tf, FFN_DIM + (j + 1) * tf)
        a_ref[slot] = jnp.dot(h, wup_ref[:, ca_cols], preferred_element_type=F32)
        v_ref[slot] = jnp.dot(h, wup_ref[:, cv_cols], preferred_element_type=F32)
        ca = conv(a_ref.at[slot], cw_ref[:, ca_cols], cb_ref[:, ca_cols])
        cv = conv(v_ref.at[slot], cw_ref[:, cv_cols], cb_ref[:, cv_cols])
        gt_ref[:, ca_cols] = (ca * _sigmoid(ca) * cv).astype(BF16)
    y = x_ref[...] + jnp.dot(gt_ref[...], wd_ref[...], preferred_element_type=F32)
    out_ref[...] = _rms(y, fg_ref[...]) if final_norm else y


def _ffn(x2, g, wup, cw, cb, wd, seq, final_g=None):
    t = x2.shape[0]
    tm, halo, tf = FFN_ROW_TILE, FFN_HALO, FFN_COL_TILE
    assert FFN_DIM % tf == 0
    slots = min(2, FFN_DIM // tf)
    once = lambda a: pl.BlockSpec(a.shape, lambda i: (0,) * a.ndim, pipeline_mode=pl.Buffered(1))
    args = [x2, x2, g, wup, cw, cb, wd] + ([final_g] if final_g is not None else [])
    return pl.pallas_call(
        functools.partial(_ffn_kernel, tiles_per_seq=seq // tm, final_norm=final_g is not None),
        grid=(t // tm,),
        in_specs=[pl.BlockSpec((tm, D_MODEL), lambda i: (i, 0)),
                  pl.BlockSpec((halo, D_MODEL), lambda i: (jnp.maximum(i * (tm // halo) - 1, 0), 0))]
                 + [once(a) for a in args[2:]],
        out_specs=pl.BlockSpec((tm, D_MODEL), lambda i: (i, 0)),
        out_shape=jax.ShapeDtypeStruct((t, D_MODEL), F32),
        scratch_shapes=[pltpu.VMEM((tm + halo, D_MODEL), BF16), pltpu.VMEM((slots, tm + halo, tf), F32),
                        pltpu.VMEM((slots, tm + halo, tf), F32), pltpu.VMEM((tm, FFN_DIM), BF16)],
        compiler_params=_params(("parallel",)),
        name="conv_ffn",
    )(*args)


def _sel_map_t(seq):
    ncmp = (seq - CMP_BLOCK) // CMP_STRIDE + 1
    nr = seq // CMP_STRIDE
    nsel = seq // SEL_BLOCK
    cs = np.arange(ncmp) * CMP_STRIDE
    ce = cs + CMP_BLOCK - 1
    ss = np.arange(nsel) * SEL_BLOCK
    se = ss + SEL_BLOCK - 1
    ov = np.minimum(ce[:, None], se[None, :]) - np.maximum(cs[:, None], ss[None, :]) + 1
    m = np.zeros((nsel, nr), np.float32)
    m[:, :ncmp] = (np.clip(ov, 0, None).astype(np.float32) / CMP_BLOCK).T
    return jnp.asarray(m)


def _key_aug(seq):
    pos = np.arange(seq)
    nsel = seq // SEL_BLOCK
    a = np.zeros((seq, AUG), np.float32)
    a[:, AUG_POS + 0] = pos // SEL_BLOCK
    a[:, AUG_POS + 1] = pos % SEL_BLOCK
    a[:, AUG_POS + 2] = 1.0
    a[:, AUG_POS + 3] = 1.0
    s = a.copy()
    s[pos, pos // SEL_BLOCK] = 1.0
    assert nsel <= AUG_POS
    z = np.zeros((seq, HEAD_DIM), np.float32)
    return jnp.asarray(np.concatenate([z, s, z, s, z, a, z, a], axis=1))


def _layer(x2, batch, seq, p, final_g):
    kall, kc, vc, yc, gm2, qt, vt, gnt = _inproj(x2, p["norm1_g"], p["wn"], p["wt"], _key_aug(seq), p["dww"],
                                                 p["dwb"], p["lng"], p["lnb"], p["wconv"], seq)
    oct, selb = _cmp_branch(kc, vc, p["pek"], p["pev"], p["w1k"], p["w2k"], p["w1v"], p["w2v"], qt,
                            _sel_map_t(seq), batch, seq)
    ot = _attention(qt, kall, vt, selb, oct, gnt, batch, seq)
    x2 = _merge(ot, yc, gm2, x2, p["wattn"], p["wout"])
    return _ffn(x2, p["norm2_g"], p["wup"], p["cw"], p["cb"], p["wd"], seq, final_g)


def _prep_layer(l, norm1_g, w_in, cmp_pe_k, cmp_pe_v, cmp_k_w1, cmp_k_w2, cmp_v_w1, cmp_v_w2, w_attn_br,
                conv_dw_w, conv_dw_b, conv_ln_g, conv_ln_b, w_conv_br, w_out, norm2_g, ffn_w_up, ffn_dw_w,
                ffn_dw_b, ffn_w_down):
    w = w_in[l].astype(BF16)
    kvw = KV_WIDTH
    c_q = ATTN_WIDTH
    c_kc, c_vc, c_ks, c_vs, c_kw, c_vw = (c_q + i * kvw for i in range(6))
    c_gn = c_q + 6 * kvw
    c_uc = c_gn + N_GATES
    c_gm = c_uc + 2 * CONV_CH

    def widen(cols):
        k = cols.reshape(D_MODEL, N_KV_GROUPS, HEAD_DIM)
        return jnp.pad(k, ((0, 0), (0, 0), (0, AUG))).reshape(D_MODEL, N_KV_GROUPS * LANES)

    wn = jnp.concatenate([widen(w[:, c_ks:c_ks + kvw]), widen(w[:, c_kw:c_kw + kvw]),
                          w[:, c_kc:c_kc + 2 * kvw], w[:, c_uc:c_gm], w[:, c_gm:]], axis=1)
    gates = w[:, c_gn:c_uc].reshape(D_MODEL, N_KV_GROUPS, 3 * HEADS_PER_GROUP)
    gates = jnp.pad(gates, ((0, 0), (0, 0), (0, GATE_ROWS - 3 * HEADS_PER_GROUP)))
    wt = jnp.concatenate([w[:, :c_q], w[:, c_vs:c_vs + kvw], w[:, c_vw:c_vw + kvw],
                          gates.reshape(D_MODEL, N_KV_GROUPS * GATE_ROWS)], axis=1).T

    def per_group(a):
        z = jnp.zeros_like(a)
        return jnp.concatenate([jnp.concatenate([a, z], axis=-1), jnp.concatenate([z, a], axis=-1)], axis=-2)

    assert N_KV_GROUPS == 2
    w1 = lambda a: per_group(a.astype(BF16).reshape(CMP_BLOCK, HEAD_DIM, CMP_HIDDEN))
    pe = lambda a: jnp.concatenate([a] * N_KV_GROUPS, axis=1)
    return dict(
        norm1_g=norm1_g[l][None, :], wn=wn, wt=wt,
        pek=pe(cmp_pe_k[l]), pev=pe(cmp_pe_v[l]),
        w1k=w1(cmp_k_w1[l]), w2k=per_group(cmp_k_w2[l].astype(BF16)),
        w1v=w1(cmp_v_w1[l]), w2v=per_group(cmp_v_w2[l].astype(BF16)),
        wattn=w_attn_br[l].astype(BF16),
        dww=conv_dw_w[l], dwb=conv_dw_b[l][None, :], lng=conv_ln_g[l][None, :], lnb=conv_ln_b[l][None, :],
        wconv=w_conv_br[l].astype(BF16), wout=w_out[l].astype(BF16),
        norm2_g=norm2_g[l][None, :],
        wup=ffn_w_up[l].astype(BF16), cw=ffn_dw_w[l], cb=ffn_dw_b[l][None, :],
        wd=ffn_w_down[l].astype(BF16),
    )


def kernel(x, norm1_g, w_in, cmp_pe_k, cmp_pe_v, cmp_k_w1, cmp_k_w2, cmp_v_w1, cmp_v_w2, w_attn_br, conv_dw_w, conv_dw_b, conv_ln_g, conv_ln_b, w_conv_br, w_out, norm2_g, ffn_w_up, ffn_dw_w, ffn_dw_b, ffn_w_down, final_g):
    batch, seq, d = x.shape
    assert d == D_MODEL and seq % ROW_TILE == 0 and seq % FFN_ROW_TILE == 0 and seq % CMP_Q_TILE == 0
    assert seq // SEL_BLOCK == AUG_POS
    x2 = x.reshape(batch * seq, d)
    for l in range(w_in.shape[0]):
        p = _prep_layer(l, norm1_g, w_in, cmp_pe_k, cmp_pe_v, cmp_k_w1, cmp_k_w2, cmp_v_w1, cmp_v_w2, w_attn_br,
                        conv_dw_w, conv_dw_b, conv_ln_g, conv_ln_b, w_conv_br, w_out, norm2_g, ffn_w_up,
                        ffn_dw_w, ffn_dw_b, ffn_w_down)
        last = l == w_in.shape[0] - 1
        x2 = _layer(x2, batch, seq, p, final_g[None, :] if last else None)
    return x2.reshape(batch, seq, d)
```

```python
import functools
import math

import numpy as np
import jax
import jax.numpy as jnp
from jax import lax
from jax.experimental import pallas as pl
from jax.experimental.pallas import tpu as pltpu

F32 = jnp.float32
BF16 = jnp.bfloat16

D_MODEL = 1024
N_HEADS = 8
HEAD_DIM = 64
N_KV_GROUPS = 2
HEADS_PER_GROUP = N_HEADS // N_KV_GROUPS
ATTN_WIDTH = N_HEADS * HEAD_DIM
KV_WIDTH = N_KV_GROUPS * HEAD_DIM
CMP_BLOCK = 32
CMP_STRIDE = 16
CMP_HIDDEN = 128
SEL_BLOCK = 64
SEL_TOP = 16
SEL_SHIFT = 6
WINDOW = 512
FORCE_BONUS = 1.0e4
NEG_INF = -1.0e30
CONV_CH = 512
CONV_WIDTH = 31
FFN_DIM = 2816
FFN_CONV_WIDTH = 3
NORM_EPS = 1e-6
N_GATES = 3 * N_HEADS
ATTN_SCALE = HEAD_DIM ** -0.5
UNSELECTED = -(2.0 ** 99)

VMEM_LIMIT_BYTES = 56 * 1024 * 1024
LANES = 128

ROW_TILE = 512
FFN_ROW_TILE = 512
ATTN_TILE = 256
SOFTMAX_COLS = 128
CMP_Q_TILE = 512
ANCHOR_LAG = 6
CONV_HALO = 32
FFN_COL_TILE = 2816
FFN_HALO = 8

AUG = LANES - HEAD_DIM
AUG_POS = N_HEADS * 4
GATE_ROWS = 16
Q_ROWS = HEADS_PER_GROUP * HEAD_DIM

assert AUG_POS == 2048 // SEL_BLOCK and AUG_POS + 4 <= AUG


def _params(sem, flags=None):
    return pltpu.CompilerParams(dimension_semantics=sem, vmem_limit_bytes=VMEM_LIMIT_BYTES, flags=flags)


def _rms(x, g):
    y = x * lax.rsqrt(jnp.mean(x * x, axis=-1, keepdims=True) + NORM_EPS)
    return y * g


def _sigmoid(x):
    return 1.0 / (1.0 + jnp.exp(-x))


def _dot_nt(a, b, **kw):
    return lax.dot_general(a, b, (((1,), (1,)), ((), ())), preferred_element_type=F32, **kw)


def _dot_tn(a, b):
    return lax.dot_general(a, b, (((0,), (0,)), ((), ())), preferred_element_type=F32)


def _zero_after(x):
    u = pltpu.bitcast(x.astype(F32), jnp.uint32)
    return pltpu.bitcast(lax.shift_right_logical(lax.shift_right_logical(u, jnp.uint32(16)), jnp.uint32(16)), F32)


def _conformer_rows(z_ref, dww_ref, dwb_ref, lng_ref, lnb_ref, wp_ref, r0, ts, between):
    base = CONV_HALO - (CONV_WIDTH - 1)
    sub = 8
    acc = jnp.zeros((ts, CONV_CH), F32) + dwb_ref[...]
    for r in range(sub):
        zero = between()
        part = None
        ext = ts + (sub if r else 0)
        for s in range(r, base + CONV_WIDTH, sub):
            if s < base:
                continue
            w = dww_ref[s - base:s - base + 1, :]
            if part is None and zero is not None:
                w = w + zero
            term = w * z_ref[r0 + s - r:r0 + s - r + ext, :]
            part = term if part is None else part + term
        acc = acc + part[r:r + ts, :]
    mu = jnp.mean(acc, axis=-1, keepdims=True)
    cen = acc - mu
    var = jnp.mean(cen * cen, axis=-1, keepdims=True)
    y = cen * lax.rsqrt(var + NORM_EPS) * lng_ref[...] + lnb_ref[...]
    y = y * _sigmoid(y)
    return jnp.dot(y.astype(BF16), wp_ref[...], preferred_element_type=F32)


def _inproj_kernel(x_ref, g_ref, wn_ref, wt_ref, aug_ref, dww_ref, dwb_ref, lng_ref, lnb_ref, wp_ref,
                   kall_ref, kc_ref, vc_ref, yc_ref, gm_ref, qt_ref, vt_ref, gnt_ref, z_ref, zprev_ref,
                   *, tiles_per_seq):
    tm, halo = ROW_TILE, CONV_HALO
    h = _rms(x_ref[...], g_ref[...]).astype(BF16)
    kw, kvw = N_KV_GROUPS * LANES, KV_WIDTH
    c_kc = 2 * kw
    c_uc = c_kc + 2 * kvw
    c_gm = c_uc + 2 * CONV_CH
    proj = lambda lo, hi: jnp.dot(h, wn_ref[:, lo:hi], preferred_element_type=F32)

    u = proj(c_uc, c_gm)
    seq_start = pl.program_id(0) % tiles_per_seq == 0
    z_ref[0:halo, :] = jnp.where(seq_start, 0.0, zprev_ref[...])
    z_ref[halo:, :] = u[:, :CONV_CH] * _sigmoid(u[:, CONV_CH:])
    zprev_ref[...] = z_ref[tm:tm + halo, :]

    step = 2 * LANES
    nq, nv = qt_ref.shape[0], vt_ref.shape[0]

    def key_piece(br):
        def run():
            y = proj(br * kw, (br + 1) * kw)
            kall_ref[br] = (y + aug_ref[:, br * kw:(br + 1) * kw]).astype(BF16)
            return y
        return run

    def cmp_piece():
        kc_ref[...] = proj(c_kc, c_kc + kvw)
        y = proj(c_kc + kvw, c_uc)
        vc_ref[...] = y
        return y

    def gate_piece(c):
        def run():
            y = proj(c_gm + c, c_gm + c + step)
            gm_ref[:, c:c + step] = y.astype(BF16)
            return y
        return run

    def t_piece(lo, hi):
        def run():
            t = _dot_nt(wt_ref[lo:hi, :], h)
            if hi <= nq:
                qt_ref[lo:hi, :] = (t * ATTN_SCALE).astype(BF16)
            elif hi <= nq + nv:
                vt_ref[lo - nq:hi - nq, :] = t.astype(BF16)
            else:
                gnt_ref[...] = t
            return t
        return run

    pieces = ([key_piece(0), key_piece(1), cmp_piece] + [gate_piece(c) for c in range(0, 2 * D_MODEL, step)]
              + [t_piece(lo, lo + step) for lo in range(0, nq + nv, step)] + [t_piece(nq + nv, wt_ref.shape[0])])

    n_quarters, groups_per_quarter = 4, 8
    rq = tm // n_quarters
    slots = n_quarters * groups_per_quarter
    emitted = [0, 0]

    pending = []

    def between():
        if emitted[1] * slots < emitted[0] * len(pieces):
            done = pieces[emitted[1]]()
            emitted[1] += 1
            row = _zero_after(done[-8:, -LANES:])[0:1, :]
            pending.append((emitted[0] + ANCHOR_LAG, jnp.concatenate([row] * (CONV_CH // LANES), axis=1)))
        emitted[0] += 1
        zero = None
        while pending and (pending[0][0] < emitted[0] or emitted[0] == slots):
            z = pending.pop(0)[1]
            zero = z if zero is None else zero + z
        return zero

    for k in range(n_quarters):
        yq = _conformer_rows(z_ref, dww_ref, dwb_ref, lng_ref, lnb_ref, wp_ref, k * rq, rq, between)
        yc_ref[k * rq:(k + 1) * rq, :] = yq.astype(BF16)
    assert emitted[1] == len(pieces)


def _inproj(x2, g, wn, wt, aug, dww, dwb, lng, lnb, wp, seq):
    t = x2.shape[0]
    tm = ROW_TILE
    row = lambda n: pl.BlockSpec((tm, n), lambda i: (i, 0))
    col = lambda n: pl.BlockSpec((n, tm), lambda i: (0, i))
    full = lambda a: pl.BlockSpec(a.shape, lambda i: (0, 0))
    kw = N_KV_GROUPS * LANES
    nat = ((KV_WIDTH, F32), (KV_WIDTH, F32), (D_MODEL, BF16), (2 * D_MODEL, BF16))
    trn = ((ATTN_WIDTH, BF16), (2 * KV_WIDTH, BF16), (N_KV_GROUPS * GATE_ROWS, F32))
    return pl.pallas_call(
        functools.partial(_inproj_kernel, tiles_per_seq=seq // tm),
        grid=(t // tm,),
        in_specs=[row(D_MODEL), full(g), full(wn), full(wt),
                  pl.BlockSpec((tm, 2 * kw), lambda i: (i % (seq // tm), 0)),
                  full(dww), full(dwb), full(lng), full(lnb), full(wp)],
        out_specs=[pl.BlockSpec((2, tm, kw), lambda i: (0, i, 0))] + [row(n) for n, _ in nat]
                  + [col(n) for n, _ in trn],
        out_shape=[jax.ShapeDtypeStruct((2, t, kw), BF16)] + [jax.ShapeDtypeStruct((t, n), d) for n, d in nat]
                  + [jax.ShapeDtypeStruct((n, t), d) for n, d in trn],
        scratch_shapes=[pltpu.VMEM((tm + CONV_HALO, CONV_CH), F32), pltpu.VMEM((CONV_HALO, CONV_CH), F32)],
        compiler_params=_params(("arbitrary",)),
        name="inproj",
    )(x2, g, wn, wt, aug, dww, dwb, lng, lnb, wp)


def _gelu_tanh(x):
    return 0.5 * x * (1.0 + jnp.tanh(math.sqrt(2.0 / math.pi) * (x + 0.044715 * (x * x * x))))


def _compress(x_ref, pe_ref, w1_ref, w2_ref, nr):
    half = CMP_BLOCK // 2
    a = b = None
    for l in range(half):
        x = x_ref[pl.ds(l, nr, stride=CMP_STRIDE), :]
        ta = jnp.dot((x + pe_ref[l:l + 1, :]).astype(BF16), w1_ref[l], preferred_element_type=F32)
        tb = jnp.dot((x + pe_ref[half + l:half + l + 1, :]).astype(BF16), w1_ref[half + l],
                     preferred_element_type=F32)
        a = ta if a is None else a + ta
        b = tb if b is None else b + tb
    pre = a + pltpu.roll(b, nr - 1, 0)
    hid = _gelu_tanh(pre).astype(BF16)
    return jnp.dot(hid, w2_ref[...], preferred_element_type=F32)


def _cmp_kernel(kc_ref, vc_ref, pek_ref, pev_ref, w1k_ref, w2k_ref, w1v_ref, w2v_ref,
                qt_ref, mapt_ref, oct_ref, selb_ref, *, seq):
    ng, nh, dh = N_KV_GROUPS, HEADS_PER_GROUP, HEAD_DIM
    nr = seq // CMP_STRIDE
    kc = _compress(kc_ref, pek_ref, w1k_ref, w2k_ref, nr).astype(BF16)
    vct = jnp.transpose(_compress(vc_ref, pev_ref, w1v_ref, w2v_ref, nr)).astype(BF16)
    nsel = seq // SEL_BLOCK
    tq = CMP_Q_TILE
    c_end = lax.broadcasted_iota(jnp.int32, (nr, 1), 0) * CMP_STRIDE + (CMP_BLOCK - 1)
    blk = lax.broadcasted_iota(jnp.int32, (nsel, 1), 0)
    zeros_q = jnp.zeros((dh, tq), BF16)

    n_top = min(SEL_TOP, nsel)
    for ti in range(seq // tq):
        t0, t1 = ti * tq, (ti + 1) * tq
        nrv, nbv = t1 // CMP_STRIDE, t1 // SEL_BLOCK
        pos = t0 + lax.broadcasted_iota(jnp.int32, (1, tq), 1)
        dist = pos - c_end[:nrv]
        valid = dist >= 0
        distf = dist.astype(F32)
        cur = pos >> SEL_SHIFT
        blk_v = blk[:nbv]
        forced = (blk_v == 0) | (blk_v == cur) | (blk_v == cur - 1)
        causal = blk_v <= cur
        for g in range(ng):
            psum = jnp.zeros((nrv, tq), F32)
            for j in range(nh):
                slope = 2.0 ** -(g * nh + j + 1)
                rows = slice((g * nh + j) * dh, (g * nh + j + 1) * dh)
                q = qt_ref[rows, t0:t1]
                q2 = jnp.concatenate([q, zeros_q] if g == 0 else [zeros_q, q], axis=0)
                s = jnp.dot(kc[:nrv], q2, preferred_element_type=F32)
                s = jnp.where(valid, s - slope * distf, NEG_INF)
                m = jnp.max(s, axis=0, keepdims=True)
                e = jnp.where(valid, jnp.exp(s - m), 0.0)
                l = jnp.sum(e, axis=0, keepdims=True)
                p = e / jnp.where(l > 0.0, l, 1.0)
                psum = psum + p
                pfull = p.astype(BF16)
                if nrv < nr:
                    pfull = jnp.concatenate([pfull, jnp.zeros((nr - nrv, tq), BF16)], axis=0)
                oct_ref[rows, t0:t1] = jnp.dot(vct[g * dh:(g + 1) * dh, :], pfull, preferred_element_type=F32)
            if nrv < nr:
                psum = jnp.concatenate([psum, jnp.zeros((nr - nrv, tq), F32)], axis=0)
            imp = jnp.dot(mapt_ref[0:nbv, :], psum, precision=lax.Precision.HIGHEST, preferred_element_type=F32)
            val = jnp.where(causal, imp + jnp.where(forced, FORCE_BONUS, 0.0), NEG_INF)
            sub = 8
            cnts = []
            for c0 in range(0, nbv, sub):
                vc_ = val[c0:c0 + sub, :]
                cnt = jnp.zeros((sub, tq), F32)
                for i in range(nbv):
                    vi = val[i:i + 1, :]
                    if i < c0:
                        cnt = cnt + jnp.where(vi >= vc_, 1.0, 0.0)
                    elif i >= c0 + sub:
                        cnt = cnt + jnp.where(vi > vc_, 1.0, 0.0)
                    else:
                        tie = (blk_v[c0:c0 + sub] > i).astype(F32)
                        cnt = cnt + jnp.where(vi > vc_, 1.0, jnp.where(vi == vc_, tie, 0.0))
                cnts.append(cnt)
            cnt = jnp.concatenate(cnts, axis=0) if len(cnts) > 1 else cnts[0]
            selb = jnp.where(causal & (cnt < n_top), 0.0, UNSELECTED)
            if nbv < nsel:
                selb = jnp.concatenate([selb, jnp.full((nsel - nbv, tq), UNSELECTED, F32)], axis=0)
            selb_ref[0, g, :, t0:t1] = selb.astype(BF16)


def _cmp_branch(kc, vc, pek, pev, w1k, w2k, w1v, w2v, qt, mapt, batch, seq):
    nsel = seq // SEL_BLOCK
    full = lambda a: pl.BlockSpec(a.shape, lambda i: (0,) * a.ndim)
    rows = pl.BlockSpec((seq, KV_WIDTH), lambda i: (i, 0))
    heads = pl.BlockSpec((ATTN_WIDTH, seq), lambda i: (0, i))
    return pl.pallas_call(
        functools.partial(_cmp_kernel, seq=seq),
        grid=(batch,),
        in_specs=[rows, rows, full(pek), full(pev), full(w1k), full(w2k), full(w1v), full(w2v), heads, full(mapt)],
        out_specs=[heads, pl.BlockSpec((1, N_KV_GROUPS, nsel, seq), lambda i: (i, 0, 0, 0))],
        out_shape=[jax.ShapeDtypeStruct((ATTN_WIDTH, batch * seq), F32),
                   jax.ShapeDtypeStruct((batch, N_KV_GROUPS, nsel, seq), BF16)],
        compiler_params=_params(("parallel",)),
        name="cmp_topk",
    )(kc, vc, pek, pev, w1k, w2k, w1v, w2v, qt, mapt)


def _attn_kernel(qt_ref, k_ref, v_ref, selb_ref, oct_ref, gnt_ref, o_ref,
                 qa_ref, s_ref, c_ref, m_ref, l_ref, acc_ref, out_ref):
    qi = pl.program_id(1)
    ta = ATTN_TILE
    ng, nh, dh = N_KV_GROUPS, HEADS_PER_GROUP, HEAD_DIM
    nsel = selb_ref.shape[2]
    groups = range(ng)
    pos = qi * ta + lax.broadcasted_iota(jnp.int32, (1, ta), 1)
    tb = (pos >> SEL_SHIFT).astype(F32)
    tr = (pos & (SEL_BLOCK - 1)).astype(F32)
    arow = lax.broadcasted_iota(jnp.int32, (16, 1), 0)
    gate = _sigmoid(gnt_ref[...])

    def gate_row(g, c):
        r = g * GATE_ROWS + c
        return jnp.concatenate([gate[r + 3 * j:r + 3 * j + 1, :] for j in range(nh)], axis=1)

    for g in groups:
        for j in range(nh):
            slope = 2.0 ** -(g * nh + j + 1)
            cols = slice(j * ta, (j + 1) * ta)
            rows = slice((g * nh + j) * dh, (g * nh + j + 1) * dh)
            qa_ref[g, 0:dh, cols] = qt_ref[rows, :]
            qa_ref[g, dh:dh + nsel, cols] = selb_ref[0, g]
            alibi = jnp.where(arow == 0, slope * SEL_BLOCK,
                              jnp.where(arow == 1, slope,
                                        jnp.where(arow == 2, -slope * SEL_BLOCK * tb,
                                                  jnp.where(arow == 3, -slope * tr, 0.0))))
            qa_ref[g, dh + nsel:dh + nsel + 16, cols] = alibi.astype(BF16)
            qa_ref[g, dh + nsel + 16:, cols] = jnp.zeros((LANES - dh - nsel - 16, ta), BF16)
            out_ref[g, :, cols] = gate[g * GATE_ROWS + 3 * j:g * GATE_ROWS + 3 * j + 1, :] * oct_ref[rows, :]

    krow = lax.broadcasted_iota(jnp.int32, (ta, 1), 0)
    qcol = lax.broadcasted_iota(jnp.int32, (1, ta), 1)
    lanes4 = lambda b: jnp.concatenate([b] * nh, axis=1)

    def qk(g, br, kt):
        k = k_ref[br, pl.ds(pl.multiple_of(kt * ta, ta), ta), g * LANES:(g + 1) * LANES]
        return jnp.dot(k, qa_ref[g], preferred_element_type=F32)

    def pv(g, br, kt, p):
        v = v_ref[br, g, :, pl.ds(pl.multiple_of(kt * ta, ta), ta)]
        return jnp.dot(v, p.astype(BF16), preferred_element_type=F32)

    far = WINDOW // ta
    causal = jnp.where(qcol >= krow, 0.0, NEG_INF)
    kt_far = jnp.maximum(qi - far, 0)
    beyond = jnp.where((qcol < krow) & (qi >= far), 0.0, NEG_INF)
    for c in range(nh):
        cols = slice(c * ta, (c + 1) * ta)
        for g in groups:
            def qk_head(br, kt):
                k = k_ref[br, pl.ds(pl.multiple_of(kt * ta, ta), ta), g * LANES:(g + 1) * LANES]
                return jnp.dot(k, qa_ref[g, :, cols], preferred_element_type=F32)

            s_sel = qk_head(0, qi) + causal
            m_sel = jnp.max(s_sel, axis=0, keepdims=True)
            p_sel = jnp.exp(s_sel - m_sel)
            m_ref[g, 0, :, cols] = m_sel
            l_ref[g, 0, :, cols] = jnp.sum(p_sel, axis=0, keepdims=True)
            acc_ref[g, 0, :, cols] = pv(g, 0, qi, p_sel)
            s_win = qk_head(1, qi) + causal
            s_far = qk_head(1, kt_far) + beyond
            m_win = jnp.maximum(jnp.max(s_win, axis=0, keepdims=True), jnp.max(s_far, axis=0, keepdims=True))
            p_win = jnp.exp(s_win - m_win)
            p_far = jnp.exp(s_far - m_win)
            m_ref[g, 1, :, cols] = m_win
            l_ref[g, 1, :, cols] = jnp.sum(p_win, axis=0, keepdims=True) + jnp.sum(p_far, axis=0, keepdims=True)
            acc_ref[g, 1, :, cols] = pv(g, 1, qi, p_win) + pv(g, 1, kt_far, p_far)

    n_tasks = qi + 1

    def task(n):
        n = jnp.minimum(n, n_tasks - 1)
        return jnp.where(n == 0, 1, 0), qi - jnp.maximum(n, 1)

    def scores(n, slot):
        br, kt = task(n)
        for g in groups:
            s = qk(g, br, kt)
            s_ref[g, slot] = s
            c_ref[g, slot] = jnp.max(s, axis=0, keepdims=True)

    def absorb(n, slot):
        br, kt = task(n)
        for c in range(nh):
            cols = slice(c * ta, (c + 1) * ta)
            for g in groups:
                m_old = m_ref[g, br, :, cols]
                m_new = jnp.maximum(m_old, c_ref[g, slot, :, cols])
                alpha = jnp.exp(m_old - m_new)
                p = jnp.exp(s_ref[g, slot, :, cols] - m_new)
                l_ref[g, br, :, cols] = alpha * l_ref[g, br, :, cols] + jnp.sum(p, axis=0, keepdims=True)
                m_ref[g, br, :, cols] = m_new
                acc_ref[g, br, :, cols] = alpha * acc_ref[g, br, :, cols] + pv(g, br, kt, p)

    @pl.when(qi >= 1)
    def _():
        scores(0, 0)

        def pair(i, carry):
            n = 2 * i
            scores(n + 1, 1)
            absorb(n, 0)
            scores(n + 2, 0)
            absorb(n + 1, 1)
            return carry

        lax.fori_loop(0, n_tasks // 2, pair, 0)

        @pl.when(n_tasks % 2 == 1)
        def _():
            absorb(n_tasks - 1, 0)

    for g in groups:
        out = (out_ref[g] + gate_row(g, 1) * (acc_ref[g, 0] / l_ref[g, 0])
               + gate_row(g, 2) * (acc_ref[g, 1] / l_ref[g, 1]))
        for j in range(nh):
            rows = slice((g * nh + j) * dh, (g * nh + j + 1) * dh)
            o_ref[rows, :] = out[:, j * ta:(j + 1) * ta].astype(BF16)


def _attention(qt, kall, vt, selb, oct, gnt, batch, seq):
    ta = ATTN_TILE
    assert WINDOW % ta == 0
    nqt = seq // ta
    ng, nh, dh = N_KV_GROUPS, HEADS_PER_GROUP, HEAD_DIM
    nsel = selb.shape[2]
    qtile = lambda rows: pl.BlockSpec((rows, ta), lambda b, t: (0, b * nqt + t))
    vall = vt.reshape(2, ng, dh, batch * seq)
    wide = nh * ta
    return pl.pallas_call(
        _attn_kernel,
        grid=(batch, nqt),
        in_specs=[qtile(ATTN_WIDTH),
                  pl.BlockSpec((2, seq, ng * LANES), lambda b, t: (0, b, 0)),
                  pl.BlockSpec((2, ng, dh, seq), lambda b, t: (0, 0, 0, b)),
                  pl.BlockSpec((1, ng, nsel, ta), lambda b, t: (b, 0, 0, t)),
                  qtile(ATTN_WIDTH), qtile(ng * GATE_ROWS)],
        out_specs=qtile(ATTN_WIDTH),
        out_shape=jax.ShapeDtypeStruct((ATTN_WIDTH, batch * seq), BF16),
        scratch_shapes=[pltpu.VMEM((ng, LANES, wide), BF16),
                        pltpu.VMEM((ng, 2, ta, wide), F32),
                        pltpu.VMEM((ng, 2, 1, wide), F32),
                        pltpu.VMEM((ng, 2, 1, wide), F32),
                        pltpu.VMEM((ng, 2, 1, wide), F32),
                        pltpu.VMEM((ng, 2, dh, wide), F32),
                        pltpu.VMEM((ng, dh, wide), F32)],
        compiler_params=_params(("parallel", "arbitrary")),
        name="nsa_flash",
    )(qt, kall, vall, selb, oct, gnt)


def _merge_kernel(ot_ref, yc_ref, gm_ref, x_ref, wa_ref, wo_ref, out_ref):
    ya = _dot_tn(ot_ref[...], wa_ref[...])
    gm = gm_ref[...].astype(F32)
    mix = _sigmoid(gm[:, :D_MODEL]) * ya + _sigmoid(gm[:, D_MODEL:]) * yc_ref[...].astype(F32)
    out_ref[...] = x_ref[...] + jnp.dot(mix.astype(BF16), wo_ref[...], preferred_element_type=F32)


def _merge(ot, yc2, gm2, x2, wa, wo):
    t = x2.shape[0]
    tm = ROW_TILE
    row = lambda n: pl.BlockSpec((tm, n), lambda i: (i, 0))
    full = lambda a: pl.BlockSpec(a.shape, lambda i: (0, 0))
    return pl.pallas_call(
        _merge_kernel,
        grid=(t // tm,),
        in_specs=[pl.BlockSpec((ATTN_WIDTH, tm), lambda i: (0, i)), row(D_MODEL), row(2 * D_MODEL), row(D_MODEL),
                  full(wa), full(wo)],
        out_specs=row(D_MODEL),
        out_shape=jax.ShapeDtypeStruct((t, D_MODEL), F32),
        compiler_params=_params(("parallel",)),
        name="merge_outproj",
    )(ot, yc2, gm2, x2, wa, wo)


def _ffn_kernel(x_ref, xh_ref, g_ref, wup_ref, cw_ref, cb_ref, wd_ref, *rest, tiles_per_seq, final_norm):
    if final_norm:
        fg_ref, out_ref, h_ref, a_ref, v_ref, gt_ref = rest
    else:
        out_ref, h_ref, a_ref, v_ref, gt_ref = rest
    tm, halo, tf = FFN_ROW_TILE, FFN_HALO, FFN_COL_TILE
    seq_start = pl.program_id(0) % tiles_per_seq == 0
    hh = _rms(xh_ref[...], g_ref[...])
    h_ref[0:halo, :] = jnp.where(seq_start, 0.0, hh).astype(BF16)
    h_ref[halo:, :] = _rms(x_ref[...], g_ref[...]).astype(BF16)
    h = h_ref[...]

    def conv(ref, w, b):
        out = b
        for k in range(FFN_CONV_WIDTH):
            off = halo - (FFN_CONV_WIDTH - 1) + k
            out = out + w[k:k + 1, :] * ref[off:off + tm, :]
        return out

    for j in range(FFN_DIM // tf):
        slot = j % a_ref.shape[0]
        ca_cols = slice(j * tf, (j + 1) * tf)
        cv_cols = slice(FFN_DIM + j * tf, FFN_DIM + (j + 1) * tf)
        a_ref[slot] = jnp.dot(h, wup_ref[:, ca_cols], preferred_element_type=F32)
        v_ref[slot] = jnp.dot(h, wup_ref[:, cv_cols], preferred_element_type=F32)
        ca = conv(a_ref.at[slot], cw_ref[:, ca_cols], cb_ref[:, ca_cols])
        cv = conv(v_ref.at[slot], cw_ref[:, cv_cols], cb_ref[:, cv_cols])
        gt_ref[:, ca_cols] = (ca * _sigmoid(ca) * cv).astype(BF16)
    y = x_ref[...] + jnp.dot(gt_ref[...], wd_ref[...], preferred_element_type=F32)
    out_ref[...] = _rms(y, fg_ref[...]) if final_norm else y


def _ffn(x2, g, wup, cw, cb, wd, seq, final_g=None):
    t = x2.shape[0]
    tm, halo, tf = FFN_ROW_TILE, FFN_HALO, FFN_COL_TILE
    assert FFN_DIM % tf == 0
    slots = min(2, FFN_DIM // tf)
    once = lambda a: pl.BlockSpec(a.shape, lambda i: (0,) * a.ndim, pipeline_mode=pl.Buffered(1))
    args = [x2, x2, g, wup, cw, cb, wd] + ([final_g] if final_g is not None else [])
    return pl.pallas_call(
        functools.partial(_ffn_kernel, tiles_per_seq=seq // tm, final_norm=final_g is not None),
        grid=(t // tm,),
        in_specs=[pl.BlockSpec((tm, D_MODEL), lambda i: (i, 0)),
                  pl.BlockSpec((halo, D_MODEL), lambda i: (jnp.maximum(i * (tm // halo) - 1, 0), 0))]
                 + [once(a) for a in args[2:]],
        out_specs=pl.BlockSpec((tm, D_MODEL), lambda i: (i, 0)),
        out_shape=jax.ShapeDtypeStruct((t, D_MODEL), F32),
        scratch_shapes=[pltpu.VMEM((tm + halo, D_MODEL), BF16), pltpu.VMEM((slots, tm + halo, tf), F32),
                        pltpu.VMEM((slots, tm + halo, tf), F32), pltpu.VMEM((tm, FFN_DIM), BF16)],
        compiler_params=_params(("parallel",)),
        name="conv_ffn",
    )(*args)


def _sel_map_t(seq):
    ncmp = (seq - CMP_BLOCK) // CMP_STRIDE + 1
    nr = seq // CMP_STRIDE
    nsel = seq // SEL_BLOCK
    cs = np.arange(ncmp) * CMP_STRIDE
    ce = cs + CMP_BLOCK - 1
    ss = np.arange(nsel) * SEL_BLOCK
    se = ss + SEL_BLOCK - 1
    ov = np.minimum(ce[:, None], se[None, :]) - np.maximum(cs[:, None], ss[None, :]) + 1
    m = np.zeros((nsel, nr), np.float32)
    m[:, :ncmp] = (np.clip(ov, 0, None).astype(np.float32) / CMP_BLOCK).T
    return jnp.asarray(m)


def _key_aug(seq):
    pos = np.arange(seq)
    nsel = seq // SEL_BLOCK
    a = np.zeros((seq, AUG), np.float32)
    a[:, AUG_POS + 0] = pos // SEL_BLOCK
    a[:, AUG_POS + 1] = pos % SEL_BLOCK
    a[:, AUG_POS + 2] = 1.0
    a[:, AUG_POS + 3] = 1.0
    s = a.copy()
    s[pos, pos // SEL_BLOCK] = 1.0
    assert nsel <= AUG_POS
    z = np.zeros((seq, HEAD_DIM), np.float32)
    return jnp.asarray(np.concatenate([z, s, z, s, z, a, z, a], axis=1))


def _layer(x2, batch, seq, p, final_g):
    kall, kc, vc, yc, gm2, qt, vt, gnt = _inproj(x2, p["norm1_g"], p["wn"], p["wt"], _key_aug(seq), p["dww"],
                                                 p["dwb"], p["lng"], p["lnb"], p["wconv"], seq)
    oct, selb = _cmp_branch(kc, vc, p["pek"], p["pev"], p["w1k"], p["w2k"], p["w1v"], p["w2v"], qt,
                            _sel_map_t(seq), batch, seq)
    ot = _attention(qt, kall, vt, selb, oct, gnt, batch, seq)
    x2 = _merge(ot, yc, gm2, x2, p["wattn"], p["wout"])
    return _ffn(x2, p["norm2_g"], p["wup"], p["cw"], p["cb"], p["wd"], seq, final_g)


def _prep_layer(l, norm1_g, w_in, cmp_pe_k, cmp_pe_v, cmp_k_w1, cmp_k_w2, cmp_v_w1, cmp_v_w2, w_attn_br,
                conv_dw_w, conv_dw_b, conv_ln_g, conv_ln_b, w_conv_br, w_out, norm2_g, ffn_w_up, ffn_dw_w,
                ffn_dw_b, ffn_w_down):
    w = w_in[l].astype(BF16)
    kvw = KV_WIDTH
    c_q = ATTN_WIDTH
    c_kc, c_vc, c_ks, c_vs, c_kw, c_vw = (c_q + i * kvw for i in range(6))
    c_gn = c_q + 6 * kvw
    c_uc = c_gn + N_GATES
    c_gm = c_uc + 2 * CONV_CH

    def widen(cols):
        k = cols.reshape(D_MODEL, N_KV_GROUPS, HEAD_DIM)
        return jnp.pad(k, ((0, 0), (0, 0), (0, AUG))).reshape(D_MODEL, N_KV_GROUPS * LANES)

    wn = jnp.concatenate([widen(w[:, c_ks:c_ks + kvw]), widen(w[:, c_kw:c_kw + kvw]),
                          w[:, c_kc:c_kc + 2 * kvw], w[:, c_uc:c_gm], w[:, c_gm:]], axis=1)
    gates = w[:, c_gn:c_uc].reshape(D_MODEL, N_KV_GROUPS, 3 * HEADS_PER_GROUP)
    gates = jnp.pad(gates, ((0, 0), (0, 0), (0, GATE_ROWS - 3 * HEADS_PER_GROUP)))
    wt = jnp.concatenate([w[:, :c_q], w[:, c_vs:c_vs + kvw], w[:, c_vw:c_vw + kvw],
                          gates.reshape(D_MODEL, N_KV_GROUPS * GATE_ROWS)], axis=1).T

    def per_group(a):
        z = jnp.zeros_like(a)
        return jnp.concatenate([jnp.concatenate([a, z], axis=-1), jnp.concatenate([z, a], axis=-1)], axis=-2)

    assert N_KV_GROUPS == 2
    w1 = lambda a: per_group(a.astype(BF16).reshape(CMP_BLOCK, HEAD_DIM, CMP_HIDDEN))
    pe = lambda a: jnp.concatenate([a] * N_KV_GROUPS, axis=1)
    return dict(
        norm1_g=norm1_g[l][None, :], wn=wn, wt=wt,
        pek=pe(cmp_pe_k[l]), pev=pe(cmp_pe_v[l]),
        w1k=w1(cmp_k_w1[l]), w2k=per_group(cmp_k_w2[l].astype(BF16)),
        w1v=w1(cmp_v_w1[l]), w2v=per_group(cmp_v_w2[l].astype(BF16)),
        wattn=w_attn_br[l].astype(BF16),
        dww=conv_dw_w[l], dwb=conv_dw_b[l][None, :], lng=conv_ln_g[l][None, :], lnb=conv_ln_b[l][None, :],
        wconv=w_conv_br[l].astype(BF16), wout=w_out[l].astype(BF16),
        norm2_g=norm2_g[l][None, :],
        wup=ffn_w_up[l].astype(BF16), cw=ffn_dw_w[l], cb=ffn_dw_b[l][None, :],
        wd=ffn_w_down[l].astype(BF16),
    )


def kernel(x, norm1_g, w_in, cmp_pe_k, cmp_pe_v, cmp_k_w1, cmp_k_w2, cmp_v_w1, cmp_v_w2, w_attn_br, conv_dw_w, conv_dw_b, conv_ln_g, conv_ln_b, w_conv_br, w_out, norm2_g, ffn_w_up, ffn_dw_w, ffn_dw_b, ffn_w_down, final_g):
    batch, seq, d = x.shape
    assert d == D_MODEL and seq % ROW_TILE == 0 and seq % FFN_ROW_TILE == 0 and seq % CMP_Q_TILE == 0
    assert seq // SEL_BLOCK == AUG_POS
    x2 = x.reshape(batch * seq, d)
    for l in range(w_in.shape[0]):
        p = _prep_layer(l, norm1_g, w_in, cmp_pe_k, cmp_pe_v, cmp_k_w1, cmp_k_w2, cmp_v_w1, cmp_v_w2, w_attn_br,
                        conv_dw_w, conv_dw_b, conv_ln_g, conv_ln_b, w_conv_br, w_out, norm2_g, ffn_w_up,
                        ffn_dw_w, ffn_dw_b, ffn_w_down)
        last = l == w_in.shape[0] - 1
        x2 = _layer(x2, batch, seq, p, final_g[None, :] if last else None)
    return x2.reshape(batch, seq, d)
```

```python
import functools
import math

import numpy as np
import jax
import jax.numpy as jnp
from jax import lax
from jax.experimental import pallas as pl
from jax.experimental.pallas import tpu as pltpu

F32 = jnp.float32
BF16 = jnp.bfloat16

D_MODEL = 1024
N_HEADS = 8
HEAD_DIM = 64
N_KV_GROUPS = 2
HEADS_PER_GROUP = N_HEADS // N_KV_GROUPS
ATTN_WIDTH = N_HEADS * HEAD_DIM
KV_WIDTH = N_KV_GROUPS * HEAD_DIM
CMP_BLOCK = 32
CMP_STRIDE = 16
CMP_HIDDEN = 128
SEL_BLOCK = 64
SEL_TOP = 16
SEL_SHIFT = 6
WINDOW = 512
FORCE_BONUS = 1.0e4
NEG_INF = -1.0e30
CONV_CH = 512
CONV_WIDTH = 31
FFN_DIM = 2816
FFN_CONV_WIDTH = 3
NORM_EPS = 1e-6
N_GATES = 3 * N_HEADS
ATTN_SCALE = HEAD_DIM ** -0.5
UNSELECTED = -(2.0 ** 99)

VMEM_LIMIT_BYTES = 56 * 1024 * 1024
LANES = 128

ROW_TILE = 512
FFN_ROW_TILE = 512
ATTN_TILE = 256
SOFTMAX_COLS = 128
CMP_Q_TILE = 512
ANCHOR_LAG = 6
CONV_HALO = 32
FFN_COL_TILE = 2816
FFN_HALO = 8

AUG = LANES - HEAD_DIM
AUG_POS = N_HEADS * 4
GATE_ROWS = 16
Q_ROWS = HEADS_PER_GROUP * HEAD_DIM

assert AUG_POS == 2048 // SEL_BLOCK and AUG_POS + 4 <= AUG


def _params(sem, flags=None):
    return pltpu.CompilerParams(dimension_semantics=sem, vmem_limit_bytes=VMEM_LIMIT_BYTES, flags=flags)


def _rms(x, g):
    y = x * lax.rsqrt(jnp.mean(x * x, axis=-1, keepdims=True) + NORM_EPS)
    return y * g


def _sigmoid(x):
    return 1.0 / (1.0 + jnp.exp(-x))


def _dot_nt(a, b, **kw):
    return lax.dot_general(a, b, (((1,), (1,)), ((), ())), preferred_element_type=F32, **kw)


def _dot_tn(a, b):
    return lax.dot_general(a, b, (((0,), (0,)), ((), ())), preferred_element_type=F32)


def _zero_after(x):
    u = pltpu.bitcast(x.astype(F32), jnp.uint32)
    return pltpu.bitcast(lax.shift_right_logical(lax.shift_right_logical(u, jnp.uint32(16)), jnp.uint32(16)), F32)


def _conformer_rows(z_ref, dww_ref, dwb_ref, lng_ref, lnb_ref, wp_ref, r0, ts, between):
    base = CONV_HALO - (CONV_WIDTH - 1)
    sub = 8
    acc = jnp.zeros((ts, CONV_CH), F32) + dwb_ref[...]
    for r in range(sub):
        zero = between()
        part = None
        ext = ts + (sub if r else 0)
        for s in range(r, base + CONV_WIDTH, sub):
            if s < base:
                continue
            w = dww_ref[s - base:s - base + 1, :]
            if part is None and zero is not None:
                w = w + zero
            term = w * z_ref[r0 + s - r:r0 + s - r + ext, :]
            part = term if part is None else part + term
        acc = acc + part[r:r + ts, :]
    mu = jnp.mean(acc, axis=-1, keepdims=True)
    cen = acc - mu
    var = jnp.mean(cen * cen, axis=-1, keepdims=True)
    y = cen * lax.rsqrt(var + NORM_EPS) * lng_ref[...] + lnb_ref[...]
    y = y * _sigmoid(y)
    return jnp.dot(y.astype(BF16), wp_ref[...], preferred_element_type=F32)


def _inproj_kernel(x_ref, g_ref, wn_ref, wt_ref, aug_ref, dww_ref, dwb_ref, lng_ref, lnb_ref, wp_ref,
                   kall_ref, kc_ref, vc_ref, yc_ref, gm_ref, qt_ref, vt_ref, gnt_ref, z_ref, zprev_ref,
                   *, tiles_per_seq):
    tm, halo = ROW_TILE, CONV_HALO
    h = _rms(x_ref[...], g_ref[...]).astype(BF16)
    kw, kvw = N_KV_GROUPS * LANES, KV_WIDTH
    c_kc = 2 * kw
    c_uc = c_kc + 2 * kvw
    c_gm = c_uc + 2 * CONV_CH
    proj = lambda lo, hi: jnp.dot(h, wn_ref[:, lo:hi], preferred_element_type=F32)

    u = proj(c_uc, c_gm)
    seq_start = pl.program_id(0) % tiles_per_seq == 0
    z_ref[0:halo, :] = jnp.where(seq_start, 0.0, zprev_ref[...])
    z_ref[halo:, :] = u[:, :CONV_CH] * _sigmoid(u[:, CONV_CH:])
    zprev_ref[...] = z_ref[tm:tm + halo, :]

    step = 2 * LANES
    nq, nv = qt_ref.shape[0], vt_ref.shape[0]

    def key_piece(br):
        def run():
            y = proj(br * kw, (br + 1) * kw)
            kall_ref[br] = (y + aug_ref[:, br * kw:(br + 1) * kw]).astype(BF16)
            return y
        return run

    def cmp_piece():
        kc_ref[...] = proj(c_kc, c_kc + kvw)
        y = proj(c_kc + kvw, c_uc)
        vc_ref[...] = y
        return y

    def gate_piece(c):
        def run():
            y = proj(c_gm + c, c_gm + c + step)
            gm_ref[:, c:c + step] = y.astype(BF16)
            return y
        return run

    def t_piece(lo, hi):
        def run():
            t = _dot_nt(wt_ref[lo:hi, :], h)
            if hi <= nq:
                qt_ref[lo:hi, :] = (t * ATTN_SCALE).astype(BF16)
            elif hi <= nq + nv:
                vt_ref[lo - nq:hi - nq, :] = t.astype(BF16)
            else:
                gnt_ref[...] = t
            return t
        return run

    pieces = ([key_piece(0), key_piece(1), cmp_piece] + [gate_piece(c) for c in range(0, 2 * D_MODEL, step)]
              + [t_piece(lo, lo + step) for lo in range(0, nq + nv, step)] + [t_piece(nq + nv, wt_ref.shape[0])])

    n_quarters, groups_per_quarter = 4, 8
    rq = tm // n_quarters
    slots = n_quarters * groups_per_quarter
    emitted = [0, 0]

    pending = []

    def between():
        if emitted[1] * slots < emitted[0] * len(pieces):
            done = pieces[emitted[1]]()
            emitted[1] += 1
            row = _zero_after(done[-8:, -LANES:])[0:1, :]
            pending.append((emitted[0] + ANCHOR_LAG, jnp.concatenate([row] * (CONV_CH // LANES), axis=1)))
        emitted[0] += 1
        zero = None
        while pending and (pending[0][0] < emitted[0] or emitted[0] == slots):
            z = pending.pop(0)[1]
            zero = z if zero is None else zero + z
        return zero

    for k in range(n_quarters):
        yq = _conformer_rows(z_ref, dww_ref, dwb_ref, lng_ref, lnb_ref, wp_ref, k * rq, rq, between)
        yc_ref[k * rq:(k + 1) * rq, :] = yq.astype(BF16)
    assert emitted[1] == len(pieces)


def _inproj(x2, g, wn, wt, aug, dww, dwb, lng, lnb, wp, seq):
    t = x2.shape[0]
    tm = ROW_TILE
    row = lambda n: pl.BlockSpec((tm, n), lambda i: (i, 0))
    col = lambda n: pl.BlockSpec((n, tm), lambda i: (0, i))
    full = lambda a: pl.BlockSpec(a.shape, lambda i: (0, 0))
    kw = N_KV_GROUPS * LANES
    nat = ((KV_WIDTH, F32), (KV_WIDTH, F32), (D_MODEL, BF16), (2 * D_MODEL, BF16))
    trn = ((ATTN_WIDTH, BF16), (2 * KV_WIDTH, BF16), (N_KV_GROUPS * GATE_ROWS, F32))
    return pl.pallas_call(
        functools.partial(_inproj_kernel, tiles_per_seq=seq // tm),
        grid=(t // tm,),
        in_specs=[row(D_MODEL), full(g), full(wn), full(wt),
                  pl.BlockSpec((tm, 2 * kw), lambda i: (i % (seq // tm), 0)),
                  full(dww), full(dwb), full(lng), full(lnb), full(wp)],
        out_specs=[pl.BlockSpec((2, tm, kw), lambda i: (0, i, 0))] + [row(n) for n, _ in nat]
                  + [col(n) for n, _ in trn],
        out_shape=[jax.ShapeDtypeStruct((2, t, kw), BF16)] + [jax.ShapeDtypeStruct((t, n), d) for n, d in nat]
                  + [jax.ShapeDtypeStruct((n, t), d) for n, d in trn],
        scratch_shapes=[pltpu.VMEM((tm + CONV_HALO, CONV_CH), F32), pltpu.VMEM((CONV_HALO, CONV_CH), F32)],
        compiler_params=_params(("arbitrary",)),
        name="inproj",
    )(x2, g, wn, wt, aug, dww, dwb, lng, lnb, wp)


def _gelu_tanh(x):
    return 0.5 * x * (1.0 + jnp.tanh(math.sqrt(2.0 / math.pi) * (x + 0.044715 * (x * x * x))))


def _compress(x_ref, pe_ref, w1_ref, w2_ref, nr):
    half = CMP_BLOCK // 2
    a = b = None
    for l in range(half):
        x = x_ref[pl.ds(l, nr, stride=CMP_STRIDE), :]
        ta = jnp.dot((x + pe_ref[l:l + 1, :]).astype(BF16), w1_ref[l], preferred_element_type=F32)
        tb = jnp.dot((x + pe_ref[half + l:half + l + 1, :]).astype(BF16), w1_ref[half + l],
                     preferred_element_type=F32)
        a = ta if a is None else a + ta
        b = tb if b is None else b + tb
    pre = a + pltpu.roll(b, nr - 1, 0)
    hid = _gelu_tanh(pre).astype(BF16)
    return jnp.dot(hid, w2_ref[...], preferred_element_type=F32)


def _cmp_kernel(kc_ref, vc_ref, pek_ref, pev_ref, w1k_ref, w2k_ref, w1v_ref, w2v_ref,
                qt_ref, mapt_ref, oct_ref, selb_ref, *, seq):
    ng, nh, dh = N_KV_GROUPS, HEADS_PER_GROUP, HEAD_DIM
    nr = seq // CMP_STRIDE
    kc = _compress(kc_ref, pek_ref, w1k_ref, w2k_ref, nr).astype(BF16)
    vct = jnp.transpose(_compress(vc_ref, pev_ref, w1v_ref, w2v_ref, nr)).astype(BF16)
    nsel = seq // SEL_BLOCK
    tq = CMP_Q_TILE
    c_end = lax.broadcasted_iota(jnp.int32, (nr, 1), 0) * CMP_STRIDE + (CMP_BLOCK - 1)
    blk = lax.broadcasted_iota(jnp.int32, (nsel, 1), 0)
    zeros_q = jnp.zeros((dh, tq), BF16)

    n_top = min(SEL_TOP, nsel)
    for ti in range(seq // tq):
        t0, t1 = ti * tq, (ti + 1) * tq
        nrv, nbv = t1 // CMP_STRIDE, t1 // SEL_BLOCK
        pos = t0 + lax.broadcasted_iota(jnp.int32, (1, tq), 1)
        dist = pos - c_end[:nrv]
        valid = dist >= 0
        distf = dist.astype(F32)
        cur = pos >> SEL_SHIFT
        blk_v = blk[:nbv]
        forced = (blk_v == 0) | (blk_v == cur) | (blk_v == cur - 1)
        causal = blk_v <= cur
        for g in range(ng):
            psum = jnp.zeros((nrv, tq), F32)
            for j in range(nh):
                slope = 2.0 ** -(g * nh + j + 1)
                rows = slice((g * nh + j) * dh, (g * nh + j + 1) * dh)
                q = qt_ref[rows, t0:t1]
                q2 = jnp.concatenate([q, zeros_q] if g == 0 else [zeros_q, q], axis=0)
                s = jnp.dot(kc[:nrv], q2, preferred_element_type=F32)
                s = jnp.where(valid, s - slope * distf, NEG_INF)
                m = jnp.max(s, axis=0, keepdims=True)
                e = jnp.where(valid, jnp.exp(s - m), 0.0)
                l = jnp.sum(e, axis=0, keepdims=True)
                p = e / jnp.where(l > 0.0, l, 1.0)
                psum = psum + p
                pfull = p.astype(BF16)
                if nrv < nr:
                    pfull = jnp.concatenate([pfull, jnp.zeros((nr - nrv, tq), BF16)], axis=0)
                oct_ref[rows, t0:t1] = jnp.dot(vct[g * dh:(g + 1) * dh, :], pfull, preferred_element_type=F32)
            if nrv < nr:
                psum = jnp.concatenate([psum, jnp.zeros((nr - nrv, tq), F32)], axis=0)
            imp = jnp.dot(mapt_ref[0:nbv, :], psum, precision=lax.Precision.HIGHEST, preferred_element_type=F32)
            val = jnp.where(causal, imp + jnp.where(forced, FORCE_BONUS, 0.0), NEG_INF)
            sub = 8
            cnts = []
            for c0 in range(0, nbv, sub):
                vc_ = val[c0:c0 + sub, :]
                cnt = jnp.zeros((sub, tq), F32)
                for i in range(nbv):
                    vi = val[i:i + 1, :]
                    if i < c0:
                        cnt = cnt + jnp.where(vi >= vc_, 1.0, 0.0)
                    elif i >= c0 + sub:
                        cnt = cnt + jnp.where(vi > vc_, 1.0, 0.0)
                    else:
                        tie = (blk_v[c0:c0 + sub] > i).astype(F32)
                        cnt = cnt + jnp.where(vi > vc_, 1.0, jnp.where(vi == vc_, tie, 0.0))
                cnts.append(cnt)
            cnt = jnp.concatenate(cnts, axis=0) if len(cnts) > 1 else cnts[0]
            selb = jnp.where(causal & (cnt < n_top), 0.0, UNSELECTED)
            if nbv < nsel:
                selb = jnp.concatenate([selb, jnp.full((nsel - nbv, tq), UNSELECTED, F32)], axis=0)
            selb_ref[0, g, :, t0:t1] = selb.astype(BF16)


def _cmp_branch(kc, vc, pek, pev, w1k, w2k, w1v, w2v, qt, mapt, batch, seq):
    nsel = seq // SEL_BLOCK
    full = lambda a: pl.BlockSpec(a.shape, lambda i: (0,) * a.ndim)
    rows = pl.BlockSpec((seq, KV_WIDTH), lambda i: (i, 0))
    heads = pl.BlockSpec((ATTN_WIDTH, seq), lambda i: (0, i))
    return pl.pallas_call(
        functools.partial(_cmp_kernel, seq=seq),
        grid=(batch,),
        in_specs=[rows, rows, full(pek), full(pev), full(w1k), full(w2k), full(w1v), full(w2v), heads, full(mapt)],
        out_specs=[heads, pl.BlockSpec((1, N_KV_GROUPS, nsel, seq), lambda i: (i, 0, 0, 0))],
        out_shape=[jax.ShapeDtypeStruct((ATTN_WIDTH, batch * seq), F32),
                   jax.ShapeDtypeStruct((batch, N_KV_GROUPS, nsel, seq), BF16)],
        compiler_params=_params(("parallel",)),
        name="cmp_topk",
    )(kc, vc, pek, pev, w1k, w2k, w1v, w2v, qt, mapt)


def _attn_kernel(qt_ref, k_ref, v_ref, selb_ref, oct_ref, gnt_ref, o_ref,
                 qa_ref, s_ref, c_ref, m_ref, l_ref, acc_ref, out_ref):
    qi = pl.program_id(1)
    ta = ATTN_TILE
    ng, nh, dh = N_KV_GROUPS, HEADS_PER_GROUP, HEAD_DIM
    nsel = selb_ref.shape[2]
    groups = range(ng)
    pos = qi * ta + lax.broadcasted_iota(jnp.int32, (1, ta), 1)
    tb = (pos >> SEL_SHIFT).astype(F32)
    tr = (pos & (SEL_BLOCK - 1)).astype(F32)
    arow = lax.broadcasted_iota(jnp.int32, (16, 1), 0)
    gate = _sigmoid(gnt_ref[...])

    def gate_row(g, c):
        r = g * GATE_ROWS + c
        return jnp.concatenate([gate[r + 3 * j:r + 3 * j + 1, :] for j in range(nh)], axis=1)

    for g in groups:
        for j in range(nh):
            slope = 2.0 ** -(g * nh + j + 1)
            cols = slice(j * ta, (j + 1) * ta)
            rows = slice((g * nh + j) * dh, (g * nh + j + 1) * dh)
            qa_ref[g, 0:dh, cols] = qt_ref[rows, :]
            qa_ref[g, dh:dh + nsel, cols] = selb_ref[0, g]
            alibi = jnp.where(arow == 0, slope * SEL_BLOCK,
                              jnp.where(arow == 1, slope,
                                        jnp.where(arow == 2, -slope * SEL_BLOCK * tb,
                                                  jnp.where(arow == 3, -slope * tr, 0.0))))
            qa_ref[g, dh + nsel:dh + nsel + 16, cols] = alibi.astype(BF16)
            qa_ref[g, dh + nsel + 16:, cols] = jnp.zeros((LANES - dh - nsel - 16, ta), BF16)
            out_ref[g, :, cols] = gate[g * GATE_ROWS + 3 * j:g * GATE_ROWS + 3 * j + 1, :] * oct_ref[rows, :]

    krow = lax.broadcasted_iota(jnp.int32, (ta, 1), 0)
    qcol = lax.broadcasted_iota(jnp.int32, (1, ta), 1)
    lanes4 = lambda b: jnp.concatenate([b] * nh, axis=1)

    def qk(g, br, kt):
        k = k_ref[br, pl.ds(pl.multiple_of(kt * ta, ta), ta), g * LANES:(g + 1) * LANES]
        return jnp.dot(k, qa_ref[g], preferred_element_type=F32)

    def pv(g, br, kt, p):
        v = v_ref[br, g, :, pl.ds(pl.multiple_of(kt * ta, ta), ta)]
        return jnp.dot(v, p.astype(BF16), preferred_element_type=F32)

    far = WINDOW // ta
    causal = lanes4(jnp.where(qcol >= krow, 0.0, NEG_INF))
    kt_far = jnp.maximum(qi - far, 0)
    beyond = lanes4(jnp.where((qcol < krow) & (qi >= far), 0.0, NEG_INF))
    m_ref[...] = jnp.full(m_ref.shape, NEG_INF, F32)
    l_ref[...] = jnp.zeros(l_ref.shape, F32)
    acc_ref[...] = jnp.zeros(acc_ref.shape, F32)
    n_tasks = qi + 1

    def task(n):
        n = jnp.minimum(n, n_tasks - 1)
        return jnp.where(n == 0, 1, 0), qi - jnp.maximum(n, 1)

    def scores_of(br, kt, slot, bias=None):
        for g in groups:
            s = qk(g, br, kt)
            if bias is not None:
                s = s + bias
            s_ref[g, slot] = s
            c_ref[g, slot] = jnp.max(s, axis=0, keepdims=True)

    def absorb_of(br, kt, slot):
        for c in range(nh):
            cols = slice(c * ta, (c + 1) * ta)
            for g in groups:
                m_old = m_ref[g, br, :, cols]
                m_new = jnp.maximum(m_old, c_ref[g, slot, :, cols])
                alpha = jnp.exp(m_old - m_new)
                p = jnp.exp(s_ref[g, slot, :, cols] - m_new)
                l_ref[g, br, :, cols] = alpha * l_ref[g, br, :, cols] + jnp.sum(p, axis=0, keepdims=True)
                m_ref[g, br, :, cols] = m_new
                acc_ref[g, br, :, cols] = alpha * acc_ref[g, br, :, cols] + pv(g, br, kt, p)

    scores = lambda n, slot: scores_of(*task(n), slot)
    absorb = lambda n, slot: absorb_of(*task(n), slot)

    scores_of(0, qi, 0, causal)
    scores_of(1, qi, 1, causal)
    absorb_of(0, qi, 0)
    scores_of(1, kt_far, 0, beyond)
    absorb_of(1, qi, 1)

    @pl.when(qi == 0)
    def _():
        absorb_of(1, kt_far, 0)

    @pl.when(qi >= 1)
    def _():
        scores(0, 1)
        absorb_of(1, kt_far, 0)

        def pair(i, carry):
            n = 2 * i
            scores(n + 1, 0)
            absorb(n, 1)
            scores(n + 2, 1)
            absorb(n + 1, 0)
            return carry

        lax.fori_loop(0, n_tasks // 2, pair, 0)

        @pl.when(n_tasks % 2 == 1)
        def _():
            absorb(n_tasks - 1, 1)

    for g in groups:
        out = (out_ref[g] + gate_row(g, 1) * (acc_ref[g, 0] / l_ref[g, 0])
               + gate_row(g, 2) * (acc_ref[g, 1] / l_ref[g, 1]))
        for j in range(nh):
            rows = slice((g * nh + j) * dh, (g * nh + j + 1) * dh)
            o_ref[rows, :] = out[:, j * ta:(j + 1) * ta].astype(BF16)


def _attention(qt, kall, vt, selb, oct, gnt, batch, seq):
    ta = ATTN_TILE
    assert WINDOW % ta == 0
    nqt = seq // ta
    ng, nh, dh = N_KV_GROUPS, HEADS_PER_GROUP, HEAD_DIM
    nsel = selb.shape[2]
    qtile = lambda rows: pl.BlockSpec((rows, ta), lambda b, t: (0, b * nqt + t))
    vall = vt.reshape(2, ng, dh, batch * seq)
    wide = nh * ta
    return pl.pallas_call(
        _attn_kernel,
        grid=(batch, nqt),
        in_specs=[qtile(ATTN_WIDTH),
                  pl.BlockSpec((2, seq, ng * LANES), lambda b, t: (0, b, 0)),
                  pl.BlockSpec((2, ng, dh, seq), lambda b, t: (0, 0, 0, b)),
                  pl.BlockSpec((1, ng, nsel, ta), lambda b, t: (b, 0, 0, t)),
                  qtile(ATTN_WIDTH), qtile(ng * GATE_ROWS)],
        out_specs=qtile(ATTN_WIDTH),
        out_shape=jax.ShapeDtypeStruct((ATTN_WIDTH, batch * seq), BF16),
        scratch_shapes=[pltpu.VMEM((ng, LANES, wide), BF16),
                        pltpu.VMEM((ng, 2, ta, wide), F32),
                        pltpu.VMEM((ng, 2, 1, wide), F32),
                        pltpu.VMEM((ng, 2, 1, wide), F32),
                        pltpu.VMEM((ng, 2, 1, wide), F32),
                        pltpu.VMEM((ng, 2, dh, wide), F32),
                        pltpu.VMEM((ng, dh, wide), F32)],
        compiler_params=_params(("parallel", "arbitrary")),
        name="nsa_flash",
    )(qt, kall, vall, selb, oct, gnt)


def _merge_kernel(ot_ref, yc_ref, gm_ref, x_ref, wa_ref, wo_ref, out_ref):
    ya = _dot_tn(ot_ref[...], wa_ref[...])
    gm = gm_ref[...].astype(F32)
    mix = _sigmoid(gm[:, :D_MODEL]) * ya + _sigmoid(gm[:, D_MODEL:]) * yc_ref[...].astype(F32)
    out_ref[...] = x_ref[...] + jnp.dot(mix.astype(BF16), wo_ref[...], preferred_element_type=F32)


def _merge(ot, yc2, gm2, x2, wa, wo):
    t = x2.shape[0]
    tm = ROW_TILE
    row = lambda n: pl.BlockSpec((tm, n), lambda i: (i, 0))
    full = lambda a: pl.BlockSpec(a.shape, lambda i: (0, 0))
    return pl.pallas_call(
        _merge_kernel,
        grid=(t // tm,),
        in_specs=[pl.BlockSpec((ATTN_WIDTH, tm), lambda i: (0, i)), row(D_MODEL), row(2 * D_MODEL), row(D_MODEL),
                  full(wa), full(wo)],
        out_specs=row(D_MODEL),
        out_shape=jax.ShapeDtypeStruct((t, D_MODEL), F32),
        compiler_params=_params(("parallel",)),
        name="merge_outproj",
    )(ot, yc2, gm2, x2, wa, wo)


def _ffn_kernel(x_ref, xh_ref, g_ref, wup_ref, cw_ref, cb_ref, wd_ref, *rest, tiles_per_seq, final_norm):
    if final_norm:
        fg_ref, out_ref, h_ref, a_ref, v_ref, gt_ref = rest
    else:
        out_ref, h_ref, a_ref, v_ref, gt_ref = rest
    tm, halo, tf = FFN_ROW_TILE, FFN_HALO, FFN_COL_TILE
    seq_start = pl.program_id(0) % tiles_per_seq == 0
    hh = _rms(xh_ref[...], g_ref[...])
    h_ref[0:halo, :] = jnp.where(seq_start, 0.0, hh).astype(BF16)
    h_ref[halo:, :] = _rms(x_ref[...], g_ref[...]).astype(BF16)
    h = h_ref[...]

    def conv(ref, w, b):
        out = b
        for k in range(FFN_CONV_WIDTH):
            off = halo - (FFN_CONV_WIDTH - 1) + k
            out = out + w[k:k + 1, :] * ref[off:off + tm, :]
        return out

    for j in range(FFN_DIM // tf):
        slot = j % a_ref.shape[0]
        ca_cols = slice(j * tf, (j + 1) * tf)
        cv_cols = slice(FFN_DIM + j * tf, FFN_DIM + (j + 1) * tf)
        a_ref[slot] = jnp.dot(h, wup_ref[:, ca_cols], preferred_element_type=F32)
        v_ref[slot] = jnp.dot(h, wup_ref[:, cv_cols], preferred_element_type=F32)
        ca = conv(a_ref.at[slot], cw_ref[:, ca_cols], cb_ref[:, ca_cols])
        cv = conv(v_ref.at[slot], cw_ref[:, cv_cols], cb_ref[:, cv_cols])
        gt_ref[:, ca_cols] = (ca * _sigmoid(ca) * cv).astype(BF16)
    y = x_ref[...] + jnp.dot(gt_ref[...], wd_ref[...], preferred_element_type=F32)
    out_ref[...] = _rms(y, fg_ref[...]) if final_norm else y


def _ffn(x2, g, wup, cw, cb, wd, seq, final_g=None):
    t = x2.shape[0]
    tm, halo, tf = FFN_ROW_TILE, FFN_HALO, FFN_COL_TILE
    assert FFN_DIM % tf == 0
    slots = min(2, FFN_DIM // tf)
    once = lambda a: pl.BlockSpec(a.shape, lambda i: (0,) * a.ndim, pipeline_mode=pl.Buffered(1))
    args = [x2, x2, g, wup, cw, cb, wd] + ([final_g] if final_g is not None else [])
    return pl.pallas_call(
        functools.partial(_ffn_kernel, tiles_per_seq=seq // tm, final_norm=final_g is not None),
        grid=(t // tm,),
        in_specs=[pl.BlockSpec((tm, D_MODEL), lambda i: (i, 0)),
                  pl.BlockSpec((halo, D_MODEL), lambda i: (jnp.maximum(i * (tm // halo) - 1, 0), 0))]
                 + [once(a) for a in args[2:]],
        out_specs=pl.BlockSpec((tm, D_MODEL), lambda i: (i, 0)),
        out_shape=jax.ShapeDtypeStruct((t, D_MODEL), F32),
        scratch_shapes=[pltpu.VMEM((tm + halo, D_MODEL), BF16), pltpu.VMEM((slots, tm + halo, tf), F32),
                        pltpu.VMEM((slots, tm + halo, tf), F32), pltpu.VMEM((tm, FFN_DIM), BF16)],
        compiler_params=_params(("parallel",)),
        name="conv_ffn",
    )(*args)


def _sel_map_t(seq):
    ncmp = (seq - CMP_BLOCK) // CMP_STRIDE + 1
    nr = seq // CMP_STRIDE
    nsel = seq // SEL_BLOCK
    cs = np.arange(ncmp) * CMP_STRIDE
    ce = cs + CMP_BLOCK - 1
    ss = np.arange(nsel) * SEL_BLOCK
    se = ss + SEL_BLOCK - 1
    ov = np.minimum(ce[:, None], se[None, :]) - np.maximum(cs[:, None], ss[None, :]) + 1
    m = np.zeros((nsel, nr), np.float32)
    m[:, :ncmp] = (np.clip(ov, 0, None).astype(np.float32) / CMP_BLOCK).T
    return jnp.asarray(m)


def _key_aug(seq):
    pos = np.arange(seq)
    nsel = seq // SEL_BLOCK
    a = np.zeros((seq, AUG), np.float32)
    a[:, AUG_POS + 0] = pos // SEL_BLOCK
    a[:, AUG_POS + 1] = pos % SEL_BLOCK
    a[:, AUG_POS + 2] = 1.0
    a[:, AUG_POS + 3] = 1.0
    s = a.copy()
    s[pos, pos // SEL_BLOCK] = 1.0
    assert nsel <= AUG_POS
    z = np.zeros((seq, HEAD_DIM), np.float32)
    return jnp.asarray(np.concatenate([z, s, z, s, z, a, z, a], axis=1))


def _layer(x2, batch, seq, p, final_g):
    kall, kc, vc, yc, gm2, qt, vt, gnt = _inproj(x2, p["norm1_g"], p["wn"], p["wt"], _key_aug(seq), p["dww"],
                                                 p["dwb"], p["lng"], p["lnb"], p["wconv"], seq)
    oct, selb = _cmp_branch(kc, vc, p["pek"], p["pev"], p["w1k"], p["w2k"], p["w1v"], p["w2v"], qt,
                            _sel_map_t(seq), batch, seq)
    ot = _attention(qt, kall, vt, selb, oct, gnt, batch, seq)
    x2 = _merge(ot, yc, gm2, x2, p["wattn"], p["wout"])
    return _ffn(x2, p["norm2_g"], p["wup"], p["cw"], p["cb"], p["wd"], seq, final_g)


def _prep_layer(l, norm1_g, w_in, cmp_pe_k, cmp_pe_v, cmp_k_w1, cmp_k_w2, cmp_v_w1, cmp_v_w2, w_attn_br,
                conv_dw_w, conv_dw_b, conv_ln_g, conv_ln_b, w_conv_br, w_out, norm2_g, ffn_w_up, ffn_dw_w,
                ffn_dw_b, ffn_w_down):
    w = w_in[l].astype(BF16)
    kvw = KV_WIDTH
    c_q = ATTN_WIDTH
    c_kc, c_vc, c_ks, c_vs, c_kw, c_vw = (c_q + i * kvw for i in range(6))
    c_gn = c_q + 6 * kvw
    c_uc = c_gn + N_GATES
    c_gm = c_uc + 2 * CONV_CH

    def widen(cols):
        k = cols.reshape(D_MODEL, N_KV_GROUPS, HEAD_DIM)
        return jnp.pad(k, ((0, 0), (0, 0), (0, AUG))).reshape(D_MODEL, N_KV_GROUPS * LANES)

    wn = jnp.concatenate([widen(w[:, c_ks:c_ks + kvw]), widen(w[:, c_kw:c_kw + kvw]),
                          w[:, c_kc:c_kc + 2 * kvw], w[:, c_uc:c_gm], w[:, c_gm:]], axis=1)
    gates = w[:, c_gn:c_uc].reshape(D_MODEL, N_KV_GROUPS, 3 * HEADS_PER_GROUP)
    gates = jnp.pad(gates, ((0, 0), (0, 0), (0, GATE_ROWS - 3 * HEADS_PER_GROUP)))
    wt = jnp.concatenate([w[:, :c_q], w[:, c_vs:c_vs + kvw], w[:, c_vw:c_vw + kvw],
                          gates.reshape(D_MODEL, N_KV_GROUPS * GATE_ROWS)], axis=1).T

    def per_group(a):
        z = jnp.zeros_like(a)
        return jnp.concatenate([jnp.concatenate([a, z], axis=-1), jnp.concatenate([z, a], axis=-1)], axis=-2)

    assert N_KV_GROUPS == 2
    w1 = lambda a: per_group(a.astype(BF16).reshape(CMP_BLOCK, HEAD_DIM, CMP_HIDDEN))
    pe = lambda a: jnp.concatenate([a] * N_KV_GROUPS, axis=1)
    return dict(
        norm1_g=norm1_g[l][None, :], wn=wn, wt=wt,
        pek=pe(cmp_pe_k[l]), pev=pe(cmp_pe_v[l]),
        w1k=w1(cmp_k_w1[l]), w2k=per_group(cmp_k_w2[l].astype(BF16)),
        w1v=w1(cmp_v_w1[l]), w2v=per_group(cmp_v_w2[l].astype(BF16)),
        wattn=w_attn_br[l].astype(BF16),
        dww=conv_dw_w[l], dwb=conv_dw_b[l][None, :], lng=conv_ln_g[l][None, :], lnb=conv_ln_b[l][None, :],
        wconv=w_conv_br[l].astype(BF16), wout=w_out[l].astype(BF16),
        norm2_g=norm2_g[l][None, :],
        wup=ffn_w_up[l].astype(BF16), cw=ffn_dw_w[l], cb=ffn_dw_b[l][None, :],
        wd=ffn_w_down[l].astype(BF16),
    )


def kernel(x, norm1_g, w_in, cmp_pe_k, cmp_pe_v, cmp_k_w1, cmp_k_w2, cmp_v_w1, cmp_v_w2, w_attn_br, conv_dw_w, conv_dw_b, conv_ln_g, conv_ln_b, w_conv_br, w_out, norm2_g, ffn_w_up, ffn_dw_w, ffn_dw_b, ffn_w_down, final_g):
    batch, seq, d = x.shape
    assert d == D_MODEL and seq % ROW_TILE == 0 and seq % FFN_ROW_TILE == 0 and seq % CMP_Q_TILE == 0
    assert seq // SEL_BLOCK == AUG_POS
    x2 = x.reshape(batch * seq, d)
    for l in range(w_in.shape[0]):
        p = _prep_layer(l, norm1_g, w_in, cmp_pe_k, cmp_pe_v, cmp_k_w1, cmp_k_w2, cmp_v_w1, cmp_v_w2, w_attn_br,
                        conv_dw_w, conv_dw_b, conv_ln_g, conv_ln_b, w_conv_br, w_out, norm2_g, ffn_w_up,
                        ffn_dw_w, ffn_dw_b, ffn_w_down)
        last = l == w_in.shape[0] - 1
        x2 = _layer(x2, batch, seq, p, final_g[None, :] if last else None)
    return x2.reshape(batch, seq, d)
```
